```python
import jax, jax.numpy as jnp
from jax import lax
import numpy as np

D_MODEL = 2048
BATCH = 8
SEQ = 4096
DEPTH = 2

N_A_LAYERS = DEPTH // 2
N_B_LAYERS = DEPTH - N_A_LAYERS

SSD_EXPAND = 2
SSD_D_INNER = SSD_EXPAND * D_MODEL
SSD_HEAD_DIM = 64
SSD_N_HEADS = SSD_D_INNER // SSD_HEAD_DIM
SSD_N_GROUPS = 8
SSD_HEADS_PER_GROUP = SSD_N_HEADS // SSD_N_GROUPS
SSD_D_STATE = 128
SSD_CONV_W = 4
SSD_CHUNK = 256
SSD_BC_DIM = SSD_N_GROUPS * SSD_D_STATE
SSD_CONV_DIM = SSD_D_INNER + 2 * SSD_BC_DIM
SSD_IN_DIM = SSD_D_INNER + SSD_CONV_DIM + SSD_N_HEADS

DIL_PATTERNS = ((128, 1), (512, 4), (2048, 16))
DIL_N_GROUPS = len(DIL_PATTERNS)
DIL_HEADS = 8
DIL_HEAD_DIM = 128
DIL_Q_WIDTH = DIL_N_GROUPS * DIL_HEADS * DIL_HEAD_DIM
DIL_OUT_WIDTH = DIL_HEADS * DIL_HEAD_DIM
DIL_IN_DIM = DIL_Q_WIDTH + DIL_OUT_WIDTH
DIL_KV_DIM = 2 * DIL_Q_WIDTH
DIL_BLOCK = 128

DEEPNORM_ALPHA = (2 * DEPTH) ** 0.25
DEEPNORM_BETA = (8 * DEPTH) ** -0.25
LN_EPS = 1e-5
RMS_EPS = 1e-5

kernel_name = "hybrid_yoco_ssd_dilated_alibi_deepnorm"


def _layer_norm(x, g, b):
    xf = x.astype(jnp.float32)
    mu = jnp.mean(xf, -1, keepdims=True)
    var = jnp.mean(jnp.square(xf - mu), -1, keepdims=True)
    return ((xf - mu) * lax.rsqrt(var + LN_EPS)).astype(x.dtype) * g + b


def _adaln(c, w, b):
    mod = jax.nn.silu(c) @ w + b
    shift, scale, gate = jnp.split(mod, 3, axis=-1)
    return shift[:, None, :], scale[:, None, :], gate[:, None, :]


def _causal_depthwise_conv(x, w, b):
    y = lax.conv_general_dilated(
        x, w[:, None, :], window_strides=(1,), padding=[(SSD_CONV_W - 1, 0)],
        dimension_numbers=("NWC", "WIO", "NWC"), feature_group_count=x.shape[-1])
    return y + b


def _ssd_chunked(xdt, dtA, Bm, Cm):
    f32 = jnp.float32
    bsz, L = xdt.shape[:2]
    G, K, P, N = SSD_N_GROUPS, SSD_HEADS_PER_GROUP, SSD_HEAD_DIM, SSD_D_STATE
    Lp = -(-L // SSD_CHUNK) * SSD_CHUNK
    nc = Lp // SSD_CHUNK

    def chunks(a):
        a = jnp.pad(a, [(0, 0), (0, Lp - L)] + [(0, 0)] * (a.ndim - 2))
        a = a.reshape((bsz, nc, SSD_CHUNK) + a.shape[2:])
        return jnp.moveaxis(a, 1, 0)

    xs = chunks(xdt.reshape(bsz, L, G, K, P).astype(f32))
    As = chunks(dtA.reshape(bsz, L, G, K).astype(f32))
    Bs = chunks(Bm.astype(f32))
    Cs = chunks(Cm.astype(f32))
    causal = jnp.tril(jnp.ones((SSD_CHUNK, SSD_CHUNK), bool))[None, :, :, None, None]

    def step(state, inp):
        xc, ac, bc, cc = inp
        acum = jnp.cumsum(ac, axis=1)
        seg = acum[:, :, None] - acum[:, None, :]
        decay = jnp.exp(jnp.where(causal, seg, -jnp.inf))
        cb = jnp.einsum("blgn,bsgn->blsg", cc, bc)
        y_diag = jnp.einsum("blsgk,bsgkp->blgkp", cb[..., None] * decay, xc)
        y_off = jnp.einsum("blgn,bgkpn->blgkp", cc, state) * jnp.exp(acum)[..., None]
        tail = jnp.exp(acum[:, -1:] - acum)
        new_state = (state * jnp.exp(acum[:, -1])[..., None, None]
                     + jnp.einsum("bsgn,bsgkp->bgkpn", bc, xc * tail[..., None]))
        return new_state, y_diag + y_off

    state0 = jnp.zeros((bsz, G, K, P, N), f32)
    _, ys = lax.scan(step, state0, (xs, As, Bs, Cs))
    ys = jnp.moveaxis(ys, 0, 1).reshape(bsz, Lp, SSD_N_HEADS * P)
    return ys[:, :L]


def _ssd_mixer(h, in_w, conv_w, conv_b, dt_bias, A_log, D_skip, norm_g, out_w):
    bsz, L, _ = h.shape
    proj = h @ in_w
    z, xBC, dt = jnp.split(proj, [SSD_D_INNER, SSD_D_INNER + SSD_CONV_DIM], axis=-1)
    xBC = jax.nn.silu(_causal_depthwise_conv(xBC, conv_w, conv_b))
    xs, Bm, Cm = jnp.split(xBC, [SSD_D_INNER, SSD_D_INNER + SSD_BC_DIM], axis=-1)
    xs = xs.reshape(bsz, L, SSD_N_HEADS, SSD_HEAD_DIM)
    Bm = Bm.reshape(bsz, L, SSD_N_GROUPS, SSD_D_STATE)
    Cm = Cm.reshape(bsz, L, SSD_N_GROUPS, SSD_D_STATE)
    dt = jax.nn.softplus((dt + dt_bias).astype(jnp.float32))
    A = -jnp.exp(A_log.astype(jnp.float32))
    y = _ssd_chunked(xs * dt[..., None], dt * A, Bm, Cm)
    y = y + (xs * D_skip[:, None]).reshape(bsz, L, SSD_D_INNER)
    y = y * jax.nn.silu(z.astype(jnp.float32))
    yg = y.reshape(bsz, L, SSD_N_GROUPS, -1)
    yg = yg * lax.rsqrt(jnp.mean(jnp.square(yg), -1, keepdims=True) + RMS_EPS)
    y = yg.reshape(bsz, L, SSD_D_INNER).astype(h.dtype) * norm_g
    return y @ out_w


def _alibi_slopes():
    n = DIL_N_GROUPS * DIL_HEADS
    s = 2.0 ** (-8.0 * np.arange(1, n + 1) / n)
    return jnp.asarray(s.reshape(DIL_N_GROUPS, DIL_HEADS), dtype=jnp.float32)


def _dilated_window_attention(q, k, v, window, dilation, slopes):
    f32 = jnp.float32
    bsz, L, H, E = q.shape
    span = window // dilation
    M = -(-L // (dilation * DIL_BLOCK)) * DIL_BLOCK
    nb = M // DIL_BLOCK
    pad = M * dilation - L

    def to_blocks(a):
        a = jnp.pad(a, [(0, 0), (0, pad), (0, 0), (0, 0)])
        a = a.reshape(bsz, M, dilation, H, E).transpose(0, 2, 1, 3, 4)
        return a.reshape(bsz, dilation, nb, DIL_BLOCK, H, E)

    def with_prev(a):
        prev = jnp.pad(a[:, :, :-1], [(0, 0), (0, 0), (1, 0), (0, 0), (0, 0), (0, 0)])
        return jnp.concatenate([prev, a], axis=3)

    qb = to_blocks(q)
    kb = with_prev(to_blocks(k))
    vb = with_prev(to_blocks(v))
    s = jnp.einsum("brnqhe,brnkhe->brnhqk", qb, kb,
                   preferred_element_type=f32) * (E ** -0.5)
    qi = jnp.arange(DIL_BLOCK)[:, None]
    kj = jnp.arange(2 * DIL_BLOCK)[None, :]
    delta = qi + DIL_BLOCK - kj
    valid = (delta >= 0) & (delta <= span)
    first = (jnp.arange(nb) == 0)[:, None, None]
    valid = valid[None] & ~(first & (kj < DIL_BLOCK)[None])
    alibi = -slopes[:, None, None] * (delta * dilation).astype(f32)[None]
    s = jnp.where(valid[None, None, :, None], s + alibi[None, None, None], -jnp.inf)
    m = jnp.max(s, -1, keepdims=True)
    p = jnp.exp(s - m)
    den = jnp.sum(p, -1)
    o = jnp.einsum("brnhqk,brnkhe->brnqhe", p, vb.astype(f32))
    o = o / jnp.moveaxis(den, 3, 4)[..., None]
    lse = jnp.moveaxis(m[..., 0] + jnp.log(den), 3, 4)

    def from_blocks(a):
        a = a.reshape((bsz, dilation, M) + a.shape[4:])
        a = jnp.moveaxis(a, 1, 2).reshape((bsz, M * dilation) + a.shape[3:])
        return a[:, :L]

    return from_blocks(o), from_blocks(lse)


def _shared_kv(x, kv_w):
    bsz, L, _ = x.shape
    k, v = jnp.split(x @ kv_w, 2, axis=-1)
    shp = (bsz, L, DIL_N_GROUPS, DIL_HEADS, DIL_HEAD_DIM)
    return k.reshape(shp), v.reshape(shp)


def _dilated_mixer(h, k_sh, v_sh, in_w, out_w):
    bsz, L, _ = h.shape
    q, z = jnp.split(h @ in_w, [DIL_Q_WIDTH], axis=-1)
    q = q.reshape(bsz, L, DIL_N_GROUPS, DIL_HEADS, DIL_HEAD_DIM)
    slopes = _alibi_slopes()
    outs, lses = [], []
    for g, (window, dilation) in enumerate(DIL_PATTERNS):
        o, lse = _dilated_window_attention(q[:, :, g], k_sh[:, :, g], v_sh[:, :, g],
                                           window, dilation, slopes[g])
        outs.append(o)
        lses.append(lse)
    o = jnp.stack(outs, 2)
    wts = jax.nn.softmax(jnp.stack(lses, 2), axis=2)
    o = jnp.einsum("blghe,blgh->blhe", o, wts).reshape(bsz, L, DIL_OUT_WIDTH)
    o = o.astype(h.dtype) * jax.nn.silu(z)
    return o @ out_w


def _fwd_setup_inputs(seed: int = 0) -> dict:
    key = jax.random.key(seed)
    ks = jax.random.split(key, 20)
    f32 = jnp.float32
    nA, nB, D = N_A_LAYERS, N_B_LAYERS, D_MODEL
    nrm = lambda k, shp, sc: jax.random.normal(k, shp, f32) * sc
    dt0 = jnp.exp(jax.random.uniform(ks[8], (nA, SSD_N_HEADS), f32,
                                     np.log(1e-3), np.log(1e-1)))
    return {
        "x": nrm(ks[0], (BATCH, SEQ, D), 1.0),
        "c": nrm(ks[1], (BATCH, D), 1.0),
        "ada_w": nrm(ks[2], (DEPTH, D, 3 * D), 0.1 * D ** -0.5),
        "ada_b": nrm(ks[3], (DEPTH, 3 * D), 0.01),
        "ln_g": 1.0 + nrm(ks[4], (DEPTH, D), 0.01),
        "ln_b": nrm(ks[5], (DEPTH, D), 0.01),
        "a_in_w": nrm(ks[6], (nA, D, SSD_IN_DIM), D ** -0.5),
        "a_conv_w": nrm(ks[7], (nA, SSD_CONV_W, SSD_CONV_DIM), SSD_CONV_W ** -0.5),
        "a_conv_b": nrm(ks[9], (nA, SSD_CONV_DIM), 0.01),
        "a_dt_bias": dt0 + jnp.log(-jnp.expm1(-dt0)),
        "a_A_log": jnp.log(jax.random.uniform(ks[10], (nA, SSD_N_HEADS), f32, 1.0, 16.0)),
        "a_D": 1.0 + nrm(ks[11], (nA, SSD_N_HEADS), 0.01),
        "a_norm_g": 1.0 + nrm(ks[12], (nA, SSD_D_INNER), 0.01),
        "a_out_w": nrm(ks[13], (nA, SSD_D_INNER, D), DEEPNORM_BETA * SSD_D_INNER ** -0.5),
        "kv_w": nrm(ks[14], (D, DIL_KV_DIM), D ** -0.5),
        "b_in_w": nrm(ks[15], (nB, D, DIL_IN_DIM), D ** -0.5),
        "b_out_w": nrm(ks[16], (nB, DIL_OUT_WIDTH, D), DEEPNORM_BETA * DIL_OUT_WIDTH ** -0.5),
    }


def _fwd_reference(x, c, ada_w, ada_b, ln_g, ln_b, a_in_w, a_conv_w, a_conv_b, a_dt_bias,
              a_A_log, a_D, a_norm_g, a_out_w, kv_w, b_in_w, b_out_w):
    k_sh, v_sh = None, None
    for layer in range(DEPTH):
        shift, scale, gate = _adaln(c, ada_w[layer], ada_b[layer])
        h = x * (1.0 + scale) + shift
        if layer < N_A_LAYERS:
            i = layer
            y = _ssd_mixer(h, a_in_w[i], a_conv_w[i], a_conv_b[i], a_dt_bias[i],
                           a_A_log[i], a_D[i], a_norm_g[i], a_out_w[i])
        else:
            i = layer - N_A_LAYERS
            y = _dilated_mixer(h, k_sh, v_sh, b_in_w[i], b_out_w[i])
        x = _layer_norm(DEEPNORM_ALPHA * x + (1.0 + gate) * y, ln_g[layer], ln_b[layer])
        if layer == N_A_LAYERS - 1:
            k_sh, v_sh = _shared_kv(x, kv_w)
    return x


import jax as _jax
import jax.numpy as _jnp

TWIN_FORMAT = 'train_step'
FWD_PARAMS = ['x', 'c', 'ada_w', 'ada_b', 'ln_g', 'ln_b', 'a_in_w', 'a_conv_w', 'a_conv_b', 'a_dt_bias', 'a_A_log', 'a_D', 'a_norm_g', 'a_out_w', 'kv_w', 'b_in_w', 'b_out_w']
TWIN_WEIGHTS = ['ada_w', 'ada_b', 'ln_g', 'ln_b', 'a_in_w', 'a_conv_w', 'a_conv_b', 'a_dt_bias', 'a_A_log', 'a_D', 'a_norm_g', 'a_out_w', 'kv_w', 'b_in_w', 'b_out_w']
TWIN_DIFF_INPUT = 'x'
TWIN_INPUTS = ['x', 'c', 'ada_w', 'ada_b', 'ln_g', 'ln_b', 'a_in_w', 'a_conv_w', 'a_conv_b', 'a_dt_bias', 'a_A_log', 'a_D', 'a_norm_g', 'a_out_w', 'kv_w', 'b_in_w', 'b_out_w', 'loss_target', 'm_ada_w', 'm_ada_b', 'm_ln_g', 'm_ln_b', 'm_a_in_w', 'm_a_conv_w', 'm_a_conv_b', 'm_a_dt_bias', 'm_a_A_log', 'm_a_D', 'm_a_norm_g', 'm_a_out_w', 'm_kv_w', 'm_b_in_w', 'm_b_out_w', 'v_ada_w', 'v_ada_b', 'v_ln_g', 'v_ln_b', 'v_a_in_w', 'v_a_conv_w', 'v_a_conv_b', 'v_a_dt_bias', 'v_a_A_log', 'v_a_D', 'v_a_norm_g', 'v_a_out_w', 'v_kv_w', 'v_b_in_w', 'v_b_out_w']
TWIN_OUTPUTS = ['loss', 'grad_x', 'grad_ada_w', 'grad_ada_b', 'grad_ln_g', 'grad_ln_b', 'grad_a_in_w', 'grad_a_conv_w', 'grad_a_conv_b', 'grad_a_dt_bias', 'grad_a_A_log', 'grad_a_D', 'grad_a_norm_g', 'grad_a_out_w', 'grad_kv_w', 'grad_b_in_w', 'grad_b_out_w', 'delta_ada_w', 'delta_ada_b', 'delta_ln_g', 'delta_ln_b', 'delta_a_in_w', 'delta_a_conv_w', 'delta_a_conv_b', 'delta_a_dt_bias', 'delta_a_A_log', 'delta_a_D', 'delta_a_norm_g', 'delta_a_out_w', 'delta_kv_w', 'delta_b_in_w', 'delta_b_out_w', 'new_m_ada_w', 'new_m_ada_b', 'new_m_ln_g', 'new_m_ln_b', 'new_m_a_in_w', 'new_m_a_conv_w', 'new_m_a_conv_b', 'new_m_a_dt_bias', 'new_m_a_A_log', 'new_m_a_D', 'new_m_a_norm_g', 'new_m_a_out_w', 'new_m_kv_w', 'new_m_b_in_w', 'new_m_b_out_w', 'new_v_ada_w', 'new_v_ada_b', 'new_v_ln_g', 'new_v_ln_b', 'new_v_a_in_w', 'new_v_a_conv_w', 'new_v_a_conv_b', 'new_v_a_dt_bias', 'new_v_a_A_log', 'new_v_a_D', 'new_v_a_norm_g', 'new_v_a_out_w', 'new_v_kv_w', 'new_v_b_in_w', 'new_v_b_out_w']
TWIN_LEAF_KINDS = {'loss': 'loss', 'grad_x': 'grad_x', 'grad_ada_w': 'grad_w', 'grad_ada_b': 'grad_w', 'grad_ln_g': 'grad_w', 'grad_ln_b': 'grad_w', 'grad_a_in_w': 'grad_w', 'grad_a_conv_w': 'grad_w', 'grad_a_conv_b': 'grad_w', 'grad_a_dt_bias': 'grad_w', 'grad_a_A_log': 'grad_w', 'grad_a_D': 'grad_w', 'grad_a_norm_g': 'grad_w', 'grad_a_out_w': 'grad_w', 'grad_kv_w': 'grad_w', 'grad_b_in_w': 'grad_w', 'grad_b_out_w': 'grad_w', 'delta_ada_w': 'delta_w', 'delta_ada_b': 'delta_w', 'delta_ln_g': 'delta_w', 'delta_ln_b': 'delta_w', 'delta_a_in_w': 'delta_w', 'delta_a_conv_w': 'delta_w', 'delta_a_conv_b': 'delta_w', 'delta_a_dt_bias': 'delta_w', 'delta_a_A_log': 'delta_w', 'delta_a_D': 'delta_w', 'delta_a_norm_g': 'delta_w', 'delta_a_out_w': 'delta_w', 'delta_kv_w': 'delta_w', 'delta_b_in_w': 'delta_w', 'delta_b_out_w': 'delta_w', 'new_m_ada_w': 'new_m', 'new_m_ada_b': 'new_m', 'new_m_ln_g': 'new_m', 'new_m_ln_b': 'new_m', 'new_m_a_in_w': 'new_m', 'new_m_a_conv_w': 'new_m', 'new_m_a_conv_b': 'new_m', 'new_m_a_dt_bias': 'new_m', 'new_m_a_A_log': 'new_m', 'new_m_a_D': 'new_m', 'new_m_a_norm_g': 'new_m', 'new_m_a_out_w': 'new_m', 'new_m_kv_w': 'new_m', 'new_m_b_in_w': 'new_m', 'new_m_b_out_w': 'new_m', 'new_v_ada_w': 'new_v', 'new_v_ada_b': 'new_v', 'new_v_ln_g': 'new_v', 'new_v_ln_b': 'new_v', 'new_v_a_in_w': 'new_v', 'new_v_a_conv_w': 'new_v', 'new_v_a_conv_b': 'new_v', 'new_v_a_dt_bias': 'new_v', 'new_v_a_A_log': 'new_v', 'new_v_a_D': 'new_v', 'new_v_a_norm_g': 'new_v', 'new_v_a_out_w': 'new_v', 'new_v_kv_w': 'new_v', 'new_v_b_in_w': 'new_v', 'new_v_b_out_w': 'new_v'}


def _forward(args):
    return _fwd_reference(*[args[k] for k in FWD_PARAMS])


def _output_shape():
    def fwd():
        inp = _fwd_setup_inputs(0)
        return _fwd_reference(*[inp[k] for k in FWD_PARAMS])
    out = _jax.eval_shape(fwd)
    return out.shape, out.dtype

N_MICROBATCH = 1
ADAM_LR = 0.001
ADAM_B1 = 0.9
ADAM_B2 = 0.999
ADAM_EPS = 1e-08
ADAM_WD = 0.01
ADAM_STEP = 10
PER_EXAMPLE_BATCH_AXIS = {'x': 0, 'c': 0, 'loss_target': 0}
SHARED_INPUTS = []
_WEIGHT_DTYPES = {'ada_w': _jnp.float32, 'ada_b': _jnp.float32, 'ln_g': _jnp.float32, 'ln_b': _jnp.float32, 'a_in_w': _jnp.float32, 'a_conv_w': _jnp.float32, 'a_conv_b': _jnp.float32, 'a_dt_bias': _jnp.float32, 'a_A_log': _jnp.float32, 'a_D': _jnp.float32, 'a_norm_g': _jnp.float32, 'a_out_w': _jnp.float32, 'kv_w': _jnp.float32, 'b_in_w': _jnp.float32, 'b_out_w': _jnp.float32}
MOMENT_SCALE = {'ada_w': 1.926844e-02, 'ada_b': 3.460211e-02, 'ln_g': 1.130489e+01, 'ln_b': 7.183403e-01, 'a_in_w': 1.971026e-02, 'a_conv_w': 1.830726e-02, 'a_conv_b': 2.874263e-02, 'a_dt_bias': 3.846289e-02, 'a_A_log': 7.402253e-02, 'a_D': 1.143452e-01, 'a_norm_g': 2.248121e-02, 'a_out_w': 6.120548e-02, 'kv_w': 4.763243e-03, 'b_in_w': 5.956312e-03, 'b_out_w': 1.275172e-02}


def _to_microbatches(a, axis):
    t = _jnp.moveaxis(a, axis, 0)
    t = t.reshape((N_MICROBATCH, t.shape[0] // N_MICROBATCH) + t.shape[1:])
    return _jnp.moveaxis(t, 1, axis + 1)


def setup_inputs(seed: int = 0) -> dict:
    inp = _fwd_setup_inputs(seed)
    key = _jax.random.fold_in(_jax.random.key(seed), 7919)
    shape, _ = _output_shape()
    out = dict(inp)
    out["loss_target"] = _jax.random.normal(_jax.random.fold_in(key, 0), shape, _jnp.float32)
    for i, name in enumerate(TWIN_WEIGHTS):
        w = inp[name].astype(_jnp.float32)
        if MOMENT_SCALE is None:
            s = _jnp.sqrt(_jnp.mean(_jnp.square(w)) + 1e-30)
        else:
            s = MOMENT_SCALE[name]
        km, kv = _jax.random.split(_jax.random.fold_in(key, i + 1))
        out[name] = w
        out["m_" + name] = s * _jax.random.normal(km, w.shape, _jnp.float32)
        out["v_" + name] = (s * s) * _jax.random.uniform(kv, w.shape, _jnp.float32, 0.5, 1.5)
    if N_MICROBATCH > 1:
        for name, axis in PER_EXAMPLE_BATCH_AXIS.items():
            out[name] = _to_microbatches(out[name], axis)
    return {'x': out['x'], 'c': out['c'], 'ada_w': out['ada_w'], 'ada_b': out['ada_b'], 'ln_g': out['ln_g'], 'ln_b': out['ln_b'], 'a_in_w': out['a_in_w'], 'a_conv_w': out['a_conv_w'], 'a_conv_b': out['a_conv_b'], 'a_dt_bias': out['a_dt_bias'], 'a_A_log': out['a_A_log'], 'a_D': out['a_D'], 'a_norm_g': out['a_norm_g'], 'a_out_w': out['a_out_w'], 'kv_w': out['kv_w'], 'b_in_w': out['b_in_w'], 'b_out_w': out['b_out_w'], 'loss_target': out['loss_target'], 'm_ada_w': out['m_ada_w'], 'm_ada_b': out['m_ada_b'], 'm_ln_g': out['m_ln_g'], 'm_ln_b': out['m_ln_b'], 'm_a_in_w': out['m_a_in_w'], 'm_a_conv_w': out['m_a_conv_w'], 'm_a_conv_b': out['m_a_conv_b'], 'm_a_dt_bias': out['m_a_dt_bias'], 'm_a_A_log': out['m_a_A_log'], 'm_a_D': out['m_a_D'], 'm_a_norm_g': out['m_a_norm_g'], 'm_a_out_w': out['m_a_out_w'], 'm_kv_w': out['m_kv_w'], 'm_b_in_w': out['m_b_in_w'], 'm_b_out_w': out['m_b_out_w'], 'v_ada_w': out['v_ada_w'], 'v_ada_b': out['v_ada_b'], 'v_ln_g': out['v_ln_g'], 'v_ln_b': out['v_ln_b'], 'v_a_in_w': out['v_a_in_w'], 'v_a_conv_w': out['v_a_conv_w'], 'v_a_conv_b': out['v_a_conv_b'], 'v_a_dt_bias': out['v_a_dt_bias'], 'v_a_A_log': out['v_a_A_log'], 'v_a_D': out['v_a_D'], 'v_a_norm_g': out['v_a_norm_g'], 'v_a_out_w': out['v_a_out_w'], 'v_kv_w': out['v_kv_w'], 'v_b_in_w': out['v_b_in_w'], 'v_b_out_w': out['v_b_out_w']}


def _loss(weights, diff, rest, loss_target):
    with _jax.named_scope("forward"):
        args = {**rest, TWIN_DIFF_INPUT: diff, **{k: w.astype(_WEIGHT_DTYPES[k]) for k, w in weights.items()}}
        y = _forward(args)
    with _jax.named_scope("loss_head"):
        err = _jnp.square(y.astype(_jnp.float32) - loss_target)
        return 0.5 * _jnp.sum(_jnp.mean(err, axis=-1)) if err.ndim else 0.5 * err


def _adamw(w, g, m, v):
    m = ADAM_B1 * m + (1.0 - ADAM_B1) * g
    v = ADAM_B2 * v + (1.0 - ADAM_B2) * _jnp.square(g)
    m_hat = m / (1.0 - ADAM_B1 ** ADAM_STEP)
    v_hat = v / (1.0 - ADAM_B2 ** ADAM_STEP)
    delta = -ADAM_LR * (m_hat / (_jnp.sqrt(v_hat) + ADAM_EPS) + ADAM_WD * w)
    return delta, m, v


def reference(x, c, ada_w, ada_b, ln_g, ln_b, a_in_w, a_conv_w, a_conv_b, a_dt_bias, a_A_log, a_D, a_norm_g, a_out_w, kv_w, b_in_w, b_out_w, loss_target, m_ada_w, m_ada_b, m_ln_g, m_ln_b, m_a_in_w, m_a_conv_w, m_a_conv_b, m_a_dt_bias, m_a_A_log, m_a_D, m_a_norm_g, m_a_out_w, m_kv_w, m_b_in_w, m_b_out_w, v_ada_w, v_ada_b, v_ln_g, v_ln_b, v_a_in_w, v_a_conv_w, v_a_conv_b, v_a_dt_bias, v_a_A_log, v_a_D, v_a_norm_g, v_a_out_w, v_kv_w, v_b_in_w, v_b_out_w):
    given = dict(x=x, c=c, ada_w=ada_w, ada_b=ada_b, ln_g=ln_g, ln_b=ln_b, a_in_w=a_in_w, a_conv_w=a_conv_w, a_conv_b=a_conv_b, a_dt_bias=a_dt_bias, a_A_log=a_A_log, a_D=a_D, a_norm_g=a_norm_g, a_out_w=a_out_w, kv_w=kv_w, b_in_w=b_in_w, b_out_w=b_out_w, loss_target=loss_target, m_ada_w=m_ada_w, m_ada_b=m_ada_b, m_ln_g=m_ln_g, m_ln_b=m_ln_b, m_a_in_w=m_a_in_w, m_a_conv_w=m_a_conv_w, m_a_conv_b=m_a_conv_b, m_a_dt_bias=m_a_dt_bias, m_a_A_log=m_a_A_log, m_a_D=m_a_D, m_a_norm_g=m_a_norm_g, m_a_out_w=m_a_out_w, m_kv_w=m_kv_w, m_b_in_w=m_b_in_w, m_b_out_w=m_b_out_w, v_ada_w=v_ada_w, v_ada_b=v_ada_b, v_ln_g=v_ln_g, v_ln_b=v_ln_b, v_a_in_w=v_a_in_w, v_a_conv_w=v_a_conv_w, v_a_conv_b=v_a_conv_b, v_a_dt_bias=v_a_dt_bias, v_a_A_log=v_a_A_log, v_a_D=v_a_D, v_a_norm_g=v_a_norm_g, v_a_out_w=v_a_out_w, v_kv_w=v_kv_w, v_b_in_w=v_b_in_w, v_b_out_w=v_b_out_w)
    weights = {n: given[n] for n in TWIN_WEIGHTS}
    shared = {n: given[n] for n in SHARED_INPUTS}
    per_example = {n: given[n] for n in ['x', 'c']}
    grad_fn = _jax.value_and_grad(_loss, argnums=(0, 1))

    def one_microbatch(ex, loss_target):
        ex = dict(ex)
        diff = ex.pop(TWIN_DIFF_INPUT)
        return grad_fn(weights, diff, {**shared, **ex}, loss_target)

    if N_MICROBATCH == 1:
        loss, (grad_w, grad_x) = one_microbatch(per_example, given["loss_target"])
    else:
        def body(carry, xs):
            loss_sum, grad_sum = carry
            l_k, (gw_k, gx_k) = one_microbatch(xs[0], xs[1])
            with _jax.named_scope("update"):
                return (loss_sum + l_k, _jax.tree.map(_jnp.add, grad_sum, gw_k)), gx_k

        init = (_jnp.zeros((), _jnp.float32), _jax.tree.map(_jnp.zeros_like, weights))
        (loss, grad_w), grad_x = _jax.lax.scan(body, init, (per_example, given["loss_target"]))
    with _jax.named_scope("update"):
        delta_w, new_m, new_v = {}, {}, {}
        for n in TWIN_WEIGHTS:
            delta_w[n], new_m[n], new_v[n] = _adamw(weights[n], grad_w[n], given["m_" + n], given["v_" + n])
    return (loss, grad_x, *[grad_w[n] for n in TWIN_WEIGHTS], *[delta_w[n] for n in TWIN_WEIGHTS],
            *[new_m[n] for n in TWIN_WEIGHTS], *[new_v[n] for n in TWIN_WEIGHTS])
```

```python
import functools
import math

import numpy as np
import jax
import jax.numpy as jnp
from jax import lax
from jax.experimental import pallas as pl
from jax.experimental.pallas import tpu as pltpu

F32, BF16 = jnp.float32, jnp.bfloat16
HI = lax.Precision.HIGHEST
MESH = pl.DeviceIdType.MESH
N_DEV = 8

SSD_HEAD_DIM = 64
SSD_N_GROUPS = 8
SSD_D_STATE = 128
SSD_CONV_W = 4
SSD_CHUNK = 256
DIL_PATTERNS = ((128, 1), (512, 4), (2048, 16))
DIL_N_GROUPS = 3
DIL_HEADS = 8
DIL_HEAD_DIM = 128
DIL_BLOCK = 128
DIL_W = DIL_HEADS * DIL_HEAD_DIM
DEPTH = 2
DEEPNORM_ALPHA = (2 * DEPTH) ** 0.25
LN_EPS = 1e-5
RMS_EPS = 1e-5
ADAM_LR, ADAM_B1, ADAM_B2, ADAM_EPS, ADAM_WD, ADAM_STEP = 0.001, 0.9, 0.999, 1e-08, 0.01, 10
LANE = 128
NEG = -1e30
VMEM_LIMIT = 56 * 1024 * 1024


def _cp(sem=None):
    return pltpu.CompilerParams(dimension_semantics=sem, vmem_limit_bytes=VMEM_LIMIT)


def _silu(x):
    return x * jax.nn.sigmoid(x)


def _dsilu(x):
    s = jax.nn.sigmoid(x)
    return s * (1.0 + x * (1.0 - s))


def _softplus(x):
    return jnp.maximum(x, 0.0) + jnp.log(1.0 + jnp.exp(-jnp.abs(x)))


def _nt(a, b):
    return lax.dot_general(a, b, (((1,), (1,)), ((), ())), preferred_element_type=F32)


def _nn(a, b):
    return jnp.dot(a, b, preferred_element_type=F32)


def _hi(a, b):
    return jnp.dot(a, b, preferred_element_type=F32, precision=HI)


def _pick(n, pref, align=LANE):
    if n <= pref:
        return n
    for t in range(pref - pref % align, 0, -align):
        if n % t == 0:
            return t
    return n


def _peers():
    x, y, c = lax.axis_index("x"), lax.axis_index("y"), lax.axis_index("c")
    out = []
    for mask in range(1, N_DEV):
        bx, by, bc = (mask >> 2) & 1, (mask >> 1) & 1, mask & 1
        px = 1 - x if bx else x
        py = 1 - y if by else y
        pc = 1 - c if bc else c
        out.append((px, py, pc))
    return (x, y, c), out


def _all_gather(v, name):
    shp, dt = v.shape, v.dtype

    def body(x_ref, out_ref, send_sems, recv_sems, local_sem):
        x, y, c = lax.axis_index("x"), lax.axis_index("y"), lax.axis_index("c")
        me, sibling = (x, y, c), (x, y, 1 - c)
        chips = [(1 - x, y), (x, 1 - y), (1 - x, 1 - y)]

        def slab(px, py, pc):
            return out_ref.at[4 * px + 2 * py + pc]

        def copy(k, block, to, src=None):
            return pltpu.make_async_remote_copy(
                src_ref=slab(*block) if src is None else src, dst_ref=slab(*block),
                send_sem=send_sems.at[k], recv_sem=recv_sems.at[k], device_id=to, device_id_type=MESH)

        mine = pltpu.make_async_copy(x_ref, slab(*me), local_sem)
        mine.start()
        first = [copy(0, me, sibling, src=x_ref)]
        first += [copy(1 + j, me, (*chip, c), src=x_ref) for j, chip in enumerate(chips)]
        for cp in first:
            cp.start()
        passed = [copy(4 + j, (*chip, c), sibling) for j, chip in enumerate(chips)]
        for j, chip in enumerate(chips):
            copy(1 + j, (*chip, c), me).wait_recv()
            passed[j].start()
        copy(0, sibling, me).wait_recv()
        for j, chip in enumerate(chips):
            copy(4 + j, (*chip, 1 - c), me).wait_recv()
        for cp in first + passed:
            cp.wait_send()
        mine.wait()

    return pl.pallas_call(
        body, name=name,
        out_shape=jax.ShapeDtypeStruct((N_DEV,) + shp, dt),
        in_specs=[pl.BlockSpec(memory_space=pl.ANY)],
        out_specs=pl.BlockSpec(memory_space=pl.ANY),
        scratch_shapes=[pltpu.SemaphoreType.DMA((7,)), pltpu.SemaphoreType.DMA((7,)), pltpu.SemaphoreType.DMA],
    )(v)


def _all_to_all(v, name):
    def body(x_ref, out_ref, send_sems, recv_sems, local_sem):
        (x, y, c), peers = _peers()
        me = 4 * x + 2 * y + c
        mine = pltpu.make_async_copy(x_ref.at[me], out_ref.at[me], local_sem)
        mine.start()
        cps = []
        for k, (px, py, pc) in enumerate(peers):
            cps.append(pltpu.make_async_remote_copy(
                src_ref=x_ref.at[4 * px + 2 * py + pc], dst_ref=out_ref.at[me],
                send_sem=send_sems.at[k], recv_sem=recv_sems.at[k], device_id=(px, py, pc), device_id_type=MESH))
        for cp in cps:
            cp.start()
        for k, (px, py, pc) in enumerate(peers):
            pltpu.make_async_remote_copy(
                src_ref=x_ref.at[me], dst_ref=out_ref.at[4 * px + 2 * py + pc],
                send_sem=send_sems.at[k], recv_sem=recv_sems.at[k], device_id=(px, py, pc),
                device_id_type=MESH).wait_recv()
        for cp in cps:
            cp.wait_send()
        mine.wait()

    return pl.pallas_call(
        body, name=name,
        out_shape=jax.ShapeDtypeStruct(v.shape, v.dtype),
        in_specs=[pl.BlockSpec(memory_space=pl.ANY)],
        out_specs=pl.BlockSpec(memory_space=pl.ANY),
        scratch_shapes=[pltpu.SemaphoreType.DMA((7,)), pltpu.SemaphoreType.DMA((7,)), pltpu.SemaphoreType.DMA],
    )(v)


def _matmul(a, b, *, name, ta=False, tb=False, out_dtype=F32, tm=1024, tn=1024, tk=2048,
            exact=False, a_silu=False, bias=None):
    (K, M) = a.shape if ta else a.shape[::-1]
    (N, K2) = b.shape if tb else b.shape[::-1]
    assert K == K2, (a.shape, b.shape, ta, tb)
    tm, tn, tk = _pick(M, tm), _pick(N, tn), _pick(K, tk)
    nk = K // tk
    a_spec = pl.BlockSpec((tk, tm), lambda i, j, k: (k, i)) if ta else pl.BlockSpec((tm, tk), lambda i, j, k: (i, k))
    b_spec = pl.BlockSpec((tn, tk), lambda i, j, k: (j, k)) if tb else pl.BlockSpec((tk, tn), lambda i, j, k: (k, j))
    dims = (((0,) if ta else (1,), (1,) if tb else (0,)), ((), ()))
    in_specs, args = [a_spec, b_spec], [a, b]
    if bias is not None:
        in_specs.append(pl.BlockSpec((1, tn), lambda i, j, k: (0, j)))
        args.append(bias)

    def body(*refs):
        a_ref, b_ref = refs[0], refs[1]
        bias_ref = refs[2] if bias is not None else None
        o_ref = refs[2 + (bias is not None)]
        av, bv = a_ref[...], b_ref[...]
        if a_silu:
            av = _silu(av.astype(F32))
        if exact:
            p = lax.dot_general(av.astype(F32), bv.astype(F32), dims, preferred_element_type=F32, precision=HI)
        else:
            p = lax.dot_general(av.astype(BF16), bv.astype(BF16), dims, preferred_element_type=F32)

        def fin(r):
            if bias_ref is not None:
                r = r + bias_ref[...]
            o_ref[...] = r.astype(o_ref.dtype)

        if nk == 1:
            fin(p)
        else:
            acc = refs[-1]
            k = pl.program_id(2)

            @pl.when(k == 0)
            def _():
                acc[...] = p

            @pl.when(k > 0)
            def _():
                acc[...] += p

            @pl.when(k == nk - 1)
            def _():
                fin(acc[...])

    return pl.pallas_call(
        body, name=name,
        out_shape=jax.ShapeDtypeStruct((M, N), out_dtype),
        grid=(M // tm, N // tn, nk),
        in_specs=in_specs,
        out_specs=pl.BlockSpec((tm, tn), lambda i, j, k: (i, j)),
        scratch_shapes=[pltpu.VMEM((tm, tn), F32)] if nk > 1 else [],
        compiler_params=_cp(("parallel", "parallel", "arbitrary")),
    )(*args)


def _rowmap(fn, rows, bcasts, outs, accs, *, name, tr=256, cw=None):
    L = rows[0][0].shape[0]
    tr = _pick(L, tr)
    nr, nb, no, na = len(rows), len(bcasts), len(outs), len(accs)
    if cw is None:
        ncol = 1
        widths = [w for (_, _, w) in rows]
    else:
        wtot = rows[0][2]
        ncol = wtot // cw
        widths = [cw] * nr
    in_specs, args = [], []
    for (arr, off, w), bw in zip(rows, widths):
        assert off % bw == 0
        in_specs.append(pl.BlockSpec((tr, bw), functools.partial(lambda j, i, o: (i, o + j), o=off // bw)))
        args.append(arr)
    for arr in bcasts:
        bw = arr.shape[1] if cw is None else cw
        in_specs.append(pl.BlockSpec((arr.shape[0], bw), lambda j, i: (0, j)))
        args.append(arr)
    out_shape, out_specs = [], []
    for (w, dt) in outs:
        bw = w if cw is None else cw
        out_shape.append(jax.ShapeDtypeStruct((L, w), dt))
        out_specs.append(pl.BlockSpec((tr, bw), lambda j, i: (i, j)))
    for (r, w) in accs:
        bw = w if cw is None else cw
        out_shape.append(jax.ShapeDtypeStruct((r, w), F32))
        out_specs.append(pl.BlockSpec((r, bw), lambda j, i: (0, j)))

    def body(*refs):
        ins = [r[...] for r in refs[:nr + nb]]
        o_refs = refs[nr + nb:nr + nb + no]
        a_refs = refs[nr + nb + no:]
        o, a = fn(*ins)
        for ref, val in zip(o_refs, o):
            ref[...] = val.astype(ref.dtype)
        if na:
            @pl.when(pl.program_id(1) == 0)
            def _():
                for ref in a_refs:
                    ref[...] = jnp.zeros_like(ref)

            for ref, val in zip(a_refs, a):
                ref[...] += val

    res = pl.pallas_call(
        body, name=name, out_shape=out_shape, grid=(ncol, L // tr),
        in_specs=in_specs, out_specs=out_specs,
        compiler_params=_cp(("parallel", "arbitrary")),
    )(*args)
    return res


def _csum(v):
    return jnp.sum(v, axis=0, keepdims=True)


def _shift_rows(v, s, rows):
    if s == 0:
        return v
    n = v.shape[0]
    r = pltpu.roll(v, s % n, 0)
    if s > 0:
        return jnp.where(rows >= s, r, 0.0)
    return jnp.where(rows < n + s, r, 0.0)


def _conv_fwd(proj, off, width, w, b, *, name, tc=256):
    L = proj.shape[0]
    tc = _pick(width, tc)

    def body(x_ref, w_ref, b_ref, o_ref):
        x = x_ref[...]
        rows = lax.broadcasted_iota(jnp.int32, x.shape, 0)
        acc = jnp.zeros_like(x) + b_ref[...]
        for k in range(SSD_CONV_W):
            acc = acc + w_ref[k:k + 1, :] * _shift_rows(x, SSD_CONV_W - 1 - k, rows)
        o_ref[...] = _silu(acc)

    return pl.pallas_call(
        body, name=name, out_shape=jax.ShapeDtypeStruct((L, width), F32), grid=(width // tc,),
        in_specs=[pl.BlockSpec((L, tc), functools.partial(lambda j, o: (0, o + j), o=off // tc)),
                  pl.BlockSpec((SSD_CONV_W, tc), lambda j: (0, j)), pl.BlockSpec((1, tc), lambda j: (0, j))],
        out_specs=pl.BlockSpec((L, tc), lambda j: (0, j)),
        compiler_params=_cp(("parallel",)),
    )(proj, w, b)


def _conv_bwd(proj, off, width, w, b, dy, *, name, tc=256):
    L = proj.shape[0]
    tc = _pick(width, tc)

    def body(x_ref, w_ref, b_ref, dy_ref, dx_ref, dw_ref, db_ref):
        x = x_ref[...]
        rows = lax.broadcasted_iota(jnp.int32, x.shape, 0)
        xs = [_shift_rows(x, SSD_CONV_W - 1 - k, rows) for k in range(SSD_CONV_W)]
        pre = jnp.zeros_like(x) + b_ref[...]
        for k in range(SSD_CONV_W):
            pre = pre + w_ref[k:k + 1, :] * xs[k]
        dpre = dy_ref[...] * _dsilu(pre)
        dx = jnp.zeros_like(x)
        for k in range(SSD_CONV_W):
            dx = dx + w_ref[k:k + 1, :] * _shift_rows(dpre, -(SSD_CONV_W - 1 - k), rows)
            dw_ref[k:k + 1, :] = _csum(dpre * xs[k])
        dx_ref[...] = dx.astype(dx_ref.dtype)
        db_ref[...] = _csum(dpre)

    return pl.pallas_call(
        body, name=name,
        out_shape=[jax.ShapeDtypeStruct((L, width), BF16), jax.ShapeDtypeStruct((SSD_CONV_W, width), F32),
                   jax.ShapeDtypeStruct((1, width), F32)],
        grid=(width // tc,),
        in_specs=[pl.BlockSpec((L, tc), functools.partial(lambda j, o: (0, o + j), o=off // tc)),
                  pl.BlockSpec((SSD_CONV_W, tc), lambda j: (0, j)), pl.BlockSpec((1, tc), lambda j: (0, j)),
                  pl.BlockSpec((L, tc), lambda j: (0, j))],
        out_specs=[pl.BlockSpec((L, tc), lambda j: (0, j)), pl.BlockSpec((SSD_CONV_W, tc), lambda j: (0, j)),
                   pl.BlockSpec((1, tc), lambda j: (0, j))],
        compiler_params=_cp(("parallel",)),
    )(proj, w, b, dy)


def _ssd_common(dtc_ref, dtr_ref, bc_ref, br_ref, alc_ref, alr_ref, Q, K, KP):
    P = KP // K
    ri = lax.broadcasted_iota(jnp.int32, (Q, Q), 0)
    ci = lax.broadcasted_iota(jnp.int32, (Q, Q), 1)
    lower = (ri >= ci)
    e_r = lax.broadcasted_iota(jnp.int32, (K, KP), 0)
    e_c = lax.broadcasted_iota(jnp.int32, (K, KP), 1)
    E = ((e_c >= e_r * P) & (e_c < (e_r + 1) * P)).astype(F32)
    t_r = lax.broadcasted_iota(jnp.int32, (KP, K), 0)
    t_c = lax.broadcasted_iota(jnp.int32, (KP, K), 1)
    Et = ((t_r >= t_c * P) & (t_r < (t_c + 1) * P)).astype(F32)
    raw_c = dtc_ref[...] + bc_ref[...]
    dt_c = _softplus(raw_c)
    A_c = -jnp.exp(alc_ref[...])
    a_c = _hi(lower.astype(F32), dt_c * A_c)
    dt_r = _softplus(dtr_ref[...] + br_ref[...])
    A_r = -jnp.exp(alr_ref[...])
    a_r = _hi(dt_r * A_r, (ri <= ci).astype(F32))
    a_f = _hi(a_c, E)
    dt_f = _hi(dt_c, E)
    return dict(lower=lower, upper=(ri <= ci), E=E, Et=Et, raw_c=raw_c, dt_c=dt_c, A_c=A_c, a_c=a_c, a_r=a_r,
                a_f=a_f, dt_f=dt_f)


def _half_masks():
    li = lax.broadcasted_iota(jnp.int32, (1, LANE), 1)
    return [(li < SSD_HEAD_DIM).astype(F32), (li >= SSD_HEAD_DIM).astype(F32)]


def _ssd_specs(L, K, KP, G, xs_off, b_off, c_off, rev, nc):
    Q, N = SSD_CHUNK, SSD_D_STATE
    cidx = (lambda c: nc - 1 - c) if rev else (lambda c: c)
    return [
        pl.BlockSpec((Q, KP), lambda g, c: (cidx(c), xs_off // KP + g)),
        pl.BlockSpec((Q, N), lambda g, c: (cidx(c), b_off // N + g)),
        pl.BlockSpec((Q, N), lambda g, c: (cidx(c), c_off // N + g)),
        pl.BlockSpec((None, Q, K), lambda g, c: (g, cidx(c), 0)),
        pl.BlockSpec((None, K, Q), lambda g, c: (g, 0, cidx(c))),
        pl.BlockSpec((None, 1, K), lambda g, c: (g, 0, 0)),
        pl.BlockSpec((None, K, 1), lambda g, c: (g, 0, 0)),
        pl.BlockSpec((None, 1, K), lambda g, c: (g, 0, 0)),
        pl.BlockSpec((None, K, 1), lambda g, c: (g, 0, 0)),
        pl.BlockSpec((1, KP), lambda g, c: (0, g)),
    ]


def _ssd_fwd(xbc, dtc, dtr, bias_c, bias_r, alog_c, alog_r, d_full, *, d_inner, name):
    L = xbc.shape[0]
    G, N, Q, P = SSD_N_GROUPS, SSD_D_STATE, SSD_CHUNK, SSD_HEAD_DIM
    KP = d_inner // G
    K = KP // P
    nc = L // Q
    npair = KP // LANE

    def body(xs_ref, b_ref, c_ref, dtc_ref, dtr_ref, bc_ref, br_ref, alc_ref, alr_ref, df_ref, y_ref, st_ref, S):
        @pl.when(pl.program_id(1) == 0)
        def _():
            S[...] = jnp.zeros_like(S)

        cm = _ssd_common(dtc_ref, dtr_ref, bc_ref, br_ref, alc_ref, alr_ref, Q, K, KP)
        st_ref[...] = S[...]
        xs = xs_ref[...]
        Bm, Cm = b_ref[...], c_ref[...]
        Bb, Cb = Bm.astype(BF16), Cm.astype(BF16)
        a_f = cm["a_f"]
        X = xs * cm["dt_f"]
        ea = jnp.exp(a_f)
        alast = a_f[Q - 1:Q, :]
        tail = jnp.exp(alast - a_f)
        cb = _nt(Cb, Bb)
        Sv = S[...]
        yoff = _nn(Cb, Sv.astype(BF16)) * ea
        skip = xs * df_ref[...]
        masks = _half_masks()
        for pr in range(npair):
            Xp = X[:, pr * LANE:(pr + 1) * LANE]
            acc = yoff[:, pr * LANE:(pr + 1) * LANE] + skip[:, pr * LANE:(pr + 1) * LANE]
            for hh in range(2):
                k = 2 * pr + hh
                seg = cm["a_c"][:, k:k + 1] - cm["a_r"][k:k + 1, :]
                dec = jnp.where(cm["lower"], jnp.exp(jnp.minimum(seg, 0.0)), 0.0)
                acc = acc + _nn((cb * dec).astype(BF16), (Xp * masks[hh]).astype(BF16))
            y_ref[:, pr * LANE:(pr + 1) * LANE] = acc
        Bt = Bm.T
        S[...] = Sv * jnp.exp(alast) + _nn(Bt.astype(BF16), (X * tail).astype(BF16))

    return pl.pallas_call(
        body, name=name,
        out_shape=[jax.ShapeDtypeStruct((L, d_inner), F32), jax.ShapeDtypeStruct((G, nc, N, KP), F32)],
        grid=(G, nc),
        in_specs=_ssd_specs(L, K, KP, G, 0, d_inner, d_inner + G * N, False, nc),
        out_specs=[pl.BlockSpec((Q, KP), lambda g, c: (c, g)), pl.BlockSpec((None, None, N, KP), lambda g, c: (g, c, 0, 0))],
        scratch_shapes=[pltpu.VMEM((N, KP), F32)],
        compiler_params=_cp(("parallel", "arbitrary")),
    )(xbc, xbc, xbc, dtc, dtr, bias_c, bias_r, alog_c, alog_r, d_full)


def _ssd_bwd(xbc, dtc, dtr, bias_c, bias_r, alog_c, alog_r, d_full, states, dy, *, d_inner, name):
    L = xbc.shape[0]
    G, N, Q, P = SSD_N_GROUPS, SSD_D_STATE, SSD_CHUNK, SSD_HEAD_DIM
    KP = d_inner // G
    K = KP // P
    nc = L // Q
    npair = KP // LANE

    def body(xs_ref, b_ref, c_ref, dtc_ref, dtr_ref, bc_ref, br_ref, alc_ref, alr_ref, df_ref, st_ref, dy_ref,
             dxs_ref, db_ref, dc_ref, ddt_ref, da_ref, dd_ref, dS):
        @pl.when(pl.program_id(1) == 0)
        def _():
            dS[...] = jnp.zeros_like(dS)

        cm = _ssd_common(dtc_ref, dtr_ref, bc_ref, br_ref, alc_ref, alr_ref, Q, K, KP)
        Et, a_c, a_r, a_f, dt_f, dt_c = cm["Et"], cm["a_c"], cm["a_r"], cm["a_f"], cm["dt_f"], cm["dt_c"]
        xs = xs_ref[...]
        Bm, Cm = b_ref[...], c_ref[...]
        Bb, Cb = Bm.astype(BF16), Cm.astype(BF16)
        dY = dy_ref[...]
        X = xs * dt_f
        ea = jnp.exp(a_f)
        alast = a_f[Q - 1:Q, :]
        tail = jnp.exp(alast - a_f)
        el = jnp.exp(alast)
        Sv, dSn = st_ref[...], dS[...]
        Sb, dSb = Sv.astype(BF16), dSn.astype(BF16)
        cb = _nt(Cb, Bb)
        cbT = _nt(Bb, Cb)
        yoff_raw = _nn(Cb, Sb)
        dYe = dY * ea
        dC = _nt(dYe.astype(BF16), Sb)
        dS[...] = dSn * el + _nn(Cm.T.astype(BF16), dYe.astype(BF16))
        Gx = _nn(Bb, dSb)
        dB = _nt((X * tail).astype(BF16), dSb)
        dtl = Gx * X * tail
        da_f = dYe * yoff_raw - dtl
        dalast_f = _csum(dtl) + _csum(dSn * Sv) * el
        da_c = _hi(da_f, Et)
        onek = lax.broadcasted_iota(jnp.int32, (1, K), 1)
        masks = _half_masks()
        dcb = jnp.zeros((Q, Q), F32)
        dcbT = jnp.zeros((Q, Q), F32)
        dX_parts = []
        for pr in range(npair):
            Xp = X[:, pr * LANE:(pr + 1) * LANE]
            dYp = dY[:, pr * LANE:(pr + 1) * LANE]
            dXp = Gx[:, pr * LANE:(pr + 1) * LANE] * tail[:, pr * LANE:(pr + 1) * LANE]
            for hh in range(2):
                k = 2 * pr + hh
                Xk = (Xp * masks[hh]).astype(BF16)
                dYk = (dYp * masks[hh]).astype(BF16)
                seg = a_c[:, k:k + 1] - a_r[k:k + 1, :]
                dec = jnp.where(cm["lower"], jnp.exp(jnp.minimum(seg, 0.0)), 0.0)
                decT = jnp.where(cm["upper"], jnp.exp(jnp.minimum(-seg, 0.0)), 0.0)
                dM = _nt(dYk, Xk)
                dMT = _nt(Xk, dYk)
                MT = cbT * decT
                dcb = dcb + dM * dec
                dcbT = dcbT + dMT * decT
                da_k = jnp.sum(dM * cb * dec, axis=1, keepdims=True) - jnp.sum(dMT * MT, axis=1, keepdims=True)
                da_c = da_c + da_k * (onek == k).astype(F32)
                dXp = dXp + _nn(MT.astype(BF16), dYk)
            dX_parts.append(dXp)
        dX = jnp.concatenate(dX_parts, axis=1) if npair > 1 else dX_parts[0]
        dC = dC + _nn(dcb.astype(BF16), Bb)
        dB = dB + _nn(dcbT.astype(BF16), Cb)
        lastrow = (lax.broadcasted_iota(jnp.int32, (Q, 1), 0) == Q - 1).astype(F32)
        da_c = da_c + lastrow * _hi(dalast_f, Et)
        ddtA = _hi(cm["upper"].astype(F32), da_c)
        ddt_c = ddtA * cm["A_c"] + _hi(dX * xs, Et)
        ddt_ref[...] = ddt_c * jax.nn.sigmoid(cm["raw_c"])
        da_ref[...] = ddtA * dt_c
        dd_ref[...] = _hi(dY * xs, Et)
        dxs_ref[...] = dX * dt_f + dY * df_ref[...]
        db_ref[...] = dB
        dc_ref[...] = dC

    rc = lambda c: nc - 1 - c
    tok = jax.ShapeDtypeStruct((G, L, K), F32)
    tok_spec = pl.BlockSpec((None, Q, K), lambda g, c: (g, rc(c), 0))
    return pl.pallas_call(
        body, name=name,
        out_shape=[jax.ShapeDtypeStruct((L, d_inner), F32), jax.ShapeDtypeStruct((L, G * N), F32),
                   jax.ShapeDtypeStruct((L, G * N), F32), tok, tok, tok],
        grid=(G, nc),
        in_specs=_ssd_specs(L, K, KP, G, 0, d_inner, d_inner + G * N, True, nc) + [
            pl.BlockSpec((None, None, N, KP), lambda g, c: (g, rc(c), 0, 0)),
            pl.BlockSpec((Q, KP), lambda g, c: (rc(c), g))],
        out_specs=[pl.BlockSpec((Q, KP), lambda g, c: (rc(c), g)), pl.BlockSpec((Q, N), lambda g, c: (rc(c), g)),
                   pl.BlockSpec((Q, N), lambda g, c: (rc(c), g)), tok_spec, tok_spec, tok_spec],
        scratch_shapes=[pltpu.VMEM((N, KP), F32)],
        compiler_params=_cp(("parallel", "arbitrary")),
    )(xbc, xbc, xbc, dtc, dtr, bias_c, bias_r, alog_c, alog_r, d_full, states, dy)


def _slopes():
    n = DIL_N_GROUPS * DIL_HEADS
    s = 2.0 ** (-8.0 * np.arange(1, n + 1) / n)
    return s.reshape(DIL_N_GROUPS, DIL_HEADS).astype(np.float32)


def _attn_scores(qh, kh, slope_d, cur, valid_blk, transposed):
    B = DIL_BLOCK
    scale = DIL_HEAD_DIM ** -0.5
    if transposed:
        s = _nt(kh, qh) * scale
        kj = lax.broadcasted_iota(jnp.int32, (B, B), 0)
        qi = lax.broadcasted_iota(jnp.int32, (B, B), 1)
    else:
        s = _nt(qh, kh) * scale
        qi = lax.broadcasted_iota(jnp.int32, (B, B), 0)
        kj = lax.broadcasted_iota(jnp.int32, (B, B), 1)
    if cur:
        delta = qi - kj
        ok = kj <= qi
    else:
        delta = qi + B - kj
        ok = (kj >= qi) & valid_blk
    return jnp.where(ok, s - slope_d * delta.astype(F32), NEG)


def _attn_fwd(q, kv, g, *, name):
    L = q.shape[0]
    window, d = DIL_PATTERNS[g]
    assert window // d == DIL_BLOCK and L % (d * DIL_BLOCK) == 0
    M, B, W = L // d, DIL_BLOCK, DIL_W
    nb = M // B
    slopes = _slopes()[g]
    qv = q.reshape(M, d * DIL_N_GROUPS * W)
    kvv = kv.reshape(M, d * 2 * DIL_N_GROUPS * W)
    prev = lambda m: jnp.maximum(m - 1, 0)

    def body(q_ref, kc_ref, kp_ref, vc_ref, vp_ref, o_ref, lse_ref):
        has_prev = pl.program_id(1) > 0
        lse = jnp.zeros((B, DIL_HEADS), F32)
        onek = lax.broadcasted_iota(jnp.int32, (1, DIL_HEADS), 1)
        for h in range(DIL_HEADS):
            sl = slice(h * DIL_HEAD_DIM, (h + 1) * DIL_HEAD_DIM)
            qh = q_ref[:, sl]
            sd = float(slopes[h]) * d
            sc = _attn_scores(qh, kc_ref[:, sl], sd, True, None, False)
            sp = _attn_scores(qh, kp_ref[:, sl], sd, False, has_prev, False)
            m = jnp.maximum(jnp.max(sc, axis=1, keepdims=True), jnp.max(sp, axis=1, keepdims=True))
            pc, pp = jnp.exp(sc - m), jnp.exp(sp - m)
            den = jnp.sum(pc, axis=1, keepdims=True) + jnp.sum(pp, axis=1, keepdims=True)
            o = _nn(pc.astype(BF16), vc_ref[:, sl]) + _nn(pp.astype(BF16), vp_ref[:, sl])
            o_ref[:, sl] = o / den
            lse = lse + (m + jnp.log(den)) * (onek == h).astype(F32)
        lse_ref[...] = lse

    o, lse = pl.pallas_call(
        body, name=name,
        out_shape=[jax.ShapeDtypeStruct((M, d * W), F32), jax.ShapeDtypeStruct((d, M, DIL_HEADS), F32)],
        grid=(d, nb),
        in_specs=[pl.BlockSpec((B, W), lambda r, m: (m, r * 3 + g)),
                  pl.BlockSpec((B, W), lambda r, m: (m, r * 6 + g)),
                  pl.BlockSpec((B, W), lambda r, m: (prev(m), r * 6 + g)),
                  pl.BlockSpec((B, W), lambda r, m: (m, r * 6 + 3 + g)),
                  pl.BlockSpec((B, W), lambda r, m: (prev(m), r * 6 + 3 + g))],
        out_specs=[pl.BlockSpec((B, W), lambda r, m: (m, r)),
                   pl.BlockSpec((None, B, DIL_HEADS), lambda r, m: (r, m, 0))],
        compiler_params=_cp(("parallel", "parallel")),
    )(qv, kvv, kvv, kvv, kvv)
    return o.reshape(L, W), lse


def _attn_bwd_q(q, kv, do, lse, dl, g, *, name):
    L = q.shape[0]
    _, d = DIL_PATTERNS[g]
    M, B, W = L // d, DIL_BLOCK, DIL_W
    nb = M // B
    slopes = _slopes()[g]
    scale = DIL_HEAD_DIM ** -0.5
    qv = q.reshape(M, d * DIL_N_GROUPS * W)
    kvv = kv.reshape(M, d * 2 * DIL_N_GROUPS * W)
    dov = do.reshape(M, d * W)
    prev = lambda m: jnp.maximum(m - 1, 0)

    def body(q_ref, kc_ref, kp_ref, vc_ref, vp_ref, do_ref, lse_ref, dl_ref, dq_ref):
        has_prev = pl.program_id(1) > 0
        lse_all, dl_all = lse_ref[...], dl_ref[...]
        for h in range(DIL_HEADS):
            sl = slice(h * DIL_HEAD_DIM, (h + 1) * DIL_HEAD_DIM)
            qh, doh = q_ref[:, sl], do_ref[:, sl]
            sd = float(slopes[h]) * d
            lse_h, dl_h = lse_all[:, h:h + 1], dl_all[:, h:h + 1]
            pc = jnp.exp(_attn_scores(qh, kc_ref[:, sl], sd, True, None, False) - lse_h)
            pp = jnp.exp(_attn_scores(qh, kp_ref[:, sl], sd, False, has_prev, False) - lse_h)
            dsc = pc * (_nt(doh, vc_ref[:, sl]) - dl_h)
            dsp = pp * (_nt(doh, vp_ref[:, sl]) - dl_h)
            dq = _nn(dsc.astype(BF16), kc_ref[:, sl]) + _nn(dsp.astype(BF16), kp_ref[:, sl])
            dq_ref[:, sl] = (dq * scale).astype(dq_ref.dtype)

    col = pl.BlockSpec((None, B, DIL_HEADS), lambda r, m: (r, m, 0))
    dq = pl.pallas_call(
        body, name=name,
        out_shape=jax.ShapeDtypeStruct((M, d * W), BF16),
        grid=(d, nb),
        in_specs=[pl.BlockSpec((B, W), lambda r, m: (m, r * 3 + g)),
                  pl.BlockSpec((B, W), lambda r, m: (m, r * 6 + g)),
                  pl.BlockSpec((B, W), lambda r, m: (prev(m), r * 6 + g)),
                  pl.BlockSpec((B, W), lambda r, m: (m, r * 6 + 3 + g)),
                  pl.BlockSpec((B, W), lambda r, m: (prev(m), r * 6 + 3 + g)),
                  pl.BlockSpec((B, W), lambda r, m: (m, r)), col, col],
        out_specs=pl.BlockSpec((B, W), lambda r, m: (m, r)),
        compiler_params=_cp(("parallel", "parallel")),
    )(qv, kvv, kvv, kvv, kvv, dov, lse, dl)
    return dq.reshape(L, W)


def _attn_bwd_kv(q, kv, do, lse_t, dl_t, g, *, name):
    L = q.shape[0]
    _, d = DIL_PATTERNS[g]
    M, B, W = L // d, DIL_BLOCK, DIL_W
    nb = M // B
    slopes = _slopes()[g]
    scale = DIL_HEAD_DIM ** -0.5
    qv = q.reshape(M, d * DIL_N_GROUPS * W)
    kvv = kv.reshape(M, d * 2 * DIL_N_GROUPS * W)
    dov = do.reshape(M, d * W)
    nxt = lambda m: jnp.minimum(m + 1, nb - 1)

    def body(k_ref, v_ref, qc_ref, qn_ref, doc_ref, don_ref, lsec_ref, lsen_ref, dlc_ref, dln_ref, dk_ref, dv_ref):
        has_next = pl.program_id(1) < nb - 1
        lsec, lsen, dlc, dln = lsec_ref[...], lsen_ref[...], dlc_ref[...], dln_ref[...]
        for h in range(DIL_HEADS):
            sl = slice(h * DIL_HEAD_DIM, (h + 1) * DIL_HEAD_DIM)
            kh, vh = k_ref[:, sl], v_ref[:, sl]
            sd = float(slopes[h]) * d
            ptc = jnp.exp(_attn_scores(qc_ref[:, sl], kh, sd, True, None, True) - lsec[h:h + 1, :])
            ptn = jnp.exp(_attn_scores(qn_ref[:, sl], kh, sd, False, has_next, True) - lsen[h:h + 1, :])
            dv = _nn(ptc.astype(BF16), doc_ref[:, sl]) + _nn(ptn.astype(BF16), don_ref[:, sl])
            dstc = ptc * (_nt(vh, doc_ref[:, sl]) - dlc[h:h + 1, :])
            dstn = ptn * (_nt(vh, don_ref[:, sl]) - dln[h:h + 1, :])
            dk = _nn(dstc.astype(BF16), qc_ref[:, sl]) + _nn(dstn.astype(BF16), qn_ref[:, sl])
            dk_ref[:, sl] = (dk * scale).astype(dk_ref.dtype)
            dv_ref[:, sl] = dv.astype(dv_ref.dtype)

    rowc = pl.BlockSpec((None, DIL_HEADS, B), lambda r, m: (r, 0, m))
    rown = pl.BlockSpec((None, DIL_HEADS, B), lambda r, m: (r, 0, nxt(m)))
    dk, dv = pl.pallas_call(
        body, name=name,
        out_shape=[jax.ShapeDtypeStruct((M, d * W), BF16), jax.ShapeDtypeStruct((M, d * W), BF16)],
        grid=(d, nb),
        in_specs=[pl.BlockSpec((B, W), lambda r, m: (m, r * 6 + g)),
                  pl.BlockSpec((B, W), lambda r, m: (m, r * 6 + 3 + g)),
                  pl.BlockSpec((B, W), lambda r, m: (m, r * 3 + g)),
                  pl.BlockSpec((B, W), lambda r, m: (nxt(m), r * 3 + g)),
                  pl.BlockSpec((B, W), lambda r, m: (m, r)),
                  pl.BlockSpec((B, W), lambda r, m: (nxt(m), r)),
                  rowc, rown, rowc, rown],
        out_specs=[pl.BlockSpec((B, W), lambda r, m: (m, r)), pl.BlockSpec((B, W), lambda r, m: (m, r))],
        compiler_params=_cp(("parallel", "parallel")),
    )(kvv, kvv, qv, qv, dov, dov, lse_t, lse_t, dl_t, dl_t)
    return dk.reshape(L, W), dv.reshape(L, W)


def _head_expand():
    r = lax.broadcasted_iota(jnp.int32, (DIL_HEADS, DIL_W), 0)
    c = lax.broadcasted_iota(jnp.int32, (DIL_HEADS, DIL_W), 1)
    E = ((c >= r * DIL_HEAD_DIM) & (c < (r + 1) * DIL_HEAD_DIM)).astype(F32)
    r2 = lax.broadcasted_iota(jnp.int32, (DIL_W, DIL_HEADS), 0)
    c2 = lax.broadcasted_iota(jnp.int32, (DIL_W, DIL_HEADS), 1)
    Et = ((r2 >= c2 * DIL_HEAD_DIM) & (r2 < (c2 + 1) * DIL_HEAD_DIM)).astype(F32)
    return E, Et


def _merge_weights(l0, l1, l2):
    m = jnp.maximum(jnp.maximum(l0, l1), l2)
    e = [jnp.exp(l - m) for l in (l0, l1, l2)]
    tot = e[0] + e[1] + e[2]
    return [v / tot for v in e]


def _merge_fwd(os_, lses, qz, z_off, *, name):
    def fn(o0, o1, o2, l0, l1, l2, z):
        E, _ = _head_expand()
        w = _merge_weights(l0, l1, l2)
        om = sum(_hi(wg, E) * og for wg, og in zip(w, (o0, o1, o2)))
        return [om * _silu(z)], []

    rows = [(o, 0, DIL_W) for o in os_] + [(l, 0, DIL_HEADS) for l in lses] + [(qz, z_off, DIL_W)]
    return _rowmap(fn, rows, [], [(DIL_W, BF16)], [], name=name)[0]


def _merge_bwd(os_, lses, qz, z_off, dog, *, name):
    def fn(o0, o1, o2, l0, l1, l2, z, dg):
        E, Et = _head_expand()
        dg = dg.astype(F32)
        w = _merge_weights(l0, l1, l2)
        wf = [_hi(wg, E) for wg in w]
        os3 = (o0, o1, o2)
        om = sum(a * b for a, b in zip(wf, os3))
        dom = dg * _silu(z)
        dz = dg * om * _dsilu(z)
        dw = [_hi(dom * og, Et) for og in os3]
        tot = sum(a * b for a, b in zip(w, dw))
        return [wf[0] * dom, wf[1] * dom, wf[2] * dom, w[0] * tot, w[1] * tot, w[2] * tot, dz], []

    rows = ([(o, 0, DIL_W) for o in os_] + [(l, 0, DIL_HEADS) for l in lses] + [(qz, z_off, DIL_W), (dog, 0, DIL_W)])
    outs = [(DIL_W, BF16)] * 3 + [(DIL_HEADS, F32)] * 3 + [(DIL_W, BF16)]
    return _rowmap(fn, rows, [], outs, [], name=name)


def _adamw(gparts, w, m, v, *, name, tr=128):
    n, R, C = gparts.shape
    tr = _pick(R, tr)
    c1 = 1.0 - ADAM_B1 ** ADAM_STEP
    c2 = 1.0 - ADAM_B2 ** ADAM_STEP

    def body(g_ref, w_ref, m_ref, v_ref, go_ref, d_ref, mo_ref, vo_ref):
        g = g_ref[0].astype(F32)
        for i in range(1, n):
            g = g + g_ref[i].astype(F32)
        mn = ADAM_B1 * m_ref[...] + (1.0 - ADAM_B1) * g
        vn = ADAM_B2 * v_ref[...] + (1.0 - ADAM_B2) * jnp.square(g)
        d_ref[...] = -ADAM_LR * ((mn / c1) / (jnp.sqrt(vn / c2) + ADAM_EPS) + ADAM_WD * w_ref[...])
        go_ref[...] = g
        mo_ref[...] = mn
        vo_ref[...] = vn

    blk = pl.BlockSpec((tr, C), lambda i: (i, 0))
    sd = jax.ShapeDtypeStruct((R, C), F32)
    return pl.pallas_call(
        body, name=name, out_shape=[sd, sd, sd, sd], grid=(R // tr,),
        in_specs=[pl.BlockSpec((n, tr, C), lambda i: (0, i, 0)), blk, blk, blk],
        out_specs=[blk, blk, blk, blk],
        compiler_params=_cp(("parallel",)),
    )(gparts, w, m, v)


def _sum_parts(parts, *, name):
    n, R, C = parts.shape

    def body(p_ref, o_ref):
        s = p_ref[0]
        for i in range(1, n):
            s = s + p_ref[i]
        o_ref[...] = s

    return pl.pallas_call(
        body, name=name, out_shape=jax.ShapeDtypeStruct((R, C), F32),
        in_specs=[pl.BlockSpec(memory_space=pltpu.VMEM)], out_specs=pl.BlockSpec(memory_space=pltpu.VMEM),
    )(parts)


def _gather_cols(w_loc, name):
    g = _all_gather(w_loc.astype(BF16), name)
    R, Cs = w_loc.shape
    return jnp.transpose(g, (1, 0, 2)).reshape(R, N_DEV * Cs)


def _scatter_cols(dw, name):
    R, C = dw.shape
    parts = jnp.transpose(dw.reshape(R, N_DEV, C // N_DEV), (1, 0, 2))
    return _all_to_all(parts, name)


def _pad128(n):
    return -(-n // LANE) * LANE


def kernel(x, c, ada_w, ada_b, ln_g, ln_b, a_in_w, a_conv_w, a_conv_b, a_dt_bias, a_A_log, a_D, a_norm_g, a_out_w, kv_w, b_in_w, b_out_w, loss_target, m_ada_w, m_ada_b, m_ln_g, m_ln_b, m_a_in_w, m_a_conv_w, m_a_conv_b, m_a_dt_bias, m_a_A_log, m_a_D, m_a_norm_g, m_a_out_w, m_kv_w, m_b_in_w, m_b_out_w, v_ada_w, v_ada_b, v_ln_g, v_ln_b, v_a_in_w, v_a_conv_w, v_a_conv_b, v_a_dt_bias, v_a_A_log, v_a_D, v_a_norm_g, v_a_out_w, v_kv_w, v_b_in_w, v_b_out_w):
    L, D = x.shape[1], x.shape[2]
    H = a_dt_bias.shape[1]
    d_inner = H * SSD_HEAD_DIM
    G, N, P = SSD_N_GROUPS, SSD_D_STATE, SSD_HEAD_DIM
    K = H // G
    KP = K * P
    conv_dim = d_inner + 2 * G * N
    in_dim = d_inner + conv_dim + H
    in_pad = d_inner + conv_dim + LANE
    assert H <= LANE and KP % LANE == 0 and L % SSD_CHUNK == 0
    me = 4 * lax.axis_index("x") + 2 * lax.axis_index("y") + lax.axis_index("c")
    x2d, tgt = x[0], loss_target[0]

    c_all = _all_gather(c, "ag_c").reshape(N_DEV, D)
    mods = []
    for l in range(DEPTH):
        ab = lax.dynamic_slice(ada_b[l], (me * (3 * D // N_DEV),), (3 * D // N_DEV,))[None]
        mods.append(_matmul(c_all, ada_w[l], name=f"mod{l}", exact=True, a_silu=True, bias=ab))
    mod_all = _all_gather(jnp.stack(mods), "ag_mod")
    mod_me = lax.dynamic_index_in_dim(jnp.transpose(mod_all, (2, 1, 0, 3)).reshape(N_DEV, DEPTH, 3 * D), me, 0, False)
    shift = [mod_me[l, None, 0:D] for l in range(DEPTH)]
    scale = [mod_me[l, None, D:2 * D] for l in range(DEPTH)]
    gate = [mod_me[l, None, 2 * D:3 * D] for l in range(DEPTH)]

    w_in = _gather_cols(a_in_w[0], "ag_a_in")
    w_in = jnp.pad(w_in, ((0, 0), (0, in_pad - in_dim)))
    conv_w = _all_gather(a_conv_w[0], "ag_conv_w")
    conv_w = jnp.transpose(conv_w, (1, 0, 2)).reshape(SSD_CONV_W, conv_dim)
    conv_b = _all_gather(a_conv_b, "ag_conv_b").reshape(1, conv_dim)
    norm_g = _all_gather(a_norm_g, "ag_norm_g").reshape(1, d_inner)
    w_aout = _all_gather(a_out_w[0].astype(BF16), "ag_a_out").reshape(d_inner, D)
    w_kv = _gather_cols(kv_w, "ag_kv")
    w_bin = _gather_cols(b_in_w[0], "ag_b_in")
    w_bout = _gather_cols(b_out_w[0], "ag_b_out")

    def modulate(xin, l, name):
        fn = lambda xv, sc, sh: ([xv * (1.0 + sc) + sh], [])
        return _rowmap(fn, [(xin, 0, D)], [scale[l], shift[l]], [(D, BF16)], [], name=name)[0]

    def ln_fwd(xin, y, l, name):
        def fn(xv, yv, gt, g, b):
            u = DEEPNORM_ALPHA * xv + (1.0 + gt) * yv
            mu = jnp.mean(u, axis=1, keepdims=True)
            uc = u - mu
            var = jnp.mean(uc * uc, axis=1, keepdims=True)
            o = uc * lax.rsqrt(var + LN_EPS) * g + b
            return [o, o], []
        return _rowmap(fn, [(xin, 0, D), (y, 0, D)], [gate[l], ln_g[l:l + 1], ln_b[l:l + 1]],
                       [(D, F32), (D, BF16)], [], name=name)

    def ln_bwd(xin, y, dout, l, name):
        def fn(xv, yv, do, gt, g, b):
            u = DEEPNORM_ALPHA * xv + (1.0 + gt) * yv
            mu = jnp.mean(u, axis=1, keepdims=True)
            uc = u - mu
            var = jnp.mean(uc * uc, axis=1, keepdims=True)
            rs = lax.rsqrt(var + LN_EPS)
            xh = uc * rs
            dxh = do * g
            du = rs * (dxh - jnp.mean(dxh, axis=1, keepdims=True) - xh * jnp.mean(dxh * xh, axis=1, keepdims=True))
            return [DEEPNORM_ALPHA * du, (1.0 + gt) * du], [_csum(du * yv), _csum(do * xh), _csum(do)]
        return _rowmap(fn, [(xin, 0, D), (y, 0, D), (dout, 0, D)], [gate[l], ln_g[l:l + 1], ln_b[l:l + 1]],
                       [(D, F32), (D, BF16)], [(1, D)] * 3, name=name)

    def mod_bwd(xin, dh, dx_acc, l, name):
        def fn(xv, dhv, dxa, sc):
            return [dxa + dhv * (1.0 + sc)], [_csum(dhv * xv), _csum(dhv)]
        return _rowmap(fn, [(xin, 0, D), (dh, 0, D), (dx_acc, 0, D)], [scale[l]], [(D, F32)], [(1, D)] * 2, name=name)

    h0 = modulate(x2d, 0, "mod_h0")
    proj = _matmul(h0, w_in, name="mm_a_in", tn=1152)
    xbc = _conv_fwd(proj, d_inner, conv_dim, conv_w, conv_b, name="conv_fwd")
    dt_raw = proj[:, d_inner + conv_dim:d_inner + conv_dim + H]
    dtc = jnp.transpose(dt_raw.reshape(L, G, K), (1, 0, 2))
    dtr = jnp.transpose(dtc, (0, 2, 1))
    bias_c, alog_c = a_dt_bias.reshape(G, 1, K), a_A_log.reshape(G, 1, K)
    bias_r, alog_r = a_dt_bias.reshape(G, K, 1), a_A_log.reshape(G, K, 1)
    d_full = jnp.repeat(a_D.reshape(H), P)[None]
    ssd_in = (xbc, dtc, dtr, bias_c, bias_r, alog_c, alog_r, d_full)
    y_ssd, states = _ssd_fwd(*ssd_in, d_inner=d_inner, name="ssd_fwd")

    gw = d_inner // G

    def gnorm_fn(yv, zv, g):
        yg = yv * _silu(zv)
        r = lax.rsqrt(jnp.mean(yg * yg, axis=1, keepdims=True) + RMS_EPS)
        return [yg * r * g], []
    yn = _rowmap(gnorm_fn, [(y_ssd, 0, d_inner), (proj, 0, d_inner)], [norm_g], [(d_inner, BF16)], [],
                 name="gnorm_fwd", cw=gw)[0]
    ya = _matmul(yn, w_aout, name="mm_a_out")
    x1, x1b = ln_fwd(x2d, ya, 0, "ln0_fwd")

    kv = _matmul(x1b, w_kv, name="mm_kv", out_dtype=BF16)
    h1 = modulate(x1, 1, "mod_h1")
    q = _matmul(h1, w_bin[:, :DIL_N_GROUPS * DIL_W], name="mm_b_q", out_dtype=BF16)
    z1 = _matmul(h1, w_bin[:, DIL_N_GROUPS * DIL_W:], name="mm_b_z")
    os_, lse_c = [], []
    for g in range(DIL_N_GROUPS):
        o, lse = _attn_fwd(q, kv, g, name=f"attn_fwd{g}")
        os_.append(o)
        lse_c.append(lse)
    lses = [jnp.transpose(l, (1, 0, 2)).reshape(L, DIL_HEADS) for l in lse_c]
    og = _merge_fwd(os_, lses, z1, 0, name="merge_fwd")
    yb = _matmul(og, w_bout, name="mm_b_out")
    x2, _ = ln_fwd(x1, yb, 1, "ln1_fwd")

    def loss_fn(xv, tv):
        e = xv - tv
        return [e * (1.0 / D)], [_csum(e * e) * (0.5 / D)]
    dx2, loss_cols = _rowmap(loss_fn, [(x2, 0, D), (tgt, 0, D)], [], [(D, F32)], [(1, D)], name="loss")
    loss = lax.psum(jnp.sum(loss_cols), ("x", "y", "c"))

    dx1a, dyb, dgate1, dlng1, dlnb1 = ln_bwd(x1, yb, dx2, 1, "ln1_bwd")
    dog = _matmul(dyb, w_bout, name="mm_b_out_dx", tb=True, out_dtype=BF16)
    dw_bout = _matmul(og, dyb, name="mm_b_out_dw", ta=True, out_dtype=BF16)
    do0, do1, do2, dl0, dl1, dl2, dz1 = _merge_bwd(os_, lses, z1, 0, dog, name="merge_bwd")
    dqs, dks, dvs = [], [], []
    for g, (do_g, dl_g) in enumerate(zip((do0, do1, do2), (dl0, dl1, dl2))):
        d = DIL_PATTERNS[g][1]
        dl_c = jnp.transpose(dl_g.reshape(L // d, d, DIL_HEADS), (1, 0, 2))
        dl_t = jnp.transpose(dl_c, (0, 2, 1))
        lse_t = jnp.transpose(lse_c[g], (0, 2, 1))
        dqs.append(_attn_bwd_q(q, kv, do_g, lse_c[g], dl_c, g, name=f"attn_bwd_q{g}"))
        dk, dv = _attn_bwd_kv(q, kv, do_g, lse_t, dl_t, g, name=f"attn_bwd_kv{g}")
        dks.append(dk)
        dvs.append(dv)
    dqz = jnp.concatenate(dqs + [dz1], axis=1)
    dkv = jnp.concatenate(dks + dvs, axis=1)
    dh1 = _matmul(dqz, w_bin, name="mm_b_in_dx", tb=True)
    dw_bin = _matmul(h1, dqz, name="mm_b_in_dw", ta=True, out_dtype=BF16)
    dx1b, dscale1, dshift1 = mod_bwd(x1, dh1, dx1a, 1, "mod1_bwd")
    dx1kv = _matmul(dkv, w_kv, name="mm_kv_dx", tb=True)
    dw_kv = _matmul(x1b, dkv, name="mm_kv_dw", ta=True, out_dtype=BF16)
    dx1 = _rowmap(lambda a, b: ([a + b], []), [(dx1b, 0, D), (dx1kv, 0, D)], [], [(D, F32)], [], name="add_dx1")[0]

    dxa, dya, dgate0, dlng0, dlnb0 = ln_bwd(x2d, ya, dx1, 0, "ln0_bwd")
    dyn = _matmul(dya, w_aout, name="mm_a_out_dx", tb=True)
    dw_aout = _matmul(yn, dya, name="mm_a_out_dw", ta=True, out_dtype=BF16)

    def gnorm_bwd_fn(yv, zv, dn, g):
        sz = _silu(zv)
        yg = yv * sz
        r = lax.rsqrt(jnp.mean(yg * yg, axis=1, keepdims=True) + RMS_EPS)
        nrm = yg * r
        dnn = dn * g
        dyg = r * (dnn - nrm * jnp.mean(dnn * nrm, axis=1, keepdims=True))
        return [dyg * sz, dyg * yv * _dsilu(zv)], [_csum(dn * nrm)]
    dy_ssd, dz, dnorm_g = _rowmap(gnorm_bwd_fn, [(y_ssd, 0, d_inner), (proj, 0, d_inner), (dyn, 0, d_inner)],
                                  [norm_g], [(d_inner, F32), (d_inner, BF16)], [(1, d_inner)], name="gnorm_bwd", cw=gw)
    dxs, dB, dC, ddt_t, dA_t, dD_t = _ssd_bwd(*ssd_in, states, dy_ssd, d_inner=d_inner, name="ssd_bwd")
    dxbc = jnp.concatenate([dxs, dB, dC], axis=1)
    dxbc_raw, dconv_w, dconv_b = _conv_bwd(proj, d_inner, conv_dim, conv_w, conv_b, dxbc, name="conv_bwd")
    tok = jnp.concatenate([jnp.transpose(t, (1, 0, 2)).reshape(L, H) for t in (ddt_t, dA_t, dD_t)], axis=1)
    ddt_raw = tok[:, :H]
    tok_pad = jnp.pad(tok, ((0, 0), (0, _pad128(3 * H) - 3 * H)))
    tok_sum = _rowmap(lambda t: ([], [_csum(t)]), [(tok_pad, 0, tok_pad.shape[1])], [], [], [(1, tok_pad.shape[1])],
                      name="tok_sum")[0]
    ddt_bias = tok_sum[:, :H]
    dA_log = tok_sum[:, H:2 * H] * (-jnp.exp(a_A_log))
    dD = tok_sum[:, 2 * H:3 * H]
    dproj = jnp.concatenate([dz, dxbc_raw, jnp.pad(ddt_raw, ((0, 0), (0, LANE - H))).astype(BF16)], axis=1)
    dh0 = _matmul(dproj, w_in, name="mm_a_in_dx", tb=True, tk=1152)
    dw_in = _matmul(h0, dproj, name="mm_a_in_dw", ta=True, out_dtype=BF16, tn=1152)[:, :in_dim]
    grad_x, dscale0, dshift0 = mod_bwd(x2d, dh0, dxa, 0, "mod0_bwd")

    dmod = jnp.concatenate([dshift0, dscale0, dgate0, dshift1, dscale1, dgate1], axis=1)
    pieces = [dmod, dlng0, dlng1, dlnb0, dlnb1, ddt_bias, dA_log, dD,
              dconv_w.reshape(1, -1), dconv_b, dnorm_g]
    sizes = [p.shape[1] for p in pieces]
    tot = sum(sizes)
    tot_pad = -(-tot // (8 * LANE)) * (8 * LANE)
    packed = jnp.pad(jnp.concatenate(pieces, axis=1), ((0, 0), (0, tot_pad - tot))).reshape(tot_pad // LANE, LANE)
    packed_all = _all_gather(packed, "ag_small")
    small = _sum_parts(packed_all, name="sum_small").reshape(tot_pad)
    offs = np.cumsum([0] + sizes)
    seg = lambda i: small[int(offs[i]):int(offs[i + 1])]
    g_ada_b = seg(0).reshape(DEPTH, 3 * D)
    g_ln_g = jnp.stack([seg(1), seg(2)])
    g_ln_b = jnp.stack([seg(3), seg(4)])
    g_dt_bias, g_A_log, g_D = seg(5)[None], seg(6)[None], seg(7)[None]
    cs = conv_dim // N_DEV
    g_conv_w = lax.dynamic_slice(seg(8).reshape(SSD_CONV_W, conv_dim), (0, me * cs), (SSD_CONV_W, cs))[None]
    g_conv_b = lax.dynamic_slice(seg(9), (me * cs,), (cs,))[None]
    ns = d_inner // N_DEV
    g_norm_g = lax.dynamic_slice(seg(10), (me * ns,), (ns,))[None]

    ms = 3 * D // N_DEV
    dmod_all = packed_all.reshape(N_DEV, tot_pad)[:, :DEPTH * 3 * D].reshape(N_DEV, DEPTH, 3 * D)
    dmod_cols = lax.dynamic_slice(dmod_all, (0, 0, me * ms), (N_DEV, DEPTH, ms))
    c_t = jnp.transpose(c_all)
    g_ada_w = jnp.stack([_matmul(c_t, dmod_cols[:, l], name=f"mm_ada_dw{l}", exact=True, a_silu=True)
                         for l in range(DEPTH)])[None]

    def upd(parts, w, m, v, name):
        shp = w.shape
        r2 = lambda a: a.reshape(-1, shp[-1])
        return [o.reshape(shp) for o in _adamw(parts, r2(w), r2(m), r2(v), name=name)]

    res = {}
    res["ada_w"] = upd(g_ada_w.reshape(1, -1, ms), ada_w, m_ada_w, v_ada_w, "adam_ada_w")
    res["a_in_w"] = upd(_scatter_cols(dw_in, "rs_a_in"), a_in_w, m_a_in_w, v_a_in_w, "adam_a_in")
    res["a_out_w"] = upd(_all_to_all(dw_aout.reshape(N_DEV, d_inner // N_DEV, D), "rs_a_out"),
                         a_out_w, m_a_out_w, v_a_out_w, "adam_a_out")
    res["kv_w"] = upd(_scatter_cols(dw_kv, "rs_kv"), kv_w, m_kv_w, v_kv_w, "adam_kv")
    res["b_in_w"] = upd(_scatter_cols(dw_bin, "rs_b_in"), b_in_w, m_b_in_w, v_b_in_w, "adam_b_in")
    res["b_out_w"] = upd(_scatter_cols(dw_bout, "rs_b_out"), b_out_w, m_b_out_w, v_b_out_w, "adam_b_out")

    small_names = ["ada_b", "ln_g", "ln_b", "a_conv_w", "a_conv_b", "a_dt_bias", "a_A_log", "a_D", "a_norm_g"]
    small_g = [g_ada_b, g_ln_g, g_ln_b, g_conv_w, g_conv_b, g_dt_bias, g_A_log, g_D, g_norm_g]
    small_w = [ada_b, ln_g, ln_b, a_conv_w, a_conv_b, a_dt_bias, a_A_log, a_D, a_norm_g]
    small_m = [m_ada_b, m_ln_g, m_ln_b, m_a_conv_w, m_a_conv_b, m_a_dt_bias, m_a_A_log, m_a_D, m_a_norm_g]
    small_v = [v_ada_b, v_ln_g, v_ln_b, v_a_conv_w, v_a_conv_b, v_a_dt_bias, v_a_A_log, v_a_D, v_a_norm_g]
    ssz = [int(np.prod(w.shape)) for w in small_w]
    stot = sum(ssz)
    spad = -(-stot // (8 * LANE)) * (8 * LANE)

    def pack(arrs, fill):
        flat = jnp.concatenate([a.reshape(-1) for a in arrs])
        return jnp.concatenate([flat, jnp.full((spad - stot,), fill, F32)]).reshape(spad // LANE, LANE)

    sres = _adamw(pack(small_g, 0.0)[None], pack(small_w, 0.0), pack(small_m, 0.0), pack(small_v, 1.0), name="adam_small")
    soffs = np.cumsum([0] + ssz)
    for i, nme in enumerate(small_names):
        res[nme] = [r.reshape(-1)[int(soffs[i]):int(soffs[i + 1])].reshape(small_w[i].shape) for r in sres]

    order = ["ada_w", "ada_b", "ln_g", "ln_b", "a_in_w", "a_conv_w", "a_conv_b", "a_dt_bias", "a_A_log", "a_D",
             "a_norm_g", "a_out_w", "kv_w", "b_in_w", "b_out_w"]
    outs = [loss, grad_x[None]]
    for j in range(4):
        outs += [res[nme][j] for nme in order]
    return tuple(outs)
```

```python
import functools
import math

import numpy as np
import jax
import jax.numpy as jnp
from jax import lax
from jax.experimental import pallas as pl
from jax.experimental.pallas import tpu as pltpu

F32, BF16 = jnp.float32, jnp.bfloat16
HI = lax.Precision.HIGHEST
MESH = pl.DeviceIdType.MESH
N_DEV = 8

SSD_HEAD_DIM = 64
SSD_N_GROUPS = 8
SSD_D_STATE = 128
SSD_CONV_W = 4
SSD_CHUNK = 256
DIL_PATTERNS = ((128, 1), (512, 4), (2048, 16))
DIL_N_GROUPS = 3
DIL_HEADS = 8
DIL_HEAD_DIM = 128
DIL_BLOCK = 128
DIL_W = DIL_HEADS * DIL_HEAD_DIM
DEPTH = 2
DEEPNORM_ALPHA = (2 * DEPTH) ** 0.25
LN_EPS = 1e-5
RMS_EPS = 1e-5
ADAM_LR, ADAM_B1, ADAM_B2, ADAM_EPS, ADAM_WD, ADAM_STEP = 0.001, 0.9, 0.999, 1e-08, 0.01, 10
LANE = 128
NEG = -1e30
VMEM_LIMIT = 56 * 1024 * 1024


def _cp(sem=None):
    return pltpu.CompilerParams(dimension_semantics=sem, vmem_limit_bytes=VMEM_LIMIT)


def _silu(x):
    return x * jax.nn.sigmoid(x)


def _dsilu(x):
    s = jax.nn.sigmoid(x)
    return s * (1.0 + x * (1.0 - s))


def _softplus(x):
    return jnp.maximum(x, 0.0) + jnp.log(1.0 + jnp.exp(-jnp.abs(x)))


def _nt(a, b):
    return lax.dot_general(a, b, (((1,), (1,)), ((), ())), preferred_element_type=F32)


def _nn(a, b):
    return jnp.dot(a, b, preferred_element_type=F32)


def _hi(a, b):
    return jnp.dot(a, b, preferred_element_type=F32, precision=HI)


def _pick(n, pref, align=LANE):
    if n <= pref:
        return n
    for t in range(pref - pref % align, 0, -align):
        if n % t == 0:
            return t
    return n


class _Comm:
    def __init__(self, ins, outs, sems, start, finish):
        self.ins, self.outs, self.sems, self.start, self.finish = ins, outs, sems, start, finish


def _comm_join(comms):
    ins = [a for c in comms for a in c.ins]
    outs = [a for c in comms for a in c.outs]
    sems = [a for c in comms for a in c.sems]

    def split(refs, attr):
        res, i = [], 0
        for c in comms:
            n = len(getattr(c, attr))
            res.append(refs[i:i + n])
            i += n
        return res

    def start(cin, cout, csem):
        for c, a, b, d in zip(comms, split(cin, "ins"), split(cout, "outs"), split(csem, "sems")):
            c.start(a, b, d)

    def finish(cin, cout, csem):
        for c, a, b, d in zip(comms, split(cin, "ins"), split(cout, "outs"), split(csem, "sems")):
            c.finish(a, b, d)

    return _Comm(ins, outs, sems, start, finish)


def _ag_comm(v):
    def parts(x_ref, out_ref, send_sems, recv_sems, local_sem):
        x, y, c = lax.axis_index("x"), lax.axis_index("y"), lax.axis_index("c")
        me, sibling = (x, y, c), (x, y, 1 - c)
        chips = [(1 - x, y), (x, 1 - y), (1 - x, 1 - y)]

        def slab(px, py, pc):
            return out_ref.at[4 * px + 2 * py + pc]

        def copy(k, block, to, src=None):
            return pltpu.make_async_remote_copy(
                src_ref=slab(*block) if src is None else src, dst_ref=slab(*block),
                send_sem=send_sems.at[k], recv_sem=recv_sems.at[k], device_id=to, device_id_type=MESH)

        mine = pltpu.make_async_copy(x_ref, slab(*me), local_sem)
        first = [copy(0, me, sibling, src=x_ref)]
        first += [copy(1 + j, me, (*chip, c), src=x_ref) for j, chip in enumerate(chips)]
        passed = [copy(4 + j, (*chip, c), sibling) for j, chip in enumerate(chips)]
        return me, sibling, chips, c, copy, mine, first, passed

    def start(cin, cout, csem):
        _, _, _, _, _, mine, first, _ = parts(cin[0], cout[0], *csem)
        mine.start()
        for cp in first:
            cp.start()

    def finish(cin, cout, csem):
        me, sibling, chips, c, copy, mine, first, passed = parts(cin[0], cout[0], *csem)
        for j, chip in enumerate(chips):
            copy(1 + j, (*chip, c), me).wait_recv()
            passed[j].start()
        copy(0, sibling, me).wait_recv()
        for j, chip in enumerate(chips):
            copy(4 + j, (*chip, 1 - c), me).wait_recv()
        for cp in first + passed:
            cp.wait_send()
        mine.wait()

    return _Comm([v], [jax.ShapeDtypeStruct((N_DEV,) + v.shape, v.dtype)],
                 [pltpu.SemaphoreType.DMA((7,)), pltpu.SemaphoreType.DMA((7,)), pltpu.SemaphoreType.DMA], start, finish)


def _a2a_comm(v):
    def parts(x_ref, out_ref, send_sems, recv_sems, local_sem):
        x, y, c = lax.axis_index("x"), lax.axis_index("y"), lax.axis_index("c")
        me = 4 * x + 2 * y + c
        mine = pltpu.make_async_copy(x_ref.at[me], out_ref.at[me], local_sem)
        sends, recvs = [], []
        for k, mask in enumerate(range(1, N_DEV)):
            px = 1 - x if (mask >> 2) & 1 else x
            py = 1 - y if (mask >> 1) & 1 else y
            pc = 1 - c if mask & 1 else c
            peer = 4 * px + 2 * py + pc
            sends.append(pltpu.make_async_remote_copy(
                src_ref=x_ref.at[peer], dst_ref=out_ref.at[me],
                send_sem=send_sems.at[k], recv_sem=recv_sems.at[k], device_id=(px, py, pc), device_id_type=MESH))
            recvs.append(pltpu.make_async_remote_copy(
                src_ref=x_ref.at[me], dst_ref=out_ref.at[peer],
                send_sem=send_sems.at[k], recv_sem=recv_sems.at[k], device_id=(px, py, pc), device_id_type=MESH))
        return mine, sends, recvs

    def start(cin, cout, csem):
        mine, sends, _ = parts(cin[0], cout[0], *csem)
        mine.start()
        for cp in sends:
            cp.start()

    def finish(cin, cout, csem):
        mine, sends, recvs = parts(cin[0], cout[0], *csem)
        for cp in recvs:
            cp.wait_recv()
        for cp in sends:
            cp.wait_send()
        mine.wait()

    return _Comm([v], [jax.ShapeDtypeStruct(v.shape, v.dtype)],
                 [pltpu.SemaphoreType.DMA((7,)), pltpu.SemaphoreType.DMA((7,)), pltpu.SemaphoreType.DMA], start, finish)


def _run_comm(comm, name):
    nci, nco = len(comm.ins), len(comm.outs)

    def body(*refs):
        comm.start(refs[:nci], refs[nci:nci + nco], refs[nci + nco:])
        comm.finish(refs[:nci], refs[nci:nci + nco], refs[nci + nco:])

    anyspec = pl.BlockSpec(memory_space=pl.ANY)
    return pl.pallas_call(body, name=name, out_shape=list(comm.outs), in_specs=[anyspec] * nci,
                          out_specs=[anyspec] * nco, scratch_shapes=list(comm.sems))(*comm.ins)


def _all_gather(v, name):
    return _run_comm(_ag_comm(v), name)[0]


def _pcall(body, args, *, name, grid, in_specs, out_specs, out_shape, scratch=(), sem=None, comm=None):
    out_shape, out_specs = list(out_shape), list(out_specs)
    if comm is None:
        return pl.pallas_call(body, name=name, grid=grid, in_specs=list(in_specs), out_specs=out_specs,
                              out_shape=out_shape, scratch_shapes=list(scratch), compiler_params=_cp(sem))(*args)
    ni, no, ns = len(args), len(out_shape), len(scratch)
    nci, nco = len(comm.ins), len(comm.outs)

    def wrapped(*refs):
        ins, cin = refs[:ni], refs[ni:ni + nci]
        o0 = ni + nci
        outs, cout = refs[o0:o0 + no], refs[o0 + no:o0 + no + nco]
        s0 = o0 + no + nco
        scr, csem = refs[s0:s0 + ns], refs[s0 + ns:]
        first = functools.reduce(jnp.logical_and, [pl.program_id(a) == 0 for a in range(len(grid))])
        last = functools.reduce(jnp.logical_and, [pl.program_id(a) == g - 1 for a, g in enumerate(grid)])

        @pl.when(first)
        def _():
            comm.start(cin, cout, csem)

        body(*ins, *outs, *scr)

        @pl.when(last)
        def _():
            comm.finish(cin, cout, csem)

    anyspec = pl.BlockSpec(memory_space=pl.ANY)
    res = pl.pallas_call(
        wrapped, name=name, grid=grid, in_specs=list(in_specs) + [anyspec] * nci,
        out_specs=out_specs + [anyspec] * nco, out_shape=out_shape + list(comm.outs),
        scratch_shapes=list(scratch) + list(comm.sems),
        compiler_params=_cp(("arbitrary",) * len(grid)))(*args, *comm.ins)
    return list(res[:no]) + list(res[no:])


def _matmul(a, b, *, name, ta=False, tb=False, out_dtype=F32, tm=1024, tn=1024, tk=2048,
            exact=False, a_silu=False, bias=None, comm=None):
    (K, M) = a.shape if ta else a.shape[::-1]
    (N, K2) = b.shape if tb else b.shape[::-1]
    assert K == K2, (a.shape, b.shape, ta, tb)
    tm, tn, tk = _pick(M, tm), _pick(N, tn), _pick(K, tk)
    nk = K // tk
    a_spec = pl.BlockSpec((tk, tm), lambda i, j, k: (k, i)) if ta else pl.BlockSpec((tm, tk), lambda i, j, k: (i, k))
    b_spec = pl.BlockSpec((tn, tk), lambda i, j, k: (j, k)) if tb else pl.BlockSpec((tk, tn), lambda i, j, k: (k, j))
    dims = (((0,) if ta else (1,), (1,) if tb else (0,)), ((), ()))
    in_specs, args = [a_spec, b_spec], [a, b]
    if bias is not None:
        in_specs.append(pl.BlockSpec((1, tn), lambda i, j, k: (0, j)))
        args.append(bias)

    def body(*refs):
        a_ref, b_ref = refs[0], refs[1]
        bias_ref = refs[2] if bias is not None else None
        o_ref = refs[2 + (bias is not None)]
        av, bv = a_ref[...], b_ref[...]
        if a_silu:
            av = _silu(av.astype(F32))
        if exact:
            p = lax.dot_general(av.astype(F32), bv.astype(F32), dims, preferred_element_type=F32, precision=HI)
        else:
            p = lax.dot_general(av.astype(BF16), bv.astype(BF16), dims, preferred_element_type=F32)

        def fin(r):
            if bias_ref is not None:
                r = r + bias_ref[...]
            o_ref[...] = r.astype(o_ref.dtype)

        if nk == 1:
            fin(p)
        else:
            acc = refs[-1]
            k = pl.program_id(2)

            @pl.when(k == 0)
            def _():
                acc[...] = p

            @pl.when(k > 0)
            def _():
                acc[...] += p

            @pl.when(k == nk - 1)
            def _():
                fin(acc[...])

    res = _pcall(
        body, args, name=name,
        out_shape=[jax.ShapeDtypeStruct((M, N), out_dtype)],
        grid=(M // tm, N // tn, nk),
        in_specs=in_specs,
        out_specs=[pl.BlockSpec((tm, tn), lambda i, j, k: (i, j))],
        scratch=[pltpu.VMEM((tm, tn), F32)] if nk > 1 else [],
        sem=("parallel", "parallel", "arbitrary"), comm=comm)
    return res[0] if comm is None else res


def _rowmap(fn, rows, bcasts, outs, accs, *, name, tr=256, cw=None, comm=None):
    L = rows[0][0].shape[0]
    tr = _pick(L, tr)
    nr, nb, no, na = len(rows), len(bcasts), len(outs), len(accs)
    if cw is None:
        ncol = 1
        widths = [w for (_, _, w) in rows]
    else:
        wtot = rows[0][2]
        ncol = wtot // cw
        widths = [cw] * nr
    in_specs, args = [], []
    for (arr, off, w), bw in zip(rows, widths):
        assert off % bw == 0
        in_specs.append(pl.BlockSpec((tr, bw), functools.partial(lambda j, i, o: (i, o + j), o=off // bw)))
        args.append(arr)
    for arr in bcasts:
        bw = arr.shape[1] if cw is None else cw
        in_specs.append(pl.BlockSpec((arr.shape[0], bw), lambda j, i: (0, j)))
        args.append(arr)
    out_shape, out_specs = [], []
    for (w, dt) in outs:
        bw = w if cw is None else cw
        out_shape.append(jax.ShapeDtypeStruct((L, w), dt))
        out_specs.append(pl.BlockSpec((tr, bw), lambda j, i: (i, j)))
    for (r, w) in accs:
        bw = w if cw is None else cw
        out_shape.append(jax.ShapeDtypeStruct((r, w), F32))
        out_specs.append(pl.BlockSpec((r, bw), lambda j, i: (0, j)))

    def body(*refs):
        ins = [r[...] for r in refs[:nr + nb]]
        o_refs = refs[nr + nb:nr + nb + no]
        a_refs = refs[nr + nb + no:]
        o, a = fn(*ins)
        for ref, val in zip(o_refs, o):
            ref[...] = val.astype(ref.dtype)
        if na:
            @pl.when(pl.program_id(1) == 0)
            def _():
                for ref in a_refs:
                    ref[...] = jnp.zeros_like(ref)

            for ref, val in zip(a_refs, a):
                ref[...] += val

    return _pcall(body, args, name=name, out_shape=out_shape, grid=(ncol, L // tr), in_specs=in_specs,
                  out_specs=out_specs, sem=("parallel", "arbitrary"), comm=comm)


def _csum(v):
    return jnp.sum(v, axis=0, keepdims=True)


def _shift_rows(v, s, rows):
    if s == 0:
        return v
    n = v.shape[0]
    r = pltpu.roll(v, s % n, 0)
    if s > 0:
        return jnp.where(rows >= s, r, 0.0)
    return jnp.where(rows < n + s, r, 0.0)


def _conv_fwd(proj, off, width, w, b, *, name, tc=256):
    L = proj.shape[0]
    tc = _pick(width, tc)

    def body(x_ref, w_ref, b_ref, o_ref):
        x = x_ref[...]
        rows = lax.broadcasted_iota(jnp.int32, x.shape, 0)
        acc = jnp.zeros_like(x) + b_ref[...]
        for k in range(SSD_CONV_W):
            acc = acc + w_ref[k:k + 1, :] * _shift_rows(x, SSD_CONV_W - 1 - k, rows)
        o_ref[...] = _silu(acc)

    return pl.pallas_call(
        body, name=name, out_shape=jax.ShapeDtypeStruct((L, width), F32), grid=(width // tc,),
        in_specs=[pl.BlockSpec((L, tc), functools.partial(lambda j, o: (0, o + j), o=off // tc)),
                  pl.BlockSpec((SSD_CONV_W, tc), lambda j: (0, j)), pl.BlockSpec((1, tc), lambda j: (0, j))],
        out_specs=pl.BlockSpec((L, tc), lambda j: (0, j)),
        compiler_params=_cp(("parallel",)),
    )(proj, w, b)


def _conv_bwd(proj, off, width, w, b, dy, *, name, tc=256, comm=None):
    L = proj.shape[0]
    tc = _pick(width, tc)

    def body(x_ref, w_ref, b_ref, dy_ref, dx_ref, dw_ref, db_ref):
        x = x_ref[...]
        rows = lax.broadcasted_iota(jnp.int32, x.shape, 0)
        xs = [_shift_rows(x, SSD_CONV_W - 1 - k, rows) for k in range(SSD_CONV_W)]
        pre = jnp.zeros_like(x) + b_ref[...]
        for k in range(SSD_CONV_W):
            pre = pre + w_ref[k:k + 1, :] * xs[k]
        dpre = dy_ref[...] * _dsilu(pre)
        dx = jnp.zeros_like(x)
        for k in range(SSD_CONV_W):
            dx = dx + w_ref[k:k + 1, :] * _shift_rows(dpre, -(SSD_CONV_W - 1 - k), rows)
            dw_ref[k:k + 1, :] = _csum(dpre * xs[k])
        dx_ref[...] = dx.astype(dx_ref.dtype)
        db_ref[...] = _csum(dpre)

    return _pcall(
        body, (proj, w, b, dy), name=name,
        out_shape=[jax.ShapeDtypeStruct((L, width), BF16), jax.ShapeDtypeStruct((SSD_CONV_W, width), F32),
                   jax.ShapeDtypeStruct((1, width), F32)],
        grid=(width // tc,),
        in_specs=[pl.BlockSpec((L, tc), functools.partial(lambda j, o: (0, o + j), o=off // tc)),
                  pl.BlockSpec((SSD_CONV_W, tc), lambda j: (0, j)), pl.BlockSpec((1, tc), lambda j: (0, j)),
                  pl.BlockSpec((L, tc), lambda j: (0, j))],
        out_specs=[pl.BlockSpec((L, tc), lambda j: (0, j)), pl.BlockSpec((SSD_CONV_W, tc), lambda j: (0, j)),
                   pl.BlockSpec((1, tc), lambda j: (0, j))],
        sem=("parallel",), comm=comm)


def _tri(Q):
    ri = lax.broadcasted_iota(jnp.int32, (Q, Q), 0)
    ci = lax.broadcasted_iota(jnp.int32, (Q, Q), 1)
    return ri >= ci, ri <= ci


def _ssd_prep(dt_raw, bias, alog, *, name):
    L, W = dt_raw.shape
    Q = SSD_CHUNK

    def body(r_ref, b_ref, al_ref, dt_ref, a_ref):
        lower, _ = _tri(Q)
        dt = _softplus(r_ref[...] + b_ref[...])
        dt_ref[...] = dt
        a_ref[...] = _hi(lower.astype(F32), dt * (-jnp.exp(al_ref[...])))

    blk = pl.BlockSpec((Q, W), lambda c: (c, 0))
    one = pl.BlockSpec((1, W), lambda c: (0, 0))
    sd = jax.ShapeDtypeStruct((L, W), F32)
    return _pcall(body, (dt_raw, bias, alog), name=name, out_shape=[sd, sd], grid=(L // Q,),
                  in_specs=[blk, one, one], out_specs=[blk, blk], sem=("parallel",))


def _ssd_post(da, s1, dt, dt_raw, bias, alog, *, name):
    L, W = da.shape
    Q = SSD_CHUNK

    def body(da_ref, s1_ref, dt_ref, r_ref, b_ref, al_ref, o_ref, db_ref, dal_ref):
        _, upper = _tri(Q)
        A = -jnp.exp(al_ref[...])
        ddtA = _hi(upper.astype(F32), da_ref[...])
        ddt_raw = (ddtA * A + s1_ref[...]) * jax.nn.sigmoid(r_ref[...] + b_ref[...])
        o_ref[...] = ddt_raw.astype(o_ref.dtype)

        @pl.when(pl.program_id(0) == 0)
        def _():
            db_ref[...] = jnp.zeros_like(db_ref)
            dal_ref[...] = jnp.zeros_like(dal_ref)

        db_ref[...] += _csum(ddt_raw)
        dal_ref[...] += _csum(ddtA * dt_ref[...]) * A

    blk = pl.BlockSpec((Q, W), lambda c: (c, 0))
    one = pl.BlockSpec((1, W), lambda c: (0, 0))
    return _pcall(body, (da, s1, dt, dt_raw, bias, alog), name=name,
                  out_shape=[jax.ShapeDtypeStruct((L, W), BF16), jax.ShapeDtypeStruct((1, W), F32),
                             jax.ShapeDtypeStruct((1, W), F32)],
                  grid=(L // Q,), in_specs=[blk, blk, blk, blk, one, one], out_specs=[blk, one, one],
                  sem=("arbitrary",))


def _head_sum(v, K, KP):
    P = KP // K
    t_r = lax.broadcasted_iota(jnp.int32, (KP, K), 0)
    t_c = lax.broadcasted_iota(jnp.int32, (KP, K), 1)
    Et = ((t_r >= t_c * P) & (t_r < (t_c + 1) * P)).astype(BF16)
    hi = v.astype(BF16)
    lo = (v - hi.astype(F32)).astype(BF16)
    return _nn(hi, Et) + _nn(lo, Et)


def _half_masks():
    li = lax.broadcasted_iota(jnp.int32, (1, LANE), 1)
    return [(li < SSD_HEAD_DIM).astype(F32), (li >= SSD_HEAD_DIM).astype(F32)]


def _ssd_specs(K, KP, d_inner, rev, nc):
    Q, N, G = SSD_CHUNK, SSD_D_STATE, SSD_N_GROUPS
    cidx = (lambda c: nc - 1 - c) if rev else (lambda c: c)
    b_off, c_off = d_inner // N, d_inner // N + G
    return [
        pl.BlockSpec((Q, KP), lambda g, c: (cidx(c), g)),
        pl.BlockSpec((Q, N), lambda g, c: (cidx(c), b_off + g)),
        pl.BlockSpec((Q, N), lambda g, c: (cidx(c), c_off + g)),
        pl.BlockSpec((Q, KP), lambda g, c: (cidx(c), g)),
        pl.BlockSpec((Q, KP), lambda g, c: (cidx(c), g)),
        pl.BlockSpec((None, Q, K), lambda g, c: (g, cidx(c), 0)),
        pl.BlockSpec((None, K, Q), lambda g, c: (g, 0, cidx(c))),
        pl.BlockSpec((1, KP), lambda g, c: (0, g)),
    ]


def _ssd_fwd(xbc, dt_f, a_f, a_c, a_r, d_full, *, d_inner, name, comm=None):
    L = xbc.shape[0]
    G, N, Q, P = SSD_N_GROUPS, SSD_D_STATE, SSD_CHUNK, SSD_HEAD_DIM
    KP = d_inner // G
    K = KP // P
    nc = L // Q
    npair = KP // LANE

    def body(xs_ref, b_ref, c_ref, dtf_ref, af_ref, ac_ref, ar_ref, df_ref, y_ref, st_ref, S):
        @pl.when(pl.program_id(1) == 0)
        def _():
            S[...] = jnp.zeros_like(S)

        lower, _ = _tri(Q)
        st_ref[...] = S[...]
        xs = xs_ref[...]
        Bm, Cm = b_ref[...], c_ref[...]
        Bb, Cb = Bm.astype(BF16), Cm.astype(BF16)
        a_f, a_c, a_r = af_ref[...], ac_ref[...], ar_ref[...]
        X = xs * dtf_ref[...]
        ea = jnp.exp(a_f)
        alast = a_f[Q - 1:Q, :]
        tail = jnp.exp(alast - a_f)
        cb = _nt(Cb, Bb)
        Sv = S[...]
        yoff = _nn(Cb, Sv.astype(BF16)) * ea
        skip = xs * df_ref[...]
        masks = _half_masks()
        for pr in range(npair):
            Xp = X[:, pr * LANE:(pr + 1) * LANE]
            acc = yoff[:, pr * LANE:(pr + 1) * LANE] + skip[:, pr * LANE:(pr + 1) * LANE]
            for hh in range(2):
                k = 2 * pr + hh
                seg = a_c[:, k:k + 1] - a_r[k:k + 1, :]
                dec = jnp.where(lower, jnp.exp(jnp.minimum(seg, 0.0)), 0.0)
                acc = acc + _nn((cb * dec).astype(BF16), (Xp * masks[hh]).astype(BF16))
            y_ref[:, pr * LANE:(pr + 1) * LANE] = acc
        Bt = Bm.T
        S[...] = Sv * jnp.exp(alast) + _nn(Bt.astype(BF16), (X * tail).astype(BF16))

    return _pcall(
        body, (xbc, xbc, xbc, dt_f, a_f, a_c, a_r, d_full), name=name,
        out_shape=[jax.ShapeDtypeStruct((L, d_inner), F32), jax.ShapeDtypeStruct((G, nc, N, KP), F32)],
        grid=(G, nc),
        in_specs=_ssd_specs(K, KP, d_inner, False, nc),
        out_specs=[pl.BlockSpec((Q, KP), lambda g, c: (c, g)), pl.BlockSpec((None, None, N, KP), lambda g, c: (g, c, 0, 0))],
        scratch=[pltpu.VMEM((N, KP), F32)],
        sem=("parallel", "arbitrary"), comm=comm)


def _ssd_bwd(xbc, dt_f, a_f, a_c, a_r, d_full, states, dy, *, d_inner, name, comm=None):
    L = xbc.shape[0]
    G, N, Q, P = SSD_N_GROUPS, SSD_D_STATE, SSD_CHUNK, SSD_HEAD_DIM
    KP = d_inner // G
    K = KP // P
    nc = L // Q
    npair = KP // LANE

    def body(xs_ref, b_ref, c_ref, dtf_ref, af_ref, ac_ref, ar_ref, df_ref, st_ref, dy_ref,
             dxs_ref, db_ref, dc_ref, da_ref, s1_ref, dd_ref, dS):
        @pl.when(pl.program_id(1) == 0)
        def _():
            dS[...] = jnp.zeros_like(dS)
            dd_ref[...] = jnp.zeros_like(dd_ref)

        lower, upper = _tri(Q)
        a_f, a_c, a_r, dt_f = af_ref[...], ac_ref[...], ar_ref[...], dtf_ref[...]
        xs = xs_ref[...]
        Bm, Cm = b_ref[...], c_ref[...]
        Bb, Cb = Bm.astype(BF16), Cm.astype(BF16)
        dY = dy_ref[...]
        X = xs * dt_f
        ea = jnp.exp(a_f)
        alast = a_f[Q - 1:Q, :]
        tail = jnp.exp(alast - a_f)
        el = jnp.exp(alast)
        Sv, dSn = st_ref[...], dS[...]
        Sb, dSb = Sv.astype(BF16), dSn.astype(BF16)
        cb = _nt(Cb, Bb)
        cbT = _nt(Bb, Cb)
        yoff_raw = _nn(Cb, Sb)
        dYe = dY * ea
        dC = _nt(dYe.astype(BF16), Sb)
        dS[...] = dSn * el + _nn(Cm.T.astype(BF16), dYe.astype(BF16))
        Gx = _nn(Bb, dSb)
        dB = _nt((X * tail).astype(BF16), dSb)
        dtl = Gx * X * tail
        da_f = dYe * yoff_raw - dtl
        dalast_f = _csum(dtl) + _csum(dSn * Sv) * el
        da_c = _head_sum(da_f, K, KP)
        onek = lax.broadcasted_iota(jnp.int32, (1, K), 1)
        masks = _half_masks()
        dcb = jnp.zeros((Q, Q), F32)
        dcbT = jnp.zeros((Q, Q), F32)
        dX_parts = []
        for pr in range(npair):
            Xp = X[:, pr * LANE:(pr + 1) * LANE]
            dYp = dY[:, pr * LANE:(pr + 1) * LANE]
            dXp = Gx[:, pr * LANE:(pr + 1) * LANE] * tail[:, pr * LANE:(pr + 1) * LANE]
            for hh in range(2):
                k = 2 * pr + hh
                Xk = (Xp * masks[hh]).astype(BF16)
                dYk = (dYp * masks[hh]).astype(BF16)
                seg = a_c[:, k:k + 1] - a_r[k:k + 1, :]
                dec = jnp.where(lower, jnp.exp(jnp.minimum(seg, 0.0)), 0.0)
                decT = jnp.where(upper, jnp.exp(jnp.minimum(-seg, 0.0)), 0.0)
                dM = _nt(dYk, Xk)
                dMT = _nt(Xk, dYk)
                MT = cbT * decT
                dcb = dcb + dM * dec
                dcbT = dcbT + dMT * decT
                da_k = jnp.sum(dM * cb * dec, axis=1, keepdims=True) - jnp.sum(dMT * MT, axis=1, keepdims=True)
                da_c = da_c + da_k * (onek == k).astype(F32)
                dXp = dXp + _nn(MT.astype(BF16), dYk)
            dX_parts.append(dXp)
        dX = jnp.concatenate(dX_parts, axis=1) if npair > 1 else dX_parts[0]
        dC = dC + _nn(dcb.astype(BF16), Bb)
        dB = dB + _nn(dcbT.astype(BF16), Cb)
        lastrow = (lax.broadcasted_iota(jnp.int32, (Q, 1), 0) == Q - 1).astype(F32)
        da_ref[...] = da_c + lastrow * _head_sum(dalast_f, K, KP)
        s1_ref[...] = _head_sum(dX * xs, K, KP)
        dd_ref[...] += _csum(dY * xs)
        dxs_ref[...] = dX * dt_f + dY * df_ref[...]
        db_ref[...] = dB
        dc_ref[...] = dC

    rc = lambda c: nc - 1 - c
    tok = jax.ShapeDtypeStruct((G, L, K), F32)
    tok_spec = pl.BlockSpec((None, Q, K), lambda g, c: (g, rc(c), 0))
    return _pcall(
        body, (xbc, xbc, xbc, dt_f, a_f, a_c, a_r, d_full, states, dy), name=name,
        out_shape=[jax.ShapeDtypeStruct((L, d_inner), F32), jax.ShapeDtypeStruct((L, G * N), F32),
                   jax.ShapeDtypeStruct((L, G * N), F32), tok, tok, jax.ShapeDtypeStruct((1, d_inner), F32)],
        grid=(G, nc),
        in_specs=_ssd_specs(K, KP, d_inner, True, nc) + [
            pl.BlockSpec((None, None, N, KP), lambda g, c: (g, rc(c), 0, 0)),
            pl.BlockSpec((Q, KP), lambda g, c: (rc(c), g))],
        out_specs=[pl.BlockSpec((Q, KP), lambda g, c: (rc(c), g)), pl.BlockSpec((Q, N), lambda g, c: (rc(c), g)),
                   pl.BlockSpec((Q, N), lambda g, c: (rc(c), g)), tok_spec, tok_spec,
                   pl.BlockSpec((1, KP), lambda g, c: (0, g))],
        scratch=[pltpu.VMEM((N, KP), F32)],
        sem=("parallel", "arbitrary"), comm=comm)


def _slopes():
    n = DIL_N_GROUPS * DIL_HEADS
    s = 2.0 ** (-8.0 * np.arange(1, n + 1) / n)
    return s.reshape(DIL_N_GROUPS, DIL_HEADS).astype(np.float32)


def _attn_scores(qh, kh, slope_d, cur, valid_blk, transposed):
    B = DIL_BLOCK
    scale = DIL_HEAD_DIM ** -0.5
    if transposed:
        s = _nt(kh, qh) * scale
        kj = lax.broadcasted_iota(jnp.int32, (B, B), 0)
        qi = lax.broadcasted_iota(jnp.int32, (B, B), 1)
    else:
        s = _nt(qh, kh) * scale
        qi = lax.broadcasted_iota(jnp.int32, (B, B), 0)
        kj = lax.broadcasted_iota(jnp.int32, (B, B), 1)
    if cur:
        delta = qi - kj
        ok = kj <= qi
    else:
        delta = qi + B - kj
        ok = (kj >= qi) & valid_blk
    return jnp.where(ok, s - slope_d * delta.astype(F32), NEG)


def _attn_fwd(q, kv, g, *, name):
    L = q.shape[0]
    window, d = DIL_PATTERNS[g]
    assert window // d == DIL_BLOCK and L % (d * DIL_BLOCK) == 0
    M, B, W = L // d, DIL_BLOCK, DIL_W
    nb = M // B
    slopes = _slopes()[g]
    qv = q.reshape(M, d * DIL_N_GROUPS * W)
    kvv = kv.reshape(M, d * 2 * DIL_N_GROUPS * W)
    prev = lambda m: jnp.maximum(m - 1, 0)

    def body(q_ref, kc_ref, kp_ref, vc_ref, vp_ref, o_ref, lse_ref):
        has_prev = pl.program_id(1) > 0
        lse = jnp.zeros((B, DIL_HEADS), F32)
        onek = lax.broadcasted_iota(jnp.int32, (1, DIL_HEADS), 1)
        for h in range(DIL_HEADS):
            sl = slice(h * DIL_HEAD_DIM, (h + 1) * DIL_HEAD_DIM)
            qh = q_ref[:, sl]
            sd = float(slopes[h]) * d
            sc = _attn_scores(qh, kc_ref[:, sl], sd, True, None, False)
            sp = _attn_scores(qh, kp_ref[:, sl], sd, False, has_prev, False)
            m = jnp.maximum(jnp.max(sc, axis=1, keepdims=True), jnp.max(sp, axis=1, keepdims=True))
            pc, pp = jnp.exp(sc - m), jnp.exp(sp - m)
            den = jnp.sum(pc, axis=1, keepdims=True) + jnp.sum(pp, axis=1, keepdims=True)
            o = _nn(pc.astype(BF16), vc_ref[:, sl]) + _nn(pp.astype(BF16), vp_ref[:, sl])
            o_ref[:, sl] = o / den
            lse = lse + (m + jnp.log(den)) * (onek == h).astype(F32)
        lse_ref[...] = lse

    o, lse = pl.pallas_call(
        body, name=name,
        out_shape=[jax.ShapeDtypeStruct((M, d * W), F32), jax.ShapeDtypeStruct((d, M, DIL_HEADS), F32)],
        grid=(d, nb),
        in_specs=[pl.BlockSpec((B, W), lambda r, m: (m, r * 3 + g)),
                  pl.BlockSpec((B, W), lambda r, m: (m, r * 6 + g)),
                  pl.BlockSpec((B, W), lambda r, m: (prev(m), r * 6 + g)),
                  pl.BlockSpec((B, W), lambda r, m: (m, r * 6 + 3 + g)),
                  pl.BlockSpec((B, W), lambda r, m: (prev(m), r * 6 + 3 + g))],
        out_specs=[pl.BlockSpec((B, W), lambda r, m: (m, r)),
                   pl.BlockSpec((None, B, DIL_HEADS), lambda r, m: (r, m, 0))],
        compiler_params=_cp(("parallel", "parallel")),
    )(qv, kvv, kvv, kvv, kvv)
    return o.reshape(L, W), lse


def _attn_bwd_q(q, kv, do, lse, dl, g, *, name):
    L = q.shape[0]
    _, d = DIL_PATTERNS[g]
    M, B, W = L // d, DIL_BLOCK, DIL_W
    nb = M // B
    slopes = _slopes()[g]
    scale = DIL_HEAD_DIM ** -0.5
    qv = q.reshape(M, d * DIL_N_GROUPS * W)
    kvv = kv.reshape(M, d * 2 * DIL_N_GROUPS * W)
    dov = do.reshape(M, d * W)
    prev = lambda m: jnp.maximum(m - 1, 0)

    def body(q_ref, kc_ref, kp_ref, vc_ref, vp_ref, do_ref, lse_ref, dl_ref, dq_ref):
        has_prev = pl.program_id(1) > 0
        lse_all, dl_all = lse_ref[...], dl_ref[...]
        for h in range(DIL_HEADS):
            sl = slice(h * DIL_HEAD_DIM, (h + 1) * DIL_HEAD_DIM)
            qh, doh = q_ref[:, sl], do_ref[:, sl]
            sd = float(slopes[h]) * d
            lse_h, dl_h = lse_all[:, h:h + 1], dl_all[:, h:h + 1]
            pc = jnp.exp(_attn_scores(qh, kc_ref[:, sl], sd, True, None, False) - lse_h)
            pp = jnp.exp(_attn_scores(qh, kp_ref[:, sl], sd, False, has_prev, False) - lse_h)
            dsc = pc * (_nt(doh, vc_ref[:, sl]) - dl_h)
            dsp = pp * (_nt(doh, vp_ref[:, sl]) - dl_h)
            dq = _nn(dsc.astype(BF16), kc_ref[:, sl]) + _nn(dsp.astype(BF16), kp_ref[:, sl])
            dq_ref[:, sl] = (dq * scale).astype(dq_ref.dtype)

    col = pl.BlockSpec((None, B, DIL_HEADS), lambda r, m: (r, m, 0))
    dq = pl.pallas_call(
        body, name=name,
        out_shape=jax.ShapeDtypeStruct((M, d * W), BF16),
        grid=(d, nb),
        in_specs=[pl.BlockSpec((B, W), lambda r, m: (m, r * 3 + g)),
                  pl.BlockSpec((B, W), lambda r, m: (m, r * 6 + g)),
                  pl.BlockSpec((B, W), lambda r, m: (prev(m), r * 6 + g)),
                  pl.BlockSpec((B, W), lambda r, m: (m, r * 6 + 3 + g)),
                  pl.BlockSpec((B, W), lambda r, m: (prev(m), r * 6 + 3 + g)),
                  pl.BlockSpec((B, W), lambda r, m: (m, r)), col, col],
        out_specs=pl.BlockSpec((B, W), lambda r, m: (m, r)),
        compiler_params=_cp(("parallel", "parallel")),
    )(qv, kvv, kvv, kvv, kvv, dov, lse, dl)
    return dq.reshape(L, W)


def _attn_bwd_kv(q, kv, do, lse_t, dl_t, g, *, name):
    L = q.shape[0]
    _, d = DIL_PATTERNS[g]
    M, B, W = L // d, DIL_BLOCK, DIL_W
    nb = M // B
    slopes = _slopes()[g]
    scale = DIL_HEAD_DIM ** -0.5
    qv = q.reshape(M, d * DIL_N_GROUPS * W)
    kvv = kv.reshape(M, d * 2 * DIL_N_GROUPS * W)
    dov = do.reshape(M, d * W)
    nxt = lambda m: jnp.minimum(m + 1, nb - 1)

    def body(k_ref, v_ref, qc_ref, qn_ref, doc_ref, don_ref, lsec_ref, lsen_ref, dlc_ref, dln_ref, dk_ref, dv_ref):
        has_next = pl.program_id(1) < nb - 1
        lsec, lsen, dlc, dln = lsec_ref[...], lsen_ref[...], dlc_ref[...], dln_ref[...]
        for h in range(DIL_HEADS):
            sl = slice(h * DIL_HEAD_DIM, (h + 1) * DIL_HEAD_DIM)
            kh, vh = k_ref[:, sl], v_ref[:, sl]
            sd = float(slopes[h]) * d
            ptc = jnp.exp(_attn_scores(qc_ref[:, sl], kh, sd, True, None, True) - lsec[h:h + 1, :])
            ptn = jnp.exp(_attn_scores(qn_ref[:, sl], kh, sd, False, has_next, True) - lsen[h:h + 1, :])
            dv = _nn(ptc.astype(BF16), doc_ref[:, sl]) + _nn(ptn.astype(BF16), don_ref[:, sl])
            dstc = ptc * (_nt(vh, doc_ref[:, sl]) - dlc[h:h + 1, :])
            dstn = ptn * (_nt(vh, don_ref[:, sl]) - dln[h:h + 1, :])
            dk = _nn(dstc.astype(BF16), qc_ref[:, sl]) + _nn(dstn.astype(BF16), qn_ref[:, sl])
            dk_ref[:, sl] = (dk * scale).astype(dk_ref.dtype)
            dv_ref[:, sl] = dv.astype(dv_ref.dtype)

    rowc = pl.BlockSpec((None, DIL_HEADS, B), lambda r, m: (r, 0, m))
    rown = pl.BlockSpec((None, DIL_HEADS, B), lambda r, m: (r, 0, nxt(m)))
    dk, dv = pl.pallas_call(
        body, name=name,
        out_shape=[jax.ShapeDtypeStruct((M, d * W), BF16), jax.ShapeDtypeStruct((M, d * W), BF16)],
        grid=(d, nb),
        in_specs=[pl.BlockSpec((B, W), lambda r, m: (m, r * 6 + g)),
                  pl.BlockSpec((B, W), lambda r, m: (m, r * 6 + 3 + g)),
                  pl.BlockSpec((B, W), lambda r, m: (m, r * 3 + g)),
                  pl.BlockSpec((B, W), lambda r, m: (nxt(m), r * 3 + g)),
                  pl.BlockSpec((B, W), lambda r, m: (m, r)),
                  pl.BlockSpec((B, W), lambda r, m: (nxt(m), r)),
                  rowc, rown, rowc, rown],
        out_specs=[pl.BlockSpec((B, W), lambda r, m: (m, r)), pl.BlockSpec((B, W), lambda r, m: (m, r))],
        compiler_params=_cp(("parallel", "parallel")),
    )(kvv, kvv, qv, qv, dov, dov, lse_t, lse_t, dl_t, dl_t)
    return dk.reshape(L, W), dv.reshape(L, W)


def _head_expand():
    r = lax.broadcasted_iota(jnp.int32, (DIL_HEADS, DIL_W), 0)
    c = lax.broadcasted_iota(jnp.int32, (DIL_HEADS, DIL_W), 1)
    E = ((c >= r * DIL_HEAD_DIM) & (c < (r + 1) * DIL_HEAD_DIM)).astype(F32)
    r2 = lax.broadcasted_iota(jnp.int32, (DIL_W, DIL_HEADS), 0)
    c2 = lax.broadcasted_iota(jnp.int32, (DIL_W, DIL_HEADS), 1)
    Et = ((r2 >= c2 * DIL_HEAD_DIM) & (r2 < (c2 + 1) * DIL_HEAD_DIM)).astype(F32)
    return E, Et


def _merge_weights(l0, l1, l2):
    m = jnp.maximum(jnp.maximum(l0, l1), l2)
    e = [jnp.exp(l - m) for l in (l0, l1, l2)]
    tot = e[0] + e[1] + e[2]
    return [v / tot for v in e]


def _merge_fwd(os_, lses, qz, z_off, *, name):
    def fn(o0, o1, o2, l0, l1, l2, z):
        E, _ = _head_expand()
        w = _merge_weights(l0, l1, l2)
        om = sum(_hi(wg, E) * og for wg, og in zip(w, (o0, o1, o2)))
        return [om * _silu(z)], []

    rows = [(o, 0, DIL_W) for o in os_] + [(l, 0, DIL_HEADS) for l in lses] + [(qz, z_off, DIL_W)]
    return _rowmap(fn, rows, [], [(DIL_W, BF16)], [], name=name)[0]


def _merge_bwd(os_, lses, qz, z_off, dog, *, name, comm=None):
    def fn(o0, o1, o2, l0, l1, l2, z, dg):
        E, Et = _head_expand()
        dg = dg.astype(F32)
        w = _merge_weights(l0, l1, l2)
        wf = [_hi(wg, E) for wg in w]
        os3 = (o0, o1, o2)
        om = sum(a * b for a, b in zip(wf, os3))
        dom = dg * _silu(z)
        dz = dg * om * _dsilu(z)
        dw = [_hi(dom * og, Et) for og in os3]
        tot = sum(a * b for a, b in zip(w, dw))
        return [wf[0] * dom, wf[1] * dom, wf[2] * dom, w[0] * tot, w[1] * tot, w[2] * tot, dz], []

    rows = ([(o, 0, DIL_W) for o in os_] + [(l, 0, DIL_HEADS) for l in lses] + [(qz, z_off, DIL_W), (dog, 0, DIL_W)])
    outs = [(DIL_W, BF16)] * 3 + [(DIL_HEADS, F32)] * 3 + [(DIL_W, BF16)]
    return _rowmap(fn, rows, [], outs, [], name=name, comm=comm)


def _adamw(gparts, w, m, v, *, name, tr=128):
    n, R, C = gparts.shape
    tr = _pick(R, tr)
    c1 = 1.0 - ADAM_B1 ** ADAM_STEP
    c2 = 1.0 - ADAM_B2 ** ADAM_STEP

    def body(g_ref, w_ref, m_ref, v_ref, go_ref, d_ref, mo_ref, vo_ref):
        g = g_ref[0].astype(F32)
        for i in range(1, n):
            g = g + g_ref[i].astype(F32)
        mn = ADAM_B1 * m_ref[...] + (1.0 - ADAM_B1) * g
        vn = ADAM_B2 * v_ref[...] + (1.0 - ADAM_B2) * jnp.square(g)
        d_ref[...] = -ADAM_LR * ((mn / c1) / (jnp.sqrt(vn / c2) + ADAM_EPS) + ADAM_WD * w_ref[...])
        go_ref[...] = g
        mo_ref[...] = mn
        vo_ref[...] = vn

    blk = pl.BlockSpec((tr, C), lambda i: (i, 0))
    sd = jax.ShapeDtypeStruct((R, C), F32)
    return pl.pallas_call(
        body, name=name, out_shape=[sd, sd, sd, sd], grid=(R // tr,),
        in_specs=[pl.BlockSpec((n, tr, C), lambda i: (0, i, 0)), blk, blk, blk],
        out_specs=[blk, blk, blk, blk],
        compiler_params=_cp(("parallel",)),
    )(gparts, w, m, v)


def _sum_parts(parts, *, name):
    n, R, C = parts.shape

    def body(p_ref, o_ref):
        s = p_ref[0]
        for i in range(1, n):
            s = s + p_ref[i]
        o_ref[...] = s

    return pl.pallas_call(
        body, name=name, out_shape=jax.ShapeDtypeStruct((R, C), F32),
        in_specs=[pl.BlockSpec(memory_space=pltpu.VMEM)], out_specs=pl.BlockSpec(memory_space=pltpu.VMEM),
    )(parts)


def _cols_from(g):
    _, R, Cs = g.shape
    return jnp.transpose(g, (1, 0, 2)).reshape(R, N_DEV * Cs)


def _col_parts(dw):
    R, C = dw.shape
    return jnp.transpose(dw.reshape(R, N_DEV, C // N_DEV), (1, 0, 2))


def _pad128(n):
    return -(-n // LANE) * LANE


def kernel(x, c, ada_w, ada_b, ln_g, ln_b, a_in_w, a_conv_w, a_conv_b, a_dt_bias, a_A_log, a_D, a_norm_g, a_out_w, kv_w, b_in_w, b_out_w, loss_target, m_ada_w, m_ada_b, m_ln_g, m_ln_b, m_a_in_w, m_a_conv_w, m_a_conv_b, m_a_dt_bias, m_a_A_log, m_a_D, m_a_norm_g, m_a_out_w, m_kv_w, m_b_in_w, m_b_out_w, v_ada_w, v_ada_b, v_ln_g, v_ln_b, v_a_in_w, v_a_conv_w, v_a_conv_b, v_a_dt_bias, v_a_A_log, v_a_D, v_a_norm_g, v_a_out_w, v_kv_w, v_b_in_w, v_b_out_w):
    L, D = x.shape[1], x.shape[2]
    H = a_dt_bias.shape[1]
    d_inner = H * SSD_HEAD_DIM
    G, N, P = SSD_N_GROUPS, SSD_D_STATE, SSD_HEAD_DIM
    K = H // G
    KP = K * P
    conv_dim = d_inner + 2 * G * N
    in_dim = d_inner + conv_dim + H
    in_pad = d_inner + conv_dim + LANE
    assert H <= LANE and KP % LANE == 0 and L % SSD_CHUNK == 0
    me = 4 * lax.axis_index("x") + 2 * lax.axis_index("y") + lax.axis_index("c")
    x2d, tgt = x[0], loss_target[0]

    c_all = _all_gather(c, "ag_c").reshape(N_DEV, D)
    mods = []
    for l in range(DEPTH):
        ab = lax.dynamic_slice(ada_b[l], (me * (3 * D // N_DEV),), (3 * D // N_DEV,))[None]
        mods.append(_matmul(c_all, ada_w[l], name=f"mod{l}", exact=True, a_silu=True, bias=ab))
    mod_all = _all_gather(jnp.stack(mods), "ag_mod")
    mod_me = lax.dynamic_index_in_dim(jnp.transpose(mod_all, (2, 1, 0, 3)).reshape(N_DEV, DEPTH, 3 * D), me, 0, False)
    shift = [mod_me[l, None, 0:D] for l in range(DEPTH)]
    scale = [mod_me[l, None, D:2 * D] for l in range(DEPTH)]
    gate = [mod_me[l, None, 2 * D:3 * D] for l in range(DEPTH)]

    w_in = _cols_from(_all_gather(a_in_w[0].astype(BF16), "ag_a_in"))
    w_in = jnp.pad(w_in, ((0, 0), (0, in_pad - in_dim)))
    conv_w = _all_gather(a_conv_w[0], "ag_conv_w")
    conv_w = jnp.transpose(conv_w, (1, 0, 2)).reshape(SSD_CONV_W, conv_dim)
    conv_b = _all_gather(a_conv_b, "ag_conv_b").reshape(1, conv_dim)
    norm_g = _all_gather(a_norm_g, "ag_norm_g").reshape(1, d_inner)

    def modulate(xin, l, name):
        fn = lambda xv, sc, sh: ([xv * (1.0 + sc) + sh], [])
        return _rowmap(fn, [(xin, 0, D)], [scale[l], shift[l]], [(D, BF16)], [], name=name)[0]

    def ln_fwd(xin, y, l, name):
        def fn(xv, yv, gt, g, b):
            u = DEEPNORM_ALPHA * xv + (1.0 + gt) * yv
            mu = jnp.mean(u, axis=1, keepdims=True)
            uc = u - mu
            var = jnp.mean(uc * uc, axis=1, keepdims=True)
            o = uc * lax.rsqrt(var + LN_EPS) * g + b
            return [o, o], []
        return _rowmap(fn, [(xin, 0, D), (y, 0, D)], [gate[l], ln_g[l:l + 1], ln_b[l:l + 1]],
                       [(D, F32), (D, BF16)], [], name=name)

    def ln_bwd(xin, y, dout, l, name):
        def fn(xv, yv, do, gt, g, b):
            u = DEEPNORM_ALPHA * xv + (1.0 + gt) * yv
            mu = jnp.mean(u, axis=1, keepdims=True)
            uc = u - mu
            var = jnp.mean(uc * uc, axis=1, keepdims=True)
            rs = lax.rsqrt(var + LN_EPS)
            xh = uc * rs
            dxh = do * g
            du = rs * (dxh - jnp.mean(dxh, axis=1, keepdims=True) - xh * jnp.mean(dxh * xh, axis=1, keepdims=True))
            return [DEEPNORM_ALPHA * du, (1.0 + gt) * du], [_csum(du * yv), _csum(do * xh), _csum(do)]
        return _rowmap(fn, [(xin, 0, D), (y, 0, D), (dout, 0, D)], [gate[l], ln_g[l:l + 1], ln_b[l:l + 1]],
                       [(D, F32), (D, BF16)], [(1, D)] * 3, name=name)

    def mod_bwd(xin, dh, dx_acc, l, name):
        def fn(xv, dhv, dxa, sc):
            return [dxa + dhv * (1.0 + sc)], [_csum(dhv * xv), _csum(dhv)]
        return _rowmap(fn, [(xin, 0, D), (dh, 0, D), (dx_acc, 0, D)], [scale[l]], [(D, F32)], [(1, D)] * 2, name=name)

    h0 = modulate(x2d, 0, "mod_h0")
    proj, g_aout = _matmul(h0, w_in, name="mm_a_in", tn=1152,
                           comm=_ag_comm(a_out_w[0].astype(BF16)))
    w_aout = g_aout.reshape(d_inner, D)
    xbc = _conv_fwd(proj, d_inner, conv_dim, conv_w, conv_b, name="conv_fwd")
    dt_raw = proj[:, d_inner + conv_dim:]
    padh = lambda a: jnp.pad(a, ((0, 0), (0, LANE - H)))
    bias_p, alog_p = padh(a_dt_bias), padh(a_A_log)
    dt_p, a_p = _ssd_prep(dt_raw, bias_p, alog_p, name="ssd_prep")
    dt_f = jnp.repeat(dt_p[:, :H], P, axis=1)
    a_f = jnp.repeat(a_p[:, :H], P, axis=1)
    a_c = jnp.transpose(a_p[:, :H].reshape(L, G, K), (1, 0, 2))
    a_r = jnp.transpose(a_c, (0, 2, 1))
    d_full = jnp.repeat(a_D.reshape(H), P)[None]
    ssd_in = (xbc, dt_f, a_f, a_c, a_r, d_full)
    y_ssd, states, g_kv = _ssd_fwd(*ssd_in, d_inner=d_inner, name="ssd_fwd", comm=_ag_comm(kv_w.astype(BF16)))
    w_kv = _cols_from(g_kv)

    gw = d_inner // G

    def gnorm_fn(yv, zv, g):
        yg = yv * _silu(zv)
        r = lax.rsqrt(jnp.mean(yg * yg, axis=1, keepdims=True) + RMS_EPS)
        return [yg * r * g], []
    yn = _rowmap(gnorm_fn, [(y_ssd, 0, d_inner), (proj, 0, d_inner)], [norm_g], [(d_inner, BF16)], [],
                 name="gnorm_fwd", cw=gw, tr=1024)[0]
    ya, g_bin = _matmul(yn, w_aout, name="mm_a_out", comm=_ag_comm(b_in_w[0].astype(BF16)))
    w_bin = _cols_from(g_bin)
    x1, x1b = ln_fwd(x2d, ya, 0, "ln0_fwd")

    kv, g_bout = _matmul(x1b, w_kv, name="mm_kv", out_dtype=BF16,
                         comm=_ag_comm(b_out_w[0].astype(BF16)))
    w_bout = _cols_from(g_bout)
    h1 = modulate(x1, 1, "mod_h1")
    q = _matmul(h1, w_bin[:, :DIL_N_GROUPS * DIL_W], name="mm_b_q", out_dtype=BF16)
    z1 = _matmul(h1, w_bin[:, DIL_N_GROUPS * DIL_W:], name="mm_b_z")
    os_, lse_c = [], []
    for g in range(DIL_N_GROUPS):
        o, lse = _attn_fwd(q, kv, g, name=f"attn_fwd{g}")
        os_.append(o)
        lse_c.append(lse)
    lses = [jnp.transpose(l, (1, 0, 2)).reshape(L, DIL_HEADS) for l in lse_c]
    og = _merge_fwd(os_, lses, z1, 0, name="merge_fwd")
    yb = _matmul(og, w_bout, name="mm_b_out")
    x2, _ = ln_fwd(x1, yb, 1, "ln1_fwd")

    def loss_fn(xv, tv):
        e = xv - tv
        return [e * (1.0 / D)], [_csum(e * e) * (0.5 / D)]
    dx2, loss_cols = _rowmap(loss_fn, [(x2, 0, D), (tgt, 0, D)], [], [(D, F32)], [(1, D)], name="loss")
    loss = lax.psum(jnp.sum(loss_cols), ("x", "y", "c"))

    dx1a, dyb, dgate1, dlng1, dlnb1 = ln_bwd(x1, yb, dx2, 1, "ln1_bwd")
    dw_bout = _matmul(og, dyb, name="mm_b_out_dw", ta=True, out_dtype=BF16)
    dog = _matmul(dyb, w_bout, name="mm_b_out_dx", tb=True, out_dtype=BF16)
    do0, do1, do2, dl0, dl1, dl2, dz1, r_bout = _merge_bwd(os_, lses, z1, 0, dog, name="merge_bwd",
                                                           comm=_a2a_comm(_col_parts(dw_bout)))
    dqs, dks, dvs = [], [], []
    for g, (do_g, dl_g) in enumerate(zip((do0, do1, do2), (dl0, dl1, dl2))):
        d = DIL_PATTERNS[g][1]
        dl_c = jnp.transpose(dl_g.reshape(L // d, d, DIL_HEADS), (1, 0, 2))
        dl_t = jnp.transpose(dl_c, (0, 2, 1))
        lse_t = jnp.transpose(lse_c[g], (0, 2, 1))
        dqs.append(_attn_bwd_q(q, kv, do_g, lse_c[g], dl_c, g, name=f"attn_bwd_q{g}"))
        dk, dv = _attn_bwd_kv(q, kv, do_g, lse_t, dl_t, g, name=f"attn_bwd_kv{g}")
        dks.append(dk)
        dvs.append(dv)
    dqz = jnp.concatenate(dqs + [dz1], axis=1)
    dkv = jnp.concatenate(dks + dvs, axis=1)
    dw_bin = _matmul(h1, dqz, name="mm_b_in_dw", ta=True, out_dtype=BF16)
    dh1 = _matmul(dqz, w_bin, name="mm_b_in_dx", tb=True)
    dx1b, dscale1, dshift1 = mod_bwd(x1, dh1, dx1a, 1, "mod1_bwd")
    dw_kv = _matmul(x1b, dkv, name="mm_kv_dw", ta=True, out_dtype=BF16)
    dx1kv = _matmul(dkv, w_kv, name="mm_kv_dx", tb=True)
    dx1 = _rowmap(lambda a, b: ([a + b], []), [(dx1b, 0, D), (dx1kv, 0, D)], [], [(D, F32)], [], name="add_dx1")[0]

    dxa, dya, dgate0, dlng0, dlnb0 = ln_bwd(x2d, ya, dx1, 0, "ln0_bwd")
    dw_aout = _matmul(yn, dya, name="mm_a_out_dw", ta=True, out_dtype=BF16)
    dyn = _matmul(dya, w_aout, name="mm_a_out_dx", tb=True)

    def gnorm_bwd_fn(yv, zv, dn, g):
        sz = _silu(zv)
        yg = yv * sz
        r = lax.rsqrt(jnp.mean(yg * yg, axis=1, keepdims=True) + RMS_EPS)
        nrm = yg * r
        dnn = dn * g
        dyg = r * (dnn - nrm * jnp.mean(dnn * nrm, axis=1, keepdims=True))
        return [dyg * sz, dyg * yv * _dsilu(zv)], [_csum(dn * nrm)]
    dy_ssd, dz, dnorm_g = _rowmap(gnorm_bwd_fn, [(y_ssd, 0, d_inner), (proj, 0, d_inner), (dyn, 0, d_inner)],
                                  [norm_g], [(d_inner, F32), (d_inner, BF16)], [(1, d_inner)], name="gnorm_bwd", cw=gw,
                                  tr=1024)
    dxs, dB, dC, da_t, s1_t, dD_f, r_kv, r_aout = _ssd_bwd(
        *ssd_in, states, dy_ssd, d_inner=d_inner, name="ssd_bwd",
        comm=_comm_join([_a2a_comm(_col_parts(dw_kv)), _a2a_comm(dw_aout.reshape(N_DEV, d_inner // N_DEV, D))]))
    dxbc = jnp.concatenate([dxs, dB, dC], axis=1)
    dxbc_raw, dconv_w, dconv_b, r_bin = _conv_bwd(proj, d_inner, conv_dim, conv_w, conv_b, dxbc, name="conv_bwd",
                                                  comm=_a2a_comm(_col_parts(dw_bin)))
    tokp = lambda t: padh(jnp.transpose(t, (1, 0, 2)).reshape(L, H))
    ddt_raw, ddt_bias_p, dA_log_p = _ssd_post(tokp(da_t), tokp(s1_t), dt_p, dt_raw, bias_p, alog_p, name="ssd_post")
    ddt_bias, dA_log = ddt_bias_p[:, :H], dA_log_p[:, :H]
    dD = jnp.sum(dD_f.reshape(H, P), axis=1)[None]
    dproj = jnp.concatenate([dz, dxbc_raw, ddt_raw], axis=1)
    dw_in = _matmul(h0, dproj, name="mm_a_in_dw", ta=True, out_dtype=BF16, tn=1152)[:, :in_dim]
    dh0, r_in = _matmul(dproj, w_in, name="mm_a_in_dx", tb=True, tk=1152,
                        comm=_a2a_comm(_col_parts(dw_in)))
    grad_x, dscale0, dshift0 = mod_bwd(x2d, dh0, dxa, 0, "mod0_bwd")

    dmod = jnp.concatenate([dshift0, dscale0, dgate0, dshift1, dscale1, dgate1], axis=1)
    pieces = [dmod, dlng0, dlng1, dlnb0, dlnb1, ddt_bias, dA_log, dD,
              dconv_w.reshape(1, -1), dconv_b, dnorm_g]
    sizes = [p.shape[1] for p in pieces]
    tot = sum(sizes)
    tot_pad = -(-tot // (8 * LANE)) * (8 * LANE)
    packed = jnp.pad(jnp.concatenate(pieces, axis=1), ((0, 0), (0, tot_pad - tot))).reshape(tot_pad // LANE, LANE)
    packed_all = _all_gather(packed, "ag_small")
    small = _sum_parts(packed_all, name="sum_small").reshape(tot_pad)
    offs = np.cumsum([0] + sizes)
    seg = lambda i: small[int(offs[i]):int(offs[i + 1])]
    g_ada_b = seg(0).reshape(DEPTH, 3 * D)
    g_ln_g = jnp.stack([seg(1), seg(2)])
    g_ln_b = jnp.stack([seg(3), seg(4)])
    g_dt_bias, g_A_log, g_D = seg(5)[None], seg(6)[None], seg(7)[None]
    cs = conv_dim // N_DEV
    g_conv_w = lax.dynamic_slice(seg(8).reshape(SSD_CONV_W, conv_dim), (0, me * cs), (SSD_CONV_W, cs))[None]
    g_conv_b = lax.dynamic_slice(seg(9), (me * cs,), (cs,))[None]
    ns = d_inner // N_DEV
    g_norm_g = lax.dynamic_slice(seg(10), (me * ns,), (ns,))[None]

    ms = 3 * D // N_DEV
    dmod_all = packed_all.reshape(N_DEV, tot_pad)[:, :DEPTH * 3 * D].reshape(N_DEV, DEPTH, 3 * D)
    dmod_cols = lax.dynamic_slice(dmod_all, (0, 0, me * ms), (N_DEV, DEPTH, ms))
    c_t = jnp.transpose(c_all)
    g_ada_w = jnp.stack([_matmul(c_t, dmod_cols[:, l], name=f"mm_ada_dw{l}", exact=True, a_silu=True)
                         for l in range(DEPTH)])[None]

    def upd(parts, w, m, v, name):
        shp = w.shape
        r2 = lambda a: a.reshape(-1, shp[-1])
        return [o.reshape(shp) for o in _adamw(parts, r2(w), r2(m), r2(v), name=name)]

    res = {}
    res["ada_w"] = upd(g_ada_w.reshape(1, -1, ms), ada_w, m_ada_w, v_ada_w, "adam_ada_w")
    res["a_in_w"] = upd(r_in, a_in_w, m_a_in_w, v_a_in_w, "adam_a_in")
    res["a_out_w"] = upd(r_aout, a_out_w, m_a_out_w, v_a_out_w, "adam_a_out")
    res["kv_w"] = upd(r_kv, kv_w, m_kv_w, v_kv_w, "adam_kv")
    res["b_in_w"] = upd(r_bin, b_in_w, m_b_in_w, v_b_in_w, "adam_b_in")
    res["b_out_w"] = upd(r_bout, b_out_w, m_b_out_w, v_b_out_w, "adam_b_out")

    small_names = ["ada_b", "ln_g", "ln_b", "a_conv_w", "a_conv_b", "a_dt_bias", "a_A_log", "a_D", "a_norm_g"]
    small_g = [g_ada_b, g_ln_g, g_ln_b, g_conv_w, g_conv_b, g_dt_bias, g_A_log, g_D, g_norm_g]
    small_w = [ada_b, ln_g, ln_b, a_conv_w, a_conv_b, a_dt_bias, a_A_log, a_D, a_norm_g]
    small_m = [m_ada_b, m_ln_g, m_ln_b, m_a_conv_w, m_a_conv_b, m_a_dt_bias, m_a_A_log, m_a_D, m_a_norm_g]
    small_v = [v_ada_b, v_ln_g, v_ln_b, v_a_conv_w, v_a_conv_b, v_a_dt_bias, v_a_A_log, v_a_D, v_a_norm_g]
    ssz = [int(np.prod(w.shape)) for w in small_w]
    stot = sum(ssz)
    spad = -(-stot // (8 * LANE)) * (8 * LANE)

    def pack(arrs, fill):
        flat = jnp.concatenate([a.reshape(-1) for a in arrs])
        return jnp.concatenate([flat, jnp.full((spad - stot,), fill, F32)]).reshape(spad // LANE, LANE)

    sres = _adamw(pack(small_g, 0.0)[None], pack(small_w, 0.0), pack(small_m, 0.0), pack(small_v, 1.0), name="adam_small")
    soffs = np.cumsum([0] + ssz)
    for i, nme in enumerate(small_names):
        res[nme] = [r.reshape(-1)[int(soffs[i]):int(soffs[i + 1])].reshape(small_w[i].shape) for r in sres]

    order = ["ada_w", "ada_b", "ln_g", "ln_b", "a_in_w", "a_conv_w", "a_conv_b", "a_dt_bias", "a_A_log", "a_D",
             "a_norm_g", "a_out_w", "kv_w", "b_in_w", "b_out_w"]
    outs = [loss, grad_x[None]]
    for j in range(4):
        outs += [res[nme][j] for nme in order]
    return tuple(outs)
```

```python
import functools
import math

import numpy as np
import jax
import jax.numpy as jnp
from jax import lax
from jax.experimental import pallas as pl
from jax.experimental.pallas import tpu as pltpu

F32, BF16 = jnp.float32, jnp.bfloat16
HI = lax.Precision.HIGHEST
MESH = pl.DeviceIdType.MESH
N_DEV = 8

SSD_HEAD_DIM = 64
SSD_N_GROUPS = 8
SSD_D_STATE = 128
SSD_CONV_W = 4
SSD_CHUNK = 256
DIL_PATTERNS = ((128, 1), (512, 4), (2048, 16))
DIL_N_GROUPS = 3
DIL_HEADS = 8
DIL_HEAD_DIM = 128
DIL_BLOCK = 128
DIL_W = DIL_HEADS * DIL_HEAD_DIM
DEPTH = 2
DEEPNORM_ALPHA = (2 * DEPTH) ** 0.25
LN_EPS = 1e-5
RMS_EPS = 1e-5
ADAM_LR, ADAM_B1, ADAM_B2, ADAM_EPS, ADAM_WD, ADAM_STEP = 0.001, 0.9, 0.999, 1e-08, 0.01, 10
LANE = 128
NEG = -1e30
VMEM_LIMIT = 56 * 1024 * 1024


def _cp(sem=None):
    return pltpu.CompilerParams(dimension_semantics=sem, vmem_limit_bytes=VMEM_LIMIT)


def _silu(x):
    return x * jax.nn.sigmoid(x)


def _dsilu(x):
    s = jax.nn.sigmoid(x)
    return s * (1.0 + x * (1.0 - s))


def _softplus(x):
    return jnp.maximum(x, 0.0) + jnp.log(1.0 + jnp.exp(-jnp.abs(x)))


def _nt(a, b):
    return lax.dot_general(a, b, (((1,), (1,)), ((), ())), preferred_element_type=F32)


def _nn(a, b):
    return jnp.dot(a, b, preferred_element_type=F32)


def _hi(a, b):
    return jnp.dot(a, b, preferred_element_type=F32, precision=HI)


def _pick(n, pref, align=LANE):
    if n <= pref:
        return n
    for t in range(pref - pref % align, 0, -align):
        if n % t == 0:
            return t
    return n


class _Comm:
    def __init__(self, ins, outs, sems, start, finish):
        self.ins, self.outs, self.sems, self.start, self.finish = ins, outs, sems, start, finish


def _comm_join(comms):
    ins = [a for c in comms for a in c.ins]
    outs = [a for c in comms for a in c.outs]
    sems = [a for c in comms for a in c.sems]

    def split(refs, attr):
        res, i = [], 0
        for c in comms:
            n = len(getattr(c, attr))
            res.append(refs[i:i + n])
            i += n
        return res

    def start(cin, cout, csem):
        for c, a, b, d in zip(comms, split(cin, "ins"), split(cout, "outs"), split(csem, "sems")):
            c.start(a, b, d)

    def finish(cin, cout, csem):
        for c, a, b, d in zip(comms, split(cin, "ins"), split(cout, "outs"), split(csem, "sems")):
            c.finish(a, b, d)

    return _Comm(ins, outs, sems, start, finish)


def _ag_comm(v, cols=False):
    if cols:
        R, Cs = v.shape
        assert Cs % LANE == 0
        out_sd = jax.ShapeDtypeStruct((R, N_DEV * Cs), v.dtype)
    else:
        out_sd = jax.ShapeDtypeStruct((N_DEV,) + v.shape, v.dtype)

    def parts(x_ref, out_ref, send_sems, recv_sems, local_sem):
        x, y, c = lax.axis_index("x"), lax.axis_index("y"), lax.axis_index("c")
        me, sibling = (x, y, c), (x, y, 1 - c)
        chips = [(1 - x, y), (x, 1 - y), (1 - x, 1 - y)]

        def slab(px, py, pc):
            k = 4 * px + 2 * py + pc
            if cols:
                return out_ref.at[:, pl.ds(pl.multiple_of(k * Cs, LANE), Cs)]
            return out_ref.at[k]

        def copy(k, block, to, src=None):
            return pltpu.make_async_remote_copy(
                src_ref=slab(*block) if src is None else src, dst_ref=slab(*block),
                send_sem=send_sems.at[k], recv_sem=recv_sems.at[k], device_id=to, device_id_type=MESH)

        mine = pltpu.make_async_copy(x_ref, slab(*me), local_sem)
        first = [copy(0, me, sibling, src=x_ref)]
        first += [copy(1 + j, me, (*chip, c), src=x_ref) for j, chip in enumerate(chips)]
        passed = [copy(4 + j, (*chip, c), sibling) for j, chip in enumerate(chips)]
        return me, sibling, chips, c, copy, mine, first, passed

    def start(cin, cout, csem):
        _, _, _, _, _, mine, first, _ = parts(cin[0], cout[0], *csem)
        mine.start()
        for cp in first:
            cp.start()

    def finish(cin, cout, csem):
        me, sibling, chips, c, copy, mine, first, passed = parts(cin[0], cout[0], *csem)
        for j, chip in enumerate(chips):
            copy(1 + j, (*chip, c), me).wait_recv()
            passed[j].start()
        copy(0, sibling, me).wait_recv()
        for j, chip in enumerate(chips):
            copy(4 + j, (*chip, 1 - c), me).wait_recv()
        for cp in first + passed:
            cp.wait_send()
        mine.wait()

    return _Comm([v], [out_sd],
                 [pltpu.SemaphoreType.DMA((7,)), pltpu.SemaphoreType.DMA((7,)), pltpu.SemaphoreType.DMA], start, finish)


def _a2a_comm(v, cols=False):
    if cols:
        R, C = v.shape
        Cs = C // N_DEV
        assert Cs % LANE == 0
        out_sd = jax.ShapeDtypeStruct((N_DEV, R, Cs), v.dtype)
    else:
        out_sd = jax.ShapeDtypeStruct(v.shape, v.dtype)

    def parts(x_ref, out_ref, send_sems, recv_sems, local_sem):
        x, y, c = lax.axis_index("x"), lax.axis_index("y"), lax.axis_index("c")
        me = 4 * x + 2 * y + c

        def src(k):
            if cols:
                return x_ref.at[:, pl.ds(pl.multiple_of(k * Cs, LANE), Cs)]
            return x_ref.at[k]

        mine = pltpu.make_async_copy(src(me), out_ref.at[me], local_sem)
        sends, recvs = [], []
        for k, mask in enumerate(range(1, N_DEV)):
            px = 1 - x if (mask >> 2) & 1 else x
            py = 1 - y if (mask >> 1) & 1 else y
            pc = 1 - c if mask & 1 else c
            peer = 4 * px + 2 * py + pc
            sends.append(pltpu.make_async_remote_copy(
                src_ref=src(peer), dst_ref=out_ref.at[me],
                send_sem=send_sems.at[k], recv_sem=recv_sems.at[k], device_id=(px, py, pc), device_id_type=MESH))
            recvs.append(pltpu.make_async_remote_copy(
                src_ref=src(me), dst_ref=out_ref.at[peer],
                send_sem=send_sems.at[k], recv_sem=recv_sems.at[k], device_id=(px, py, pc), device_id_type=MESH))
        return mine, sends, recvs

    def start(cin, cout, csem):
        mine, sends, _ = parts(cin[0], cout[0], *csem)
        mine.start()
        for cp in sends:
            cp.start()

    def finish(cin, cout, csem):
        mine, sends, recvs = parts(cin[0], cout[0], *csem)
        for cp in recvs:
            cp.wait_recv()
        for cp in sends:
            cp.wait_send()
        mine.wait()

    return _Comm([v], [out_sd],
                 [pltpu.SemaphoreType.DMA((7,)), pltpu.SemaphoreType.DMA((7,)), pltpu.SemaphoreType.DMA], start, finish)


def _pair_comm(v4):
    def copy(x_ref, out_ref, send_sem, recv_sem):
        x, y, c = lax.axis_index("x"), lax.axis_index("y"), lax.axis_index("c")
        return pltpu.make_async_remote_copy(src_ref=x_ref, dst_ref=out_ref, send_sem=send_sem, recv_sem=recv_sem,
                                            device_id=(x, y, 1 - c), device_id_type=MESH)

    def start(cin, cout, csem):
        copy(cin[0], cout[0], *csem).start()

    def finish(cin, cout, csem):
        copy(cin[0], cout[0], *csem).wait()

    return _Comm([v4], [jax.ShapeDtypeStruct(v4.shape, v4.dtype)],
                 [pltpu.SemaphoreType.DMA, pltpu.SemaphoreType.DMA], start, finish)


def _quad_comm(v4):
    def parts(x_ref, out_ref, send_sems, recv_sems, local_sem):
        x, y, c = lax.axis_index("x"), lax.axis_index("y"), lax.axis_index("c")
        me = 2 * x + y
        mine = pltpu.make_async_copy(x_ref.at[me], out_ref.at[me], local_sem)
        sends, recvs = [], []
        for k, mask in enumerate(range(1, 4)):
            px = 1 - x if (mask >> 1) & 1 else x
            py = 1 - y if mask & 1 else y
            peer = 2 * px + py
            sends.append(pltpu.make_async_remote_copy(
                src_ref=x_ref.at[peer], dst_ref=out_ref.at[me],
                send_sem=send_sems.at[k], recv_sem=recv_sems.at[k], device_id=(px, py, c), device_id_type=MESH))
            recvs.append(pltpu.make_async_remote_copy(
                src_ref=x_ref.at[me], dst_ref=out_ref.at[peer],
                send_sem=send_sems.at[k], recv_sem=recv_sems.at[k], device_id=(px, py, c), device_id_type=MESH))
        return mine, sends, recvs

    def start(cin, cout, csem):
        mine, sends, _ = parts(cin[0], cout[0], *csem)
        mine.start()
        for cp in sends:
            cp.start()

    def finish(cin, cout, csem):
        mine, sends, recvs = parts(cin[0], cout[0], *csem)
        for cp in recvs:
            cp.wait_recv()
        for cp in sends:
            cp.wait_send()
        mine.wait()

    return _Comm([v4], [jax.ShapeDtypeStruct(v4.shape, v4.dtype)],
                 [pltpu.SemaphoreType.DMA((3,)), pltpu.SemaphoreType.DMA((3,)), pltpu.SemaphoreType.DMA], start, finish)


def _run_comm(comm, name):
    nci, nco = len(comm.ins), len(comm.outs)

    def body(*refs):
        comm.start(refs[:nci], refs[nci:nci + nco], refs[nci + nco:])
        comm.finish(refs[:nci], refs[nci:nci + nco], refs[nci + nco:])

    anyspec = pl.BlockSpec(memory_space=pl.ANY)
    return pl.pallas_call(body, name=name, out_shape=list(comm.outs), in_specs=[anyspec] * nci,
                          out_specs=[anyspec] * nco, scratch_shapes=list(comm.sems))(*comm.ins)


def _all_gather(v, name):
    return _run_comm(_ag_comm(v), name)[0]


def _pcall(body, args, *, name, grid, in_specs, out_specs, out_shape, scratch=(), sem=None, comm=None):
    out_shape, out_specs = list(out_shape), list(out_specs)
    if comm is None:
        return pl.pallas_call(body, name=name, grid=grid, in_specs=list(in_specs), out_specs=out_specs,
                              out_shape=out_shape, scratch_shapes=list(scratch), compiler_params=_cp(sem))(*args)
    ni, no, ns = len(args), len(out_shape), len(scratch)
    nci, nco = len(comm.ins), len(comm.outs)

    def wrapped(*refs):
        ins, cin = refs[:ni], refs[ni:ni + nci]
        o0 = ni + nci
        outs, cout = refs[o0:o0 + no], refs[o0 + no:o0 + no + nco]
        s0 = o0 + no + nco
        scr, csem = refs[s0:s0 + ns], refs[s0 + ns:]
        first = functools.reduce(jnp.logical_and, [pl.program_id(a) == 0 for a in range(len(grid))])
        last = functools.reduce(jnp.logical_and, [pl.program_id(a) == g - 1 for a, g in enumerate(grid)])

        @pl.when(first)
        def _():
            comm.start(cin, cout, csem)

        body(*ins, *outs, *scr)

        @pl.when(last)
        def _():
            comm.finish(cin, cout, csem)

    anyspec = pl.BlockSpec(memory_space=pl.ANY)
    res = pl.pallas_call(
        wrapped, name=name, grid=grid, in_specs=list(in_specs) + [anyspec] * nci,
        out_specs=out_specs + [anyspec] * nco, out_shape=out_shape + list(comm.outs),
        scratch_shapes=list(scratch) + list(comm.sems),
        compiler_params=_cp(("arbitrary",) * len(grid)))(*args, *comm.ins)
    return list(res[:no]) + list(res[no:])


def _matmul(a, b, *, name, ta=False, tb=False, out_dtype=F32, tm=1024, tn=1024, tk=2048,
            exact=False, a_silu=False, bias=None, comm=None):
    (K, M) = a.shape if ta else a.shape[::-1]
    (N, K2) = b.shape if tb else b.shape[::-1]
    assert K == K2, (a.shape, b.shape, ta, tb)
    tm, tn, tk = _pick(M, tm), _pick(N, tn), _pick(K, tk)
    nk = K // tk
    a_spec = pl.BlockSpec((tk, tm), lambda i, j, k: (k, i)) if ta else pl.BlockSpec((tm, tk), lambda i, j, k: (i, k))
    b_spec = pl.BlockSpec((tn, tk), lambda i, j, k: (j, k)) if tb else pl.BlockSpec((tk, tn), lambda i, j, k: (k, j))
    dims = (((0,) if ta else (1,), (1,) if tb else (0,)), ((), ()))
    in_specs, args = [a_spec, b_spec], [a, b]
    if bias is not None:
        in_specs.append(pl.BlockSpec((1, tn), lambda i, j, k: (0, j)))
        args.append(bias)

    def body(*refs):
        a_ref, b_ref = refs[0], refs[1]
        bias_ref = refs[2] if bias is not None else None
        o_ref = refs[2 + (bias is not None)]
        av, bv = a_ref[...], b_ref[...]
        if a_silu:
            av = _silu(av.astype(F32))
        if exact:
            p = lax.dot_general(av.astype(F32), bv.astype(F32), dims, preferred_element_type=F32, precision=HI)
        else:
            p = lax.dot_general(av.astype(BF16), bv.astype(BF16), dims, preferred_element_type=F32)

        def fin(r):
            if bias_ref is not None:
                r = r + bias_ref[...]
            o_ref[...] = r.astype(o_ref.dtype)

        if nk == 1:
            fin(p)
        else:
            acc = refs[-1]
            k = pl.program_id(2)

            @pl.when(k == 0)
            def _():
                acc[...] = p

            @pl.when(k > 0)
            def _():
                acc[...] += p

            @pl.when(k == nk - 1)
            def _():
                fin(acc[...])

    res = _pcall(
        body, args, name=name,
        out_shape=[jax.ShapeDtypeStruct((M, N), out_dtype)],
        grid=(M // tm, N // tn, nk),
        in_specs=in_specs,
        out_specs=[pl.BlockSpec((tm, tn), lambda i, j, k: (i, j))],
        scratch=[pltpu.VMEM((tm, tn), F32)] if nk > 1 else [],
        sem=("parallel", "parallel", "arbitrary"), comm=comm)
    return res[0] if comm is None else res


def _rowmap(fn, rows, bcasts, outs, accs, *, name, tr=256, cw=None, comm=None):
    L = rows[0][0].shape[0]
    tr = _pick(L, tr)
    nr, nb, no, na = len(rows), len(bcasts), len(outs), len(accs)
    if cw is None:
        ncol = 1
        widths = [w for (_, _, w) in rows]
    else:
        wtot = rows[0][2]
        ncol = wtot // cw
        widths = [cw] * nr
    in_specs, args = [], []
    for (arr, off, w), bw in zip(rows, widths):
        assert off % bw == 0
        in_specs.append(pl.BlockSpec((tr, bw), functools.partial(lambda j, i, o: (i, o + j), o=off // bw)))
        args.append(arr)
    for arr in bcasts:
        bw = arr.shape[1] if cw is None else cw
        in_specs.append(pl.BlockSpec((arr.shape[0], bw), lambda j, i: (0, j)))
        args.append(arr)
    out_shape, out_specs = [], []
    for (w, dt) in outs:
        bw = w if cw is None else cw
        out_shape.append(jax.ShapeDtypeStruct((L, w), dt))
        out_specs.append(pl.BlockSpec((tr, bw), lambda j, i: (i, j)))
    for (r, w) in accs:
        bw = w if cw is None else cw
        out_shape.append(jax.ShapeDtypeStruct((r, w), F32))
        out_specs.append(pl.BlockSpec((r, bw), lambda j, i: (0, j)))

    def body(*refs):
        ins = [r[...] for r in refs[:nr + nb]]
        o_refs = refs[nr + nb:nr + nb + no]
        a_refs = refs[nr + nb + no:]
        o, a = fn(*ins)
        for ref, val in zip(o_refs, o):
            ref[...] = val.astype(ref.dtype)
        if na:
            @pl.when(pl.program_id(1) == 0)
            def _():
                for ref in a_refs:
                    ref[...] = jnp.zeros_like(ref)

            for ref, val in zip(a_refs, a):
                ref[...] += val

    return _pcall(body, args, name=name, out_shape=out_shape, grid=(ncol, L // tr), in_specs=in_specs,
                  out_specs=out_specs, sem=("parallel", "arbitrary"), comm=comm)


def _csum(v):
    return jnp.sum(v, axis=0, keepdims=True)


def _shift_rows(v, s, rows):
    if s == 0:
        return v
    n = v.shape[0]
    r = pltpu.roll(v, s % n, 0)
    if s > 0:
        return jnp.where(rows >= s, r, 0.0)
    return jnp.where(rows < n + s, r, 0.0)


def _conv_fwd(proj, off, width, w, b, *, name, tc=256):
    L = proj.shape[0]
    tc = _pick(width, tc)

    def body(x_ref, w_ref, b_ref, o_ref):
        x = x_ref[...]
        rows = lax.broadcasted_iota(jnp.int32, x.shape, 0)
        acc = jnp.zeros_like(x) + b_ref[...]
        for k in range(SSD_CONV_W):
            acc = acc + w_ref[k:k + 1, :] * _shift_rows(x, SSD_CONV_W - 1 - k, rows)
        o_ref[...] = _silu(acc)

    return pl.pallas_call(
        body, name=name, out_shape=jax.ShapeDtypeStruct((L, width), F32), grid=(width // tc,),
        in_specs=[pl.BlockSpec((L, tc), functools.partial(lambda j, o: (0, o + j), o=off // tc)),
                  pl.BlockSpec((SSD_CONV_W, tc), lambda j: (0, j)), pl.BlockSpec((1, tc), lambda j: (0, j))],
        out_specs=pl.BlockSpec((L, tc), lambda j: (0, j)),
        compiler_params=_cp(("parallel",)),
    )(proj, w, b)


def _conv_bwd(proj, off, width, w, b, dys, *, name, tc=256, comm=None):
    L = proj.shape[0]
    tc = _pick(width, tc)
    ntile = [d.shape[1] // tc for d in dys]
    assert all(d.shape[1] % tc == 0 for d in dys) and sum(ntile) == width // tc
    first = [sum(ntile[:i]) for i in range(len(dys))]

    def body(x_ref, w_ref, b_ref, *rest):
        dy_refs, (dx_ref, dw_ref, db_ref) = rest[:len(dys)], rest[len(dys):]
        j = pl.program_id(0)
        dy = dy_refs[0][...]
        for i in range(1, len(dys)):
            dy = jnp.where(j >= first[i], dy_refs[i][...], dy)
        x = x_ref[...]
        rows = lax.broadcasted_iota(jnp.int32, x.shape, 0)
        xs = [_shift_rows(x, SSD_CONV_W - 1 - k, rows) for k in range(SSD_CONV_W)]
        pre = jnp.zeros_like(x) + b_ref[...]
        for k in range(SSD_CONV_W):
            pre = pre + w_ref[k:k + 1, :] * xs[k]
        dpre = dy * _dsilu(pre)
        dx = jnp.zeros_like(x)
        for k in range(SSD_CONV_W):
            dx = dx + w_ref[k:k + 1, :] * _shift_rows(dpre, -(SSD_CONV_W - 1 - k), rows)
            dw_ref[k:k + 1, :] = _csum(dpre * xs[k])
        dx_ref[...] = dx.astype(dx_ref.dtype)
        db_ref[...] = _csum(dpre)

    dy_specs = [pl.BlockSpec((L, tc), functools.partial(lambda j, f, n: (0, jnp.clip(j - f, 0, n - 1)), f=f, n=n))
                for f, n in zip(first, ntile)]
    return _pcall(
        body, (proj, w, b, *dys), name=name,
        out_shape=[jax.ShapeDtypeStruct((L, width), BF16), jax.ShapeDtypeStruct((SSD_CONV_W, width), F32),
                   jax.ShapeDtypeStruct((1, width), F32)],
        grid=(width // tc,),
        in_specs=[pl.BlockSpec((L, tc), functools.partial(lambda j, o: (0, o + j), o=off // tc)),
                  pl.BlockSpec((SSD_CONV_W, tc), lambda j: (0, j)), pl.BlockSpec((1, tc), lambda j: (0, j))] + dy_specs,
        out_specs=[pl.BlockSpec((L, tc), lambda j: (0, j)), pl.BlockSpec((SSD_CONV_W, tc), lambda j: (0, j)),
                   pl.BlockSpec((1, tc), lambda j: (0, j))],
        sem=("parallel",), comm=comm)


def _tri(Q):
    ri = lax.broadcasted_iota(jnp.int32, (Q, Q), 0)
    ci = lax.broadcasted_iota(jnp.int32, (Q, Q), 1)
    return ri >= ci, ri <= ci


def _ssd_prep(dt_raw, bias, alog, *, name):
    L, W = dt_raw.shape
    Q = SSD_CHUNK

    def body(r_ref, b_ref, al_ref, dt_ref, a_ref):
        lower, _ = _tri(Q)
        dt = _softplus(r_ref[...] + b_ref[...])
        dt_ref[...] = dt
        a_ref[...] = _hi(lower.astype(F32), dt * (-jnp.exp(al_ref[...])))

    blk = pl.BlockSpec((Q, W), lambda c: (c, 0))
    one = pl.BlockSpec((1, W), lambda c: (0, 0))
    sd = jax.ShapeDtypeStruct((L, W), F32)
    return _pcall(body, (dt_raw, bias, alog), name=name, out_shape=[sd, sd], grid=(L // Q,),
                  in_specs=[blk, one, one], out_specs=[blk, blk], sem=("parallel",))


def _ssd_post(da, s1, dt, dt_raw, bias, alog, *, name):
    L, W = da.shape
    Q = SSD_CHUNK

    def body(da_ref, s1_ref, dt_ref, r_ref, b_ref, al_ref, o_ref, db_ref, dal_ref):
        _, upper = _tri(Q)
        A = -jnp.exp(al_ref[...])
        ddtA = _hi(upper.astype(F32), da_ref[...])
        ddt_raw = (ddtA * A + s1_ref[...]) * jax.nn.sigmoid(r_ref[...] + b_ref[...])
        o_ref[...] = ddt_raw.astype(o_ref.dtype)

        @pl.when(pl.program_id(0) == 0)
        def _():
            db_ref[...] = jnp.zeros_like(db_ref)
            dal_ref[...] = jnp.zeros_like(dal_ref)

        db_ref[...] += _csum(ddt_raw)
        dal_ref[...] += _csum(ddtA * dt_ref[...]) * A

    blk = pl.BlockSpec((Q, W), lambda c: (c, 0))
    one = pl.BlockSpec((1, W), lambda c: (0, 0))
    return _pcall(body, (da, s1, dt, dt_raw, bias, alog), name=name,
                  out_shape=[jax.ShapeDtypeStruct((L, W), BF16), jax.ShapeDtypeStruct((1, W), F32),
                             jax.ShapeDtypeStruct((1, W), F32)],
                  grid=(L // Q,), in_specs=[blk, blk, blk, blk, one, one], out_specs=[blk, one, one],
                  sem=("arbitrary",))


def _head_sum(v, K, KP):
    P = KP // K
    t_r = lax.broadcasted_iota(jnp.int32, (KP, K), 0)
    t_c = lax.broadcasted_iota(jnp.int32, (KP, K), 1)
    Et = ((t_r >= t_c * P) & (t_r < (t_c + 1) * P)).astype(BF16)
    hi = v.astype(BF16)
    lo = (v - hi.astype(F32)).astype(BF16)
    return _nn(hi, Et) + _nn(lo, Et)


def _half_masks():
    li = lax.broadcasted_iota(jnp.int32, (1, LANE), 1)
    return [(li < SSD_HEAD_DIM).astype(F32), (li >= SSD_HEAD_DIM).astype(F32)]


def _ssd_specs(K, KP, d_inner, rev, nc):
    Q, N, G = SSD_CHUNK, SSD_D_STATE, SSD_N_GROUPS
    cidx = (lambda c: nc - 1 - c) if rev else (lambda c: c)
    b_off, c_off = d_inner // N, d_inner // N + G
    return [
        pl.BlockSpec((Q, KP), lambda g, c: (cidx(c), g)),
        pl.BlockSpec((Q, N), lambda g, c: (cidx(c), b_off + g)),
        pl.BlockSpec((Q, N), lambda g, c: (cidx(c), c_off + g)),
        pl.BlockSpec((None, Q, K), lambda g, c: (g, cidx(c), 0)),
        pl.BlockSpec((None, Q, K), lambda g, c: (g, cidx(c), 0)),
        pl.BlockSpec((None, K, Q), lambda g, c: (g, 0, cidx(c))),
        pl.BlockSpec((1, KP), lambda g, c: (0, g)),
    ]


def _expand_heads(vc, K):
    Q = vc.shape[0]
    left = lax.broadcasted_iota(jnp.int32, (Q, LANE), 1) < SSD_HEAD_DIM
    parts = []
    for pr in range(K // 2):
        a = jnp.broadcast_to(vc[:, 2 * pr:2 * pr + 1], (Q, LANE))
        b = jnp.broadcast_to(vc[:, 2 * pr + 1:2 * pr + 2], (Q, LANE))
        parts.append(jnp.where(left, a, b))
    return jnp.concatenate(parts, axis=1) if len(parts) > 1 else parts[0]


def _ssd_fwd(xbc, dt_c, a_c, a_r, d_full, *, d_inner, name, comm=None):
    L = xbc.shape[0]
    G, N, Q, P = SSD_N_GROUPS, SSD_D_STATE, SSD_CHUNK, SSD_HEAD_DIM
    KP = d_inner // G
    K = KP // P
    nc = L // Q
    npair = KP // LANE

    def body(xs_ref, b_ref, c_ref, dtc_ref, ac_ref, ar_ref, df_ref, y_ref, st_ref, S):
        @pl.when(pl.program_id(1) == 0)
        def _():
            S[...] = jnp.zeros_like(S)

        lower, _ = _tri(Q)
        st_ref[...] = S[...]
        xs = xs_ref[...]
        Bm, Cm = b_ref[...], c_ref[...]
        Bb, Cb = Bm.astype(BF16), Cm.astype(BF16)
        a_c, a_r = ac_ref[...], ar_ref[...]
        a_f = _expand_heads(a_c, K)
        X = xs * _expand_heads(dtc_ref[...], K)
        ea = jnp.exp(a_f)
        alast = a_f[Q - 1:Q, :]
        tail = jnp.exp(alast - a_f)
        cb = _nt(Cb, Bb)
        Sv = S[...]
        yoff = _nn(Cb, Sv.astype(BF16)) * ea
        skip = xs * df_ref[...]
        masks = _half_masks()
        for pr in range(npair):
            Xp = X[:, pr * LANE:(pr + 1) * LANE]
            acc = yoff[:, pr * LANE:(pr + 1) * LANE] + skip[:, pr * LANE:(pr + 1) * LANE]
            for hh in range(2):
                k = 2 * pr + hh
                seg = a_c[:, k:k + 1] - a_r[k:k + 1, :]
                dec = jnp.where(lower, jnp.exp(jnp.minimum(seg, 0.0)), 0.0)
                acc = acc + _nn((cb * dec).astype(BF16), (Xp * masks[hh]).astype(BF16))
            y_ref[:, pr * LANE:(pr + 1) * LANE] = acc
        Bt = Bm.T
        S[...] = Sv * jnp.exp(alast) + _nn(Bt.astype(BF16), (X * tail).astype(BF16))

    return _pcall(
        body, (xbc, xbc, xbc, dt_c, a_c, a_r, d_full), name=name,
        out_shape=[jax.ShapeDtypeStruct((L, d_inner), F32), jax.ShapeDtypeStruct((G, nc, N, KP), F32)],
        grid=(G, nc),
        in_specs=_ssd_specs(K, KP, d_inner, False, nc),
        out_specs=[pl.BlockSpec((Q, KP), lambda g, c: (c, g)), pl.BlockSpec((None, None, N, KP), lambda g, c: (g, c, 0, 0))],
        scratch=[pltpu.VMEM((N, KP), F32)],
        sem=("parallel", "arbitrary"), comm=comm)


def _ssd_bwd(xbc, dt_c, a_c, a_r, d_full, states, dy, *, d_inner, name, comm=None):
    L = xbc.shape[0]
    G, N, Q, P = SSD_N_GROUPS, SSD_D_STATE, SSD_CHUNK, SSD_HEAD_DIM
    KP = d_inner // G
    K = KP // P
    nc = L // Q
    npair = KP // LANE

    def body(xs_ref, b_ref, c_ref, dtc_ref, ac_ref, ar_ref, df_ref, st_ref, dy_ref,
             dxs_ref, db_ref, dc_ref, da_ref, s1_ref, dd_ref, dS):
        @pl.when(pl.program_id(1) == 0)
        def _():
            dS[...] = jnp.zeros_like(dS)
            dd_ref[...] = jnp.zeros_like(dd_ref)

        lower, upper = _tri(Q)
        a_c, a_r = ac_ref[...], ar_ref[...]
        a_f, dt_f = _expand_heads(a_c, K), _expand_heads(dtc_ref[...], K)
        xs = xs_ref[...]
        Bm, Cm = b_ref[...], c_ref[...]
        Bb, Cb = Bm.astype(BF16), Cm.astype(BF16)
        dY = dy_ref[...]
        X = xs * dt_f
        ea = jnp.exp(a_f)
        alast = a_f[Q - 1:Q, :]
        tail = jnp.exp(alast - a_f)
        el = jnp.exp(alast)
        Sv, dSn = st_ref[...], dS[...]
        Sb, dSb = Sv.astype(BF16), dSn.astype(BF16)
        cb = _nt(Cb, Bb)
        cbT = _nt(Bb, Cb)
        yoff_raw = _nn(Cb, Sb)
        dYe = dY * ea
        dC = _nt(dYe.astype(BF16), Sb)
        dS[...] = dSn * el + _nn(Cm.T.astype(BF16), dYe.astype(BF16))
        Gx = _nn(Bb, dSb)
        dB = _nt((X * tail).astype(BF16), dSb)
        dtl = Gx * X * tail
        da_f = dYe * yoff_raw - dtl
        dalast_f = _csum(dtl) + _csum(dSn * Sv) * el
        da_c = _head_sum(da_f, K, KP)
        onek = lax.broadcasted_iota(jnp.int32, (1, K), 1)
        masks = _half_masks()
        dcb = jnp.zeros((Q, Q), F32)
        dcbT = jnp.zeros((Q, Q), F32)
        dX_parts = []
        for pr in range(npair):
            Xp = X[:, pr * LANE:(pr + 1) * LANE]
            dYp = dY[:, pr * LANE:(pr + 1) * LANE]
            dXp = Gx[:, pr * LANE:(pr + 1) * LANE] * tail[:, pr * LANE:(pr + 1) * LANE]
            for hh in range(2):
                k = 2 * pr + hh
                Xk = (Xp * masks[hh]).astype(BF16)
                dYk = (dYp * masks[hh]).astype(BF16)
                seg = a_c[:, k:k + 1] - a_r[k:k + 1, :]
                dec = jnp.where(lower, jnp.exp(jnp.minimum(seg, 0.0)), 0.0)
                decT = jnp.where(upper, jnp.exp(jnp.minimum(-seg, 0.0)), 0.0)
                dM = _nt(dYk, Xk)
                dMT = _nt(Xk, dYk)
                MT = cbT * decT
                dcb = dcb + dM * dec
                dcbT = dcbT + dMT * decT
                da_k = jnp.sum(dM * cb * dec, axis=1, keepdims=True) - jnp.sum(dMT * MT, axis=1, keepdims=True)
                da_c = da_c + da_k * (onek == k).astype(F32)
                dXp = dXp + _nn(MT.astype(BF16), dYk)
            dX_parts.append(dXp)
        dX = jnp.concatenate(dX_parts, axis=1) if npair > 1 else dX_parts[0]
        dC = dC + _nn(dcb.astype(BF16), Bb)
        dB = dB + _nn(dcbT.astype(BF16), Cb)
        lastrow = (lax.broadcasted_iota(jnp.int32, (Q, 1), 0) == Q - 1).astype(F32)
        da_ref[...] = da_c + lastrow * _head_sum(dalast_f, K, KP)
        s1_ref[...] = _head_sum(dX * xs, K, KP)
        dd_ref[...] += _csum(dY * xs)
        dxs_ref[...] = dX * dt_f + dY * df_ref[...]
        db_ref[...] = dB
        dc_ref[...] = dC

    rc = lambda c: nc - 1 - c
    tok = jax.ShapeDtypeStruct((G, L, K), F32)
    tok_spec = pl.BlockSpec((None, Q, K), lambda g, c: (g, rc(c), 0))
    return _pcall(
        body, (xbc, xbc, xbc, dt_c, a_c, a_r, d_full, states, dy), name=name,
        out_shape=[jax.ShapeDtypeStruct((L, d_inner), F32), jax.ShapeDtypeStruct((L, G * N), F32),
                   jax.ShapeDtypeStruct((L, G * N), F32), tok, tok, jax.ShapeDtypeStruct((1, d_inner), F32)],
        grid=(G, nc),
        in_specs=_ssd_specs(K, KP, d_inner, True, nc) + [
            pl.BlockSpec((None, None, N, KP), lambda g, c: (g, rc(c), 0, 0)),
            pl.BlockSpec((Q, KP), lambda g, c: (rc(c), g))],
        out_specs=[pl.BlockSpec((Q, KP), lambda g, c: (rc(c), g)), pl.BlockSpec((Q, N), lambda g, c: (rc(c), g)),
                   pl.BlockSpec((Q, N), lambda g, c: (rc(c), g)), tok_spec, tok_spec,
                   pl.BlockSpec((1, KP), lambda g, c: (0, g))],
        scratch=[pltpu.VMEM((N, KP), F32)],
        sem=("parallel", "arbitrary"), comm=comm)


def _slopes():
    n = DIL_N_GROUPS * DIL_HEADS
    s = 2.0 ** (-8.0 * np.arange(1, n + 1) / n)
    return s.reshape(DIL_N_GROUPS, DIL_HEADS).astype(np.float32)


def _attn_scores(qh, kh, slope_d, cur, valid_blk, transposed):
    B = DIL_BLOCK
    scale = DIL_HEAD_DIM ** -0.5
    if transposed:
        s = _nt(kh, qh) * scale
        kj = lax.broadcasted_iota(jnp.int32, (B, B), 0)
        qi = lax.broadcasted_iota(jnp.int32, (B, B), 1)
    else:
        s = _nt(qh, kh) * scale
        qi = lax.broadcasted_iota(jnp.int32, (B, B), 0)
        kj = lax.broadcasted_iota(jnp.int32, (B, B), 1)
    if cur:
        delta = qi - kj
        ok = kj <= qi
    else:
        delta = qi + B - kj
        ok = (kj >= qi) & valid_blk
    return jnp.where(ok, s - slope_d * delta.astype(F32), NEG)


def _attn_fwd(q, kv, g, *, name):
    L = q.shape[0]
    window, d = DIL_PATTERNS[g]
    assert window // d == DIL_BLOCK and L % (d * DIL_BLOCK) == 0
    M, B, W = L // d, DIL_BLOCK, DIL_W
    nb = M // B
    slopes = _slopes()[g]
    qv = q.reshape(M, d * DIL_N_GROUPS * W)
    kvv = kv.reshape(M, d * 2 * DIL_N_GROUPS * W)
    prev = lambda m: jnp.maximum(m - 1, 0)

    def body(q_ref, kc_ref, kp_ref, vc_ref, vp_ref, o_ref, lse_ref):
        has_prev = pl.program_id(1) > 0
        lse = jnp.zeros((B, DIL_HEADS), F32)
        onek = lax.broadcasted_iota(jnp.int32, (1, DIL_HEADS), 1)
        for h in range(DIL_HEADS):
            sl = slice(h * DIL_HEAD_DIM, (h + 1) * DIL_HEAD_DIM)
            qh = q_ref[:, sl]
            sd = float(slopes[h]) * d
            sc = _attn_scores(qh, kc_ref[:, sl], sd, True, None, False)
            sp = _attn_scores(qh, kp_ref[:, sl], sd, False, has_prev, False)
            m = jnp.maximum(jnp.max(sc, axis=1, keepdims=True), jnp.max(sp, axis=1, keepdims=True))
            pc, pp = jnp.exp(sc - m), jnp.exp(sp - m)
            den = jnp.sum(pc, axis=1, keepdims=True) + jnp.sum(pp, axis=1, keepdims=True)
            o = _nn(pc.astype(BF16), vc_ref[:, sl]) + _nn(pp.astype(BF16), vp_ref[:, sl])
            o_ref[:, sl] = o / den
            lse = lse + (m + jnp.log(den)) * (onek == h).astype(F32)
        lse_ref[...] = lse

    o, lse = pl.pallas_call(
        body, name=name,
        out_shape=[jax.ShapeDtypeStruct((M, d * W), F32), jax.ShapeDtypeStruct((d, M, DIL_HEADS), F32)],
        grid=(d, nb),
        in_specs=[pl.BlockSpec((B, W), lambda r, m: (m, r * 3 + g)),
                  pl.BlockSpec((B, W), lambda r, m: (m, r * 6 + g)),
                  pl.BlockSpec((B, W), lambda r, m: (prev(m), r * 6 + g)),
                  pl.BlockSpec((B, W), lambda r, m: (m, r * 6 + 3 + g)),
                  pl.BlockSpec((B, W), lambda r, m: (prev(m), r * 6 + 3 + g))],
        out_specs=[pl.BlockSpec((B, W), lambda r, m: (m, r)),
                   pl.BlockSpec((None, B, DIL_HEADS), lambda r, m: (r, m, 0))],
        compiler_params=_cp(("parallel", "parallel")),
    )(qv, kvv, kvv, kvv, kvv)
    return o.reshape(L, W), lse


def _attn_bwd_q(q, kv, do, lse, dl, g, *, name):
    L = q.shape[0]
    _, d = DIL_PATTERNS[g]
    M, B, W = L // d, DIL_BLOCK, DIL_W
    nb = M // B
    slopes = _slopes()[g]
    scale = DIL_HEAD_DIM ** -0.5
    qv = q.reshape(M, d * DIL_N_GROUPS * W)
    kvv = kv.reshape(M, d * 2 * DIL_N_GROUPS * W)
    dov = do.reshape(M, d * W)
    prev = lambda m: jnp.maximum(m - 1, 0)

    def body(q_ref, kc_ref, kp_ref, vc_ref, vp_ref, do_ref, lse_ref, dl_ref, dq_ref):
        has_prev = pl.program_id(1) > 0
        lse_all, dl_all = lse_ref[...], dl_ref[...]
        for h in range(DIL_HEADS):
            sl = slice(h * DIL_HEAD_DIM, (h + 1) * DIL_HEAD_DIM)
            qh, doh = q_ref[:, sl], do_ref[:, sl]
            sd = float(slopes[h]) * d
            lse_h, dl_h = lse_all[:, h:h + 1], dl_all[:, h:h + 1]
            pc = jnp.exp(_attn_scores(qh, kc_ref[:, sl], sd, True, None, False) - lse_h)
            pp = jnp.exp(_attn_scores(qh, kp_ref[:, sl], sd, False, has_prev, False) - lse_h)
            dsc = pc * (_nt(doh, vc_ref[:, sl]) - dl_h)
            dsp = pp * (_nt(doh, vp_ref[:, sl]) - dl_h)
            dq = _nn(dsc.astype(BF16), kc_ref[:, sl]) + _nn(dsp.astype(BF16), kp_ref[:, sl])
            dq_ref[:, sl] = (dq * scale).astype(dq_ref.dtype)

    col = pl.BlockSpec((None, B, DIL_HEADS), lambda r, m: (r, m, 0))
    dq = pl.pallas_call(
        body, name=name,
        out_shape=jax.ShapeDtypeStruct((M, d * W), BF16),
        grid=(d, nb),
        in_specs=[pl.BlockSpec((B, W), lambda r, m: (m, r * 3 + g)),
                  pl.BlockSpec((B, W), lambda r, m: (m, r * 6 + g)),
                  pl.BlockSpec((B, W), lambda r, m: (prev(m), r * 6 + g)),
                  pl.BlockSpec((B, W), lambda r, m: (m, r * 6 + 3 + g)),
                  pl.BlockSpec((B, W), lambda r, m: (prev(m), r * 6 + 3 + g)),
                  pl.BlockSpec((B, W), lambda r, m: (m, r)), col, col],
        out_specs=pl.BlockSpec((B, W), lambda r, m: (m, r)),
        compiler_params=_cp(("parallel", "parallel")),
    )(qv, kvv, kvv, kvv, kvv, dov, lse, dl)
    return dq.reshape(L, W)


def _attn_bwd_kv(q, kv, do, lse_t, dl_t, g, *, name):
    L = q.shape[0]
    _, d = DIL_PATTERNS[g]
    M, B, W = L // d, DIL_BLOCK, DIL_W
    nb = M // B
    slopes = _slopes()[g]
    scale = DIL_HEAD_DIM ** -0.5
    qv = q.reshape(M, d * DIL_N_GROUPS * W)
    kvv = kv.reshape(M, d * 2 * DIL_N_GROUPS * W)
    dov = do.reshape(M, d * W)
    nxt = lambda m: jnp.minimum(m + 1, nb - 1)

    def body(k_ref, v_ref, qc_ref, qn_ref, doc_ref, don_ref, lsec_ref, lsen_ref, dlc_ref, dln_ref, dk_ref, dv_ref):
        has_next = pl.program_id(1) < nb - 1
        lsec, lsen, dlc, dln = lsec_ref[...], lsen_ref[...], dlc_ref[...], dln_ref[...]
        for h in range(DIL_HEADS):
            sl = slice(h * DIL_HEAD_DIM, (h + 1) * DIL_HEAD_DIM)
            kh, vh = k_ref[:, sl], v_ref[:, sl]
            sd = float(slopes[h]) * d
            ptc = jnp.exp(_attn_scores(qc_ref[:, sl], kh, sd, True, None, True) - lsec[h:h + 1, :])
            ptn = jnp.exp(_attn_scores(qn_ref[:, sl], kh, sd, False, has_next, True) - lsen[h:h + 1, :])
            dv = _nn(ptc.astype(BF16), doc_ref[:, sl]) + _nn(ptn.astype(BF16), don_ref[:, sl])
            dstc = ptc * (_nt(vh, doc_ref[:, sl]) - dlc[h:h + 1, :])
            dstn = ptn * (_nt(vh, don_ref[:, sl]) - dln[h:h + 1, :])
            dk = _nn(dstc.astype(BF16), qc_ref[:, sl]) + _nn(dstn.astype(BF16), qn_ref[:, sl])
            dk_ref[:, sl] = (dk * scale).astype(dk_ref.dtype)
            dv_ref[:, sl] = dv.astype(dv_ref.dtype)

    rowc = pl.BlockSpec((None, DIL_HEADS, B), lambda r, m: (r, 0, m))
    rown = pl.BlockSpec((None, DIL_HEADS, B), lambda r, m: (r, 0, nxt(m)))
    dk, dv = pl.pallas_call(
        body, name=name,
        out_shape=[jax.ShapeDtypeStruct((M, d * W), BF16), jax.ShapeDtypeStruct((M, d * W), BF16)],
        grid=(d, nb),
        in_specs=[pl.BlockSpec((B, W), lambda r, m: (m, r * 6 + g)),
                  pl.BlockSpec((B, W), lambda r, m: (m, r * 6 + 3 + g)),
                  pl.BlockSpec((B, W), lambda r, m: (m, r * 3 + g)),
                  pl.BlockSpec((B, W), lambda r, m: (nxt(m), r * 3 + g)),
                  pl.BlockSpec((B, W), lambda r, m: (m, r)),
                  pl.BlockSpec((B, W), lambda r, m: (nxt(m), r)),
                  rowc, rown, rowc, rown],
        out_specs=[pl.BlockSpec((B, W), lambda r, m: (m, r)), pl.BlockSpec((B, W), lambda r, m: (m, r))],
        compiler_params=_cp(("parallel", "parallel")),
    )(kvv, kvv, qv, qv, dov, dov, lse_t, lse_t, dl_t, dl_t)
    return dk.reshape(L, W), dv.reshape(L, W)


def _head_expand():
    r = lax.broadcasted_iota(jnp.int32, (DIL_HEADS, DIL_W), 0)
    c = lax.broadcasted_iota(jnp.int32, (DIL_HEADS, DIL_W), 1)
    E = ((c >= r * DIL_HEAD_DIM) & (c < (r + 1) * DIL_HEAD_DIM)).astype(F32)
    r2 = lax.broadcasted_iota(jnp.int32, (DIL_W, DIL_HEADS), 0)
    c2 = lax.broadcasted_iota(jnp.int32, (DIL_W, DIL_HEADS), 1)
    Et = ((r2 >= c2 * DIL_HEAD_DIM) & (r2 < (c2 + 1) * DIL_HEAD_DIM)).astype(F32)
    return E, Et


def _merge_weights(l0, l1, l2):
    m = jnp.maximum(jnp.maximum(l0, l1), l2)
    e = [jnp.exp(l - m) for l in (l0, l1, l2)]
    tot = e[0] + e[1] + e[2]
    return [v / tot for v in e]


def _merge_fwd(os_, lses, qz, z_off, *, name):
    def fn(o0, o1, o2, l0, l1, l2, z):
        E, _ = _head_expand()
        w = _merge_weights(l0, l1, l2)
        om = sum(_hi(wg, E) * og for wg, og in zip(w, (o0, o1, o2)))
        return [om * _silu(z)], []

    rows = [(o, 0, DIL_W) for o in os_] + [(l, 0, DIL_HEADS) for l in lses] + [(qz, z_off, DIL_W)]
    return _rowmap(fn, rows, [], [(DIL_W, BF16)], [], name=name)[0]


def _merge_bwd(os_, lses, qz, z_off, dog, *, name, comm=None):
    def fn(o0, o1, o2, l0, l1, l2, z, dg):
        E, Et = _head_expand()
        dg = dg.astype(F32)
        w = _merge_weights(l0, l1, l2)
        wf = [_hi(wg, E) for wg in w]
        os3 = (o0, o1, o2)
        om = sum(a * b for a, b in zip(wf, os3))
        dom = dg * _silu(z)
        dz = dg * om * _dsilu(z)
        dw = [_hi(dom * og, Et) for og in os3]
        tot = sum(a * b for a, b in zip(w, dw))
        return [wf[0] * dom, wf[1] * dom, wf[2] * dom, w[0] * tot, w[1] * tot, w[2] * tot, dz], []

    rows = ([(o, 0, DIL_W) for o in os_] + [(l, 0, DIL_HEADS) for l in lses] + [(qz, z_off, DIL_W), (dog, 0, DIL_W)])
    outs = [(DIL_W, BF16)] * 3 + [(DIL_HEADS, F32)] * 3 + [(DIL_W, BF16)]
    return _rowmap(fn, rows, [], outs, [], name=name, comm=comm)


def _adamw(gparts, w, m, v, *, name, tr=128):
    n, R, C = gparts.shape
    tr = _pick(R, tr)
    c1 = 1.0 - ADAM_B1 ** ADAM_STEP
    c2 = 1.0 - ADAM_B2 ** ADAM_STEP

    def body(g_ref, w_ref, m_ref, v_ref, go_ref, d_ref, mo_ref, vo_ref):
        g = g_ref[0].astype(F32)
        for i in range(1, n):
            g = g + g_ref[i].astype(F32)
        mn = ADAM_B1 * m_ref[...] + (1.0 - ADAM_B1) * g
        vn = ADAM_B2 * v_ref[...] + (1.0 - ADAM_B2) * jnp.square(g)
        d_ref[...] = -ADAM_LR * ((mn / c1) / (jnp.sqrt(vn / c2) + ADAM_EPS) + ADAM_WD * w_ref[...])
        go_ref[...] = g
        mo_ref[...] = mn
        vo_ref[...] = vn

    blk = pl.BlockSpec((tr, C), lambda i: (i, 0))
    sd = jax.ShapeDtypeStruct((R, C), F32)
    return pl.pallas_call(
        body, name=name, out_shape=[sd, sd, sd, sd], grid=(R // tr,),
        in_specs=[pl.BlockSpec((n, tr, C), lambda i: (0, i, 0)), blk, blk, blk],
        out_specs=[blk, blk, blk, blk],
        compiler_params=_cp(("parallel",)),
    )(gparts, w, m, v)


def _sum_parts(parts, *, name):
    n, R, C = parts.shape

    def body(p_ref, o_ref):
        s = p_ref[0]
        for i in range(1, n):
            s = s + p_ref[i]
        o_ref[...] = s

    return pl.pallas_call(
        body, name=name, out_shape=jax.ShapeDtypeStruct((R, C), F32),
        in_specs=[pl.BlockSpec(memory_space=pltpu.VMEM)], out_specs=pl.BlockSpec(memory_space=pltpu.VMEM),
    )(parts)


def _cols_from(g):
    _, R, Cs = g.shape
    return jnp.transpose(g, (1, 0, 2)).reshape(R, N_DEV * Cs)


def _col_parts(dw):
    R, C = dw.shape
    return jnp.transpose(dw.reshape(R, N_DEV, C // N_DEV), (1, 0, 2))


def _ag_cols(w_loc):
    if w_loc.shape[1] % LANE == 0:
        return _ag_comm(w_loc, cols=True), (lambda g: g)
    return _ag_comm(w_loc), _cols_from


def _rs_cols(dw):
    if (dw.shape[1] // N_DEV) % LANE == 0:
        return _a2a_comm(dw, cols=True)
    return _a2a_comm(_col_parts(dw))


def kernel(x, c, ada_w, ada_b, ln_g, ln_b, a_in_w, a_conv_w, a_conv_b, a_dt_bias, a_A_log, a_D, a_norm_g, a_out_w, kv_w, b_in_w, b_out_w, loss_target, m_ada_w, m_ada_b, m_ln_g, m_ln_b, m_a_in_w, m_a_conv_w, m_a_conv_b, m_a_dt_bias, m_a_A_log, m_a_D, m_a_norm_g, m_a_out_w, m_kv_w, m_b_in_w, m_b_out_w, v_ada_w, v_ada_b, v_ln_g, v_ln_b, v_a_in_w, v_a_conv_w, v_a_conv_b, v_a_dt_bias, v_a_A_log, v_a_D, v_a_norm_g, v_a_out_w, v_kv_w, v_b_in_w, v_b_out_w):
    L, D = x.shape[1], x.shape[2]
    H = a_dt_bias.shape[1]
    d_inner = H * SSD_HEAD_DIM
    G, N, P = SSD_N_GROUPS, SSD_D_STATE, SSD_HEAD_DIM
    K = H // G
    KP = K * P
    conv_dim = d_inner + 2 * G * N
    in_dim = d_inner + conv_dim + H
    in_pad = d_inner + conv_dim + LANE
    assert H <= LANE and KP % LANE == 0 and L % SSD_CHUNK == 0
    me = 4 * lax.axis_index("x") + 2 * lax.axis_index("y") + lax.axis_index("c")
    x2d, tgt = x[0], loss_target[0]

    c_all = _all_gather(c, "ag_c").reshape(N_DEV, D)
    mods = []
    for l in range(DEPTH):
        ab = lax.dynamic_slice(ada_b[l], (me * (3 * D // N_DEV),), (3 * D // N_DEV,))[None]
        mods.append(_matmul(c_all, ada_w[l], name=f"mod{l}", exact=True, a_silu=True, bias=ab))
    mod_all = _all_gather(jnp.stack(mods), "ag_mod")
    mod_me = lax.dynamic_index_in_dim(jnp.transpose(mod_all, (2, 1, 0, 3)).reshape(N_DEV, DEPTH, 3 * D), me, 0, False)
    shift = [mod_me[l, None, 0:D] for l in range(DEPTH)]
    scale = [mod_me[l, None, D:2 * D] for l in range(DEPTH)]
    gate = [mod_me[l, None, 2 * D:3 * D] for l in range(DEPTH)]

    w_in = _cols_from(_all_gather(a_in_w[0].astype(BF16), "ag_a_in"))
    w_in = jnp.pad(w_in, ((0, 0), (0, in_pad - in_dim)))
    conv_w = _all_gather(a_conv_w[0], "ag_conv_w")
    conv_w = jnp.transpose(conv_w, (1, 0, 2)).reshape(SSD_CONV_W, conv_dim)
    conv_b = _all_gather(a_conv_b, "ag_conv_b").reshape(1, conv_dim)
    norm_g = _all_gather(a_norm_g, "ag_norm_g").reshape(1, d_inner)

    def modulate(xin, l, name):
        fn = lambda xv, sc, sh: ([xv * (1.0 + sc) + sh], [])
        return _rowmap(fn, [(xin, 0, D)], [scale[l], shift[l]], [(D, BF16)], [], name=name)[0]

    def ln_fwd(xin, y, l, name):
        def fn(xv, yv, gt, g, b):
            u = DEEPNORM_ALPHA * xv + (1.0 + gt) * yv
            mu = jnp.mean(u, axis=1, keepdims=True)
            uc = u - mu
            var = jnp.mean(uc * uc, axis=1, keepdims=True)
            o = uc * lax.rsqrt(var + LN_EPS) * g + b
            return [o, o], []
        return _rowmap(fn, [(xin, 0, D), (y, 0, D)], [gate[l], ln_g[l:l + 1], ln_b[l:l + 1]],
                       [(D, F32), (D, BF16)], [], name=name)

    def ln_bwd(xin, y, dout, l, name):
        def fn(xv, yv, do, gt, g, b):
            u = DEEPNORM_ALPHA * xv + (1.0 + gt) * yv
            mu = jnp.mean(u, axis=1, keepdims=True)
            uc = u - mu
            var = jnp.mean(uc * uc, axis=1, keepdims=True)
            rs = lax.rsqrt(var + LN_EPS)
            xh = uc * rs
            dxh = do * g
            du = rs * (dxh - jnp.mean(dxh, axis=1, keepdims=True) - xh * jnp.mean(dxh * xh, axis=1, keepdims=True))
            return [DEEPNORM_ALPHA * du, (1.0 + gt) * du], [_csum(du * yv), _csum(do * xh), _csum(do)]
        return _rowmap(fn, [(xin, 0, D), (y, 0, D), (dout, 0, D)], [gate[l], ln_g[l:l + 1], ln_b[l:l + 1]],
                       [(D, F32), (D, BF16)], [(1, D)] * 3, name=name)

    def mod_bwd(xin, dh, dx_acc, l, name):
        def fn(xv, dhv, dxa, sc):
            return [dxa + dhv * (1.0 + sc)], [_csum(dhv * xv), _csum(dhv)]
        return _rowmap(fn, [(xin, 0, D), (dh, 0, D), (dx_acc, 0, D)], [scale[l]], [(D, F32)], [(1, D)] * 2, name=name)

    h0 = modulate(x2d, 0, "mod_h0")
    proj, g_aout = _matmul(h0, w_in, name="mm_a_in", tn=1152,
                           comm=_ag_comm(a_out_w[0].astype(BF16)))
    w_aout = g_aout.reshape(d_inner, D)
    xbc = _conv_fwd(proj, d_inner, conv_dim, conv_w, conv_b, name="conv_fwd")
    dt_raw = proj[:, d_inner + conv_dim:]
    padh = lambda a: jnp.pad(a, ((0, 0), (0, LANE - H)))
    bias_p, alog_p = padh(a_dt_bias), padh(a_A_log)
    dt_p, a_p = _ssd_prep(dt_raw, bias_p, alog_p, name="ssd_prep")
    dt_c = jnp.transpose(dt_p[:, :H].reshape(L, G, K), (1, 0, 2))
    a_c = jnp.transpose(a_p[:, :H].reshape(L, G, K), (1, 0, 2))
    a_r = jnp.transpose(a_c, (0, 2, 1))
    d_full = jnp.repeat(a_D.reshape(H), P)[None]
    ssd_in = (xbc, dt_c, a_c, a_r, d_full)
    cm_kv, fix_kv = _ag_cols(kv_w.astype(BF16))
    y_ssd, states, w_kv = _ssd_fwd(*ssd_in, d_inner=d_inner, name="ssd_fwd", comm=cm_kv)
    w_kv = fix_kv(w_kv)

    gw = d_inner // G

    def gnorm_fn(yv, zv, g):
        yg = yv * _silu(zv)
        r = lax.rsqrt(jnp.mean(yg * yg, axis=1, keepdims=True) + RMS_EPS)
        return [yg * r * g], []
    yn = _rowmap(gnorm_fn, [(y_ssd, 0, d_inner), (proj, 0, d_inner)], [norm_g], [(d_inner, BF16)], [],
                 name="gnorm_fwd", cw=gw, tr=1024)[0]
    cm_bin, fix_bin = _ag_cols(b_in_w[0].astype(BF16))
    ya, w_bin = _matmul(yn, w_aout, name="mm_a_out", comm=cm_bin)
    w_bin = fix_bin(w_bin)
    x1, x1b = ln_fwd(x2d, ya, 0, "ln0_fwd")

    cm_bout, fix_bout = _ag_cols(b_out_w[0].astype(BF16))
    kv, w_bout = _matmul(x1b, w_kv, name="mm_kv", out_dtype=BF16, comm=cm_bout)
    w_bout = fix_bout(w_bout)
    h1 = modulate(x1, 1, "mod_h1")
    q = _matmul(h1, w_bin[:, :DIL_N_GROUPS * DIL_W], name="mm_b_q", out_dtype=BF16)
    z1 = _matmul(h1, w_bin[:, DIL_N_GROUPS * DIL_W:], name="mm_b_z")
    os_, lse_c = [], []
    for g in range(DIL_N_GROUPS):
        o, lse = _attn_fwd(q, kv, g, name=f"attn_fwd{g}")
        os_.append(o)
        lse_c.append(lse)
    lses = [jnp.transpose(l, (1, 0, 2)).reshape(L, DIL_HEADS) for l in lse_c]
    og = _merge_fwd(os_, lses, z1, 0, name="merge_fwd")
    yb = _matmul(og, w_bout, name="mm_b_out")
    x2, _ = ln_fwd(x1, yb, 1, "ln1_fwd")

    def loss_fn(xv, tv):
        e = xv - tv
        return [e * (1.0 / D)], [_csum(e * e) * (0.5 / D)]
    dx2, loss_cols = _rowmap(loss_fn, [(x2, 0, D), (tgt, 0, D)], [], [(D, F32)], [(1, D)], name="loss")
    loss = lax.psum(jnp.sum(loss_cols), ("x", "y", "c"))

    dx1a, dyb, dgate1, dlng1, dlnb1 = ln_bwd(x1, yb, dx2, 1, "ln1_bwd")
    dw_bout = _matmul(og, dyb, name="mm_b_out_dw", ta=True, out_dtype=BF16)
    dog = _matmul(dyb, w_bout, name="mm_b_out_dx", tb=True, out_dtype=BF16)
    do0, do1, do2, dl0, dl1, dl2, dz1, r_bout = _merge_bwd(os_, lses, z1, 0, dog, name="merge_bwd",
                                                           comm=_rs_cols(dw_bout))
    dqs, dks, dvs = [], [], []
    for g, (do_g, dl_g) in enumerate(zip((do0, do1, do2), (dl0, dl1, dl2))):
        d = DIL_PATTERNS[g][1]
        dl_c = jnp.transpose(dl_g.reshape(L // d, d, DIL_HEADS), (1, 0, 2))
        dl_t = jnp.transpose(dl_c, (0, 2, 1))
        lse_t = jnp.transpose(lse_c[g], (0, 2, 1))
        dqs.append(_attn_bwd_q(q, kv, do_g, lse_c[g], dl_c, g, name=f"attn_bwd_q{g}"))
        dk, dv = _attn_bwd_kv(q, kv, do_g, lse_t, dl_t, g, name=f"attn_bwd_kv{g}")
        dks.append(dk)
        dvs.append(dv)
    dqz = jnp.concatenate(dqs + [dz1], axis=1)
    dkv = jnp.concatenate(dks + dvs, axis=1)
    dw_bin = _matmul(h1, dqz, name="mm_b_in_dw", ta=True, out_dtype=BF16)
    dh1 = _matmul(dqz, w_bin, name="mm_b_in_dx", tb=True)
    dx1b, dscale1, dshift1 = mod_bwd(x1, dh1, dx1a, 1, "mod1_bwd")
    dw_kv = _matmul(x1b, dkv, name="mm_kv_dw", ta=True, out_dtype=BF16)
    dx1kv = _matmul(dkv, w_kv, name="mm_kv_dx", tb=True)
    dx1 = _rowmap(lambda a, b: ([a + b], []), [(dx1b, 0, D), (dx1kv, 0, D)], [], [(D, F32)], [], name="add_dx1")[0]

    dxa, dya, dgate0, dlng0, dlnb0 = ln_bwd(x2d, ya, dx1, 0, "ln0_bwd")
    dw_aout = _matmul(yn, dya, name="mm_a_out_dw", ta=True, out_dtype=BF16)
    dyn = _matmul(dya, w_aout, name="mm_a_out_dx", tb=True)

    def gnorm_bwd_fn(yv, zv, dn, g):
        sz = _silu(zv)
        yg = yv * sz
        r = lax.rsqrt(jnp.mean(yg * yg, axis=1, keepdims=True) + RMS_EPS)
        nrm = yg * r
        dnn = dn * g
        dyg = r * (dnn - nrm * jnp.mean(dnn * nrm, axis=1, keepdims=True))
        return [dyg * sz, dyg * yv * _dsilu(zv)], [_csum(dn * nrm)]
    dy_ssd, dz, dnorm_g = _rowmap(gnorm_bwd_fn, [(y_ssd, 0, d_inner), (proj, 0, d_inner), (dyn, 0, d_inner)],
                                  [norm_g], [(d_inner, F32), (d_inner, BF16)], [(1, d_inner)], name="gnorm_bwd", cw=gw,
                                  tr=1024)
    dxs, dB, dC, da_t, s1_t, dD_f, r_kv, r_aout = _ssd_bwd(
        *ssd_in, states, dy_ssd, d_inner=d_inner, name="ssd_bwd",
        comm=_comm_join([_rs_cols(dw_kv), _a2a_comm(dw_aout.reshape(N_DEV, d_inner // N_DEV, D))]))
    dxbc_raw, dconv_w, dconv_b, r_bin = _conv_bwd(proj, d_inner, conv_dim, conv_w, conv_b, (dxs, dB, dC),
                                                  name="conv_bwd", tc=128, comm=_rs_cols(dw_bin))
    tokp = lambda t: padh(jnp.transpose(t, (1, 0, 2)).reshape(L, H))
    ddt_raw, ddt_bias_p, dA_log_p = _ssd_post(tokp(da_t), tokp(s1_t), dt_p, dt_raw, bias_p, alog_p, name="ssd_post")
    ddt_bias, dA_log = ddt_bias_p[:, :H], dA_log_p[:, :H]
    dD = jnp.sum(dD_f.reshape(H, P), axis=1)[None]
    dproj = jnp.concatenate([dz, dxbc_raw, ddt_raw], axis=1)
    dw_in = _matmul(h0, dproj, name="mm_a_in_dw", ta=True, out_dtype=BF16, tn=1152)[:, :in_dim]
    cs_in = in_dim // N_DEV
    by_c = jnp.transpose(dw_in.reshape(D, N_DEV // 2, 2, cs_in), (2, 1, 0, 3))
    my_c = lax.axis_index("c")
    keep = lax.dynamic_index_in_dim(by_c, my_c, 0, False)
    give = lax.dynamic_index_in_dim(by_c, 1 - my_c, 0, False)
    got = _run_comm(_pair_comm(give), "rs_a_in_pair")[0]
    pair_sum = _rowmap(lambda a, b: ([a.astype(F32) + b.astype(F32)], []),
                       [(keep.reshape(-1, cs_in), 0, cs_in), (got.reshape(-1, cs_in), 0, cs_in)], [],
                       [(cs_in, BF16)], [], name="rs_a_in_add", tr=512)[0].reshape(N_DEV // 2, D, cs_in)
    dh0, r_in = _matmul(dproj, w_in, name="mm_a_in_dx", tb=True, tk=1152,
                        comm=_quad_comm(pair_sum))
    grad_x, dscale0, dshift0 = mod_bwd(x2d, dh0, dxa, 0, "mod0_bwd")

    dmod = jnp.concatenate([dshift0, dscale0, dgate0, dshift1, dscale1, dgate1], axis=1)
    pieces = [dmod, dlng0, dlng1, dlnb0, dlnb1, ddt_bias, dA_log, dD,
              dconv_w.reshape(1, -1), dconv_b, dnorm_g]
    sizes = [p.shape[1] for p in pieces]
    tot = sum(sizes)
    tot_pad = -(-tot // (8 * LANE)) * (8 * LANE)
    packed = jnp.pad(jnp.concatenate(pieces, axis=1), ((0, 0), (0, tot_pad - tot))).reshape(tot_pad // LANE, LANE)
    packed_all = _all_gather(packed, "ag_small")
    small = _sum_parts(packed_all, name="sum_small").reshape(tot_pad)
    offs = np.cumsum([0] + sizes)
    seg = lambda i: small[int(offs[i]):int(offs[i + 1])]
    g_ada_b = seg(0).reshape(DEPTH, 3 * D)
    g_ln_g = jnp.stack([seg(1), seg(2)])
    g_ln_b = jnp.stack([seg(3), seg(4)])
    g_dt_bias, g_A_log, g_D = seg(5)[None], seg(6)[None], seg(7)[None]
    cs = conv_dim // N_DEV
    g_conv_w = lax.dynamic_slice(seg(8).reshape(SSD_CONV_W, conv_dim), (0, me * cs), (SSD_CONV_W, cs))[None]
    g_conv_b = lax.dynamic_slice(seg(9), (me * cs,), (cs,))[None]
    ns = d_inner // N_DEV
    g_norm_g = lax.dynamic_slice(seg(10), (me * ns,), (ns,))[None]

    ms = 3 * D // N_DEV
    dmod_all = packed_all.reshape(N_DEV, tot_pad)[:, :DEPTH * 3 * D].reshape(N_DEV, DEPTH, 3 * D)
    dmod_cols = lax.dynamic_slice(dmod_all, (0, 0, me * ms), (N_DEV, DEPTH, ms))
    c_t = jnp.transpose(c_all)
    g_ada_w = jnp.stack([_matmul(c_t, dmod_cols[:, l], name=f"mm_ada_dw{l}", exact=True, a_silu=True)
                         for l in range(DEPTH)])[None]

    def upd(parts, w, m, v, name):
        shp = w.shape
        r2 = lambda a: a.reshape(-1, shp[-1])
        return [o.reshape(shp) for o in _adamw(parts, r2(w), r2(m), r2(v), name=name)]

    res = {}
    res["ada_w"] = upd(g_ada_w.reshape(1, -1, ms), ada_w, m_ada_w, v_ada_w, "adam_ada_w")
    res["a_in_w"] = upd(r_in, a_in_w, m_a_in_w, v_a_in_w, "adam_a_in")
    res["a_out_w"] = upd(r_aout, a_out_w, m_a_out_w, v_a_out_w, "adam_a_out")
    res["kv_w"] = upd(r_kv, kv_w, m_kv_w, v_kv_w, "adam_kv")
    res["b_in_w"] = upd(r_bin, b_in_w, m_b_in_w, v_b_in_w, "adam_b_in")
    res["b_out_w"] = upd(r_bout, b_out_w, m_b_out_w, v_b_out_w, "adam_b_out")

    small_names = ["ada_b", "ln_g", "ln_b", "a_conv_w", "a_conv_b", "a_dt_bias", "a_A_log", "a_D", "a_norm_g"]
    small_g = [g_ada_b, g_ln_g, g_ln_b, g_conv_w, g_conv_b, g_dt_bias, g_A_log, g_D, g_norm_g]
    small_w = [ada_b, ln_g, ln_b, a_conv_w, a_conv_b, a_dt_bias, a_A_log, a_D, a_norm_g]
    small_m = [m_ada_b, m_ln_g, m_ln_b, m_a_conv_w, m_a_conv_b, m_a_dt_bias, m_a_A_log, m_a_D, m_a_norm_g]
    small_v = [v_ada_b, v_ln_g, v_ln_b, v_a_conv_w, v_a_conv_b, v_a_dt_bias, v_a_A_log, v_a_D, v_a_norm_g]
    ssz = [int(np.prod(w.shape)) for w in small_w]
    stot = sum(ssz)
    spad = -(-stot // (8 * LANE)) * (8 * LANE)

    def pack(arrs, fill):
        flat = jnp.concatenate([a.reshape(-1) for a in arrs])
        return jnp.concatenate([flat, jnp.full((spad - stot,), fill, F32)]).reshape(spad // LANE, LANE)

    sres = _adamw(pack(small_g, 0.0)[None], pack(small_w, 0.0), pack(small_m, 0.0), pack(small_v, 1.0), name="adam_small")
    soffs = np.cumsum([0] + ssz)
    for i, nme in enumerate(small_names):
        res[nme] = [r.reshape(-1)[int(soffs[i]):int(soffs[i + 1])].reshape(small_w[i].shape) for r in sres]

    order = ["ada_w", "ada_b", "ln_g", "ln_b", "a_in_w", "a_conv_w", "a_conv_b", "a_dt_bias", "a_A_log", "a_D",
             "a_norm_g", "a_out_w", "kv_w", "b_in_w", "b_out_w"]
    outs = [loss, grad_x[None]]
    for j in range(4):
        outs += [res[nme][j] for nme in order]
    return tuple(outs)
```

```python
import functools
import math

import numpy as np
import jax
import jax.numpy as jnp
from jax import lax
from jax.experimental import pallas as pl
from jax.experimental.pallas import tpu as pltpu

F32, BF16 = jnp.float32, jnp.bfloat16
HI = lax.Precision.HIGHEST
MESH = pl.DeviceIdType.MESH
N_DEV = 8

SSD_HEAD_DIM = 64
SSD_N_GROUPS = 8
SSD_D_STATE = 128
SSD_CONV_W = 4
SSD_CHUNK = 256
DIL_PATTERNS = ((128, 1), (512, 4), (2048, 16))
DIL_N_GROUPS = 3
DIL_HEADS = 8
DIL_HEAD_DIM = 128
DIL_BLOCK = 128
DIL_W = DIL_HEADS * DIL_HEAD_DIM
DEPTH = 2
DEEPNORM_ALPHA = (2 * DEPTH) ** 0.25
LN_EPS = 1e-5
RMS_EPS = 1e-5
ADAM_LR, ADAM_B1, ADAM_B2, ADAM_EPS, ADAM_WD, ADAM_STEP = 0.001, 0.9, 0.999, 1e-08, 0.01, 10
LANE = 128
NEG = -1e30
VMEM_LIMIT = 56 * 1024 * 1024


def _cp(sem=None):
    return pltpu.CompilerParams(dimension_semantics=sem, vmem_limit_bytes=VMEM_LIMIT)


def _silu(x):
    return x * jax.nn.sigmoid(x)


def _dsilu(x):
    s = jax.nn.sigmoid(x)
    return s * (1.0 + x * (1.0 - s))


def _softplus(x):
    return jnp.maximum(x, 0.0) + jnp.log(1.0 + jnp.exp(-jnp.abs(x)))


def _nt(a, b):
    return lax.dot_general(a, b, (((1,), (1,)), ((), ())), preferred_element_type=F32)


def _nn(a, b):
    return jnp.dot(a, b, preferred_element_type=F32)


def _hi(a, b):
    return jnp.dot(a, b, preferred_element_type=F32, precision=HI)


def _pick(n, pref, align=LANE):
    if n <= pref:
        return n
    for t in range(pref - pref % align, 0, -align):
        if n % t == 0:
            return t
    return n


class _Comm:
    def __init__(self, ins, outs, sems, start, finish):
        self.ins, self.outs, self.sems, self.start, self.finish = ins, outs, sems, start, finish


def _comm_join(comms):
    ins = [a for c in comms for a in c.ins]
    outs = [a for c in comms for a in c.outs]
    sems = [a for c in comms for a in c.sems]

    def split(refs, attr):
        res, i = [], 0
        for c in comms:
            n = len(getattr(c, attr))
            res.append(refs[i:i + n])
            i += n
        return res

    def start(cin, cout, csem):
        for c, a, b, d in zip(comms, split(cin, "ins"), split(cout, "outs"), split(csem, "sems")):
            c.start(a, b, d)

    def finish(cin, cout, csem):
        for c, a, b, d in zip(comms, split(cin, "ins"), split(cout, "outs"), split(csem, "sems")):
            c.finish(a, b, d)

    return _Comm(ins, outs, sems, start, finish)


def _ag_comm(v, cols=False):
    if cols:
        R, Cs = v.shape
        assert Cs % LANE == 0
        out_sd = jax.ShapeDtypeStruct((R, N_DEV * Cs), v.dtype)
    else:
        out_sd = jax.ShapeDtypeStruct((N_DEV,) + v.shape, v.dtype)

    def parts(x_ref, out_ref, send_sems, recv_sems, local_sem):
        x, y, c = lax.axis_index("x"), lax.axis_index("y"), lax.axis_index("c")
        me, sibling = (x, y, c), (x, y, 1 - c)
        chips = [(1 - x, y), (x, 1 - y), (1 - x, 1 - y)]

        def slab(px, py, pc):
            k = 4 * px + 2 * py + pc
            if cols:
                return out_ref.at[:, pl.ds(pl.multiple_of(k * Cs, LANE), Cs)]
            return out_ref.at[k]

        def copy(k, block, to, src=None):
            return pltpu.make_async_remote_copy(
                src_ref=slab(*block) if src is None else src, dst_ref=slab(*block),
                send_sem=send_sems.at[k], recv_sem=recv_sems.at[k], device_id=to, device_id_type=MESH)

        mine = pltpu.make_async_copy(x_ref, slab(*me), local_sem)
        first = [copy(0, me, sibling, src=x_ref)]
        first += [copy(1 + j, me, (*chip, c), src=x_ref) for j, chip in enumerate(chips)]
        passed = [copy(4 + j, (*chip, c), sibling) for j, chip in enumerate(chips)]
        return me, sibling, chips, c, copy, mine, first, passed

    def start(cin, cout, csem):
        _, _, _, _, _, mine, first, _ = parts(cin[0], cout[0], *csem)
        mine.start()
        for cp in first:
            cp.start()

    def finish(cin, cout, csem):
        me, sibling, chips, c, copy, mine, first, passed = parts(cin[0], cout[0], *csem)
        for j, chip in enumerate(chips):
            copy(1 + j, (*chip, c), me).wait_recv()
            passed[j].start()
        copy(0, sibling, me).wait_recv()
        for j, chip in enumerate(chips):
            copy(4 + j, (*chip, 1 - c), me).wait_recv()
        for cp in first + passed:
            cp.wait_send()
        mine.wait()

    return _Comm([v], [out_sd],
                 [pltpu.SemaphoreType.DMA((7,)), pltpu.SemaphoreType.DMA((7,)), pltpu.SemaphoreType.DMA], start, finish)


def _a2a_comm(v, cols=False):
    if cols:
        R, C = v.shape
        Cs = C // N_DEV
        assert Cs % LANE == 0
        out_sd = jax.ShapeDtypeStruct((N_DEV, R, Cs), v.dtype)
    else:
        out_sd = jax.ShapeDtypeStruct(v.shape, v.dtype)

    def parts(x_ref, out_ref, send_sems, recv_sems, local_sem):
        x, y, c = lax.axis_index("x"), lax.axis_index("y"), lax.axis_index("c")
        me = 4 * x + 2 * y + c

        def src(k):
            if cols:
                return x_ref.at[:, pl.ds(pl.multiple_of(k * Cs, LANE), Cs)]
            return x_ref.at[k]

        mine = pltpu.make_async_copy(src(me), out_ref.at[me], local_sem)
        sends, recvs = [], []
        for k, mask in enumerate(range(1, N_DEV)):
            px = 1 - x if (mask >> 2) & 1 else x
            py = 1 - y if (mask >> 1) & 1 else y
            pc = 1 - c if mask & 1 else c
            peer = 4 * px + 2 * py + pc
            sends.append(pltpu.make_async_remote_copy(
                src_ref=src(peer), dst_ref=out_ref.at[me],
                send_sem=send_sems.at[k], recv_sem=recv_sems.at[k], device_id=(px, py, pc), device_id_type=MESH))
            recvs.append(pltpu.make_async_remote_copy(
                src_ref=src(me), dst_ref=out_ref.at[peer],
                send_sem=send_sems.at[k], recv_sem=recv_sems.at[k], device_id=(px, py, pc), device_id_type=MESH))
        return mine, sends, recvs

    def start(cin, cout, csem):
        mine, sends, _ = parts(cin[0], cout[0], *csem)
        mine.start()
        for cp in sends:
            cp.start()

    def finish(cin, cout, csem):
        mine, sends, recvs = parts(cin[0], cout[0], *csem)
        for cp in recvs:
            cp.wait_recv()
        for cp in sends:
            cp.wait_send()
        mine.wait()

    return _Comm([v], [out_sd],
                 [pltpu.SemaphoreType.DMA((7,)), pltpu.SemaphoreType.DMA((7,)), pltpu.SemaphoreType.DMA], start, finish)


def _pair_comm(v4):
    def copy(x_ref, out_ref, send_sem, recv_sem):
        x, y, c = lax.axis_index("x"), lax.axis_index("y"), lax.axis_index("c")
        return pltpu.make_async_remote_copy(src_ref=x_ref, dst_ref=out_ref, send_sem=send_sem, recv_sem=recv_sem,
                                            device_id=(x, y, 1 - c), device_id_type=MESH)

    def start(cin, cout, csem):
        copy(cin[0], cout[0], *csem).start()

    def finish(cin, cout, csem):
        copy(cin[0], cout[0], *csem).wait()

    return _Comm([v4], [jax.ShapeDtypeStruct(v4.shape, v4.dtype)],
                 [pltpu.SemaphoreType.DMA, pltpu.SemaphoreType.DMA], start, finish)


def _quad_comm(v4):
    def parts(x_ref, out_ref, send_sems, recv_sems, local_sem):
        x, y, c = lax.axis_index("x"), lax.axis_index("y"), lax.axis_index("c")
        me = 2 * x + y
        mine = pltpu.make_async_copy(x_ref.at[me], out_ref.at[me], local_sem)
        sends, recvs = [], []
        for k, mask in enumerate(range(1, 4)):
            px = 1 - x if (mask >> 1) & 1 else x
            py = 1 - y if mask & 1 else y
            peer = 2 * px + py
            sends.append(pltpu.make_async_remote_copy(
                src_ref=x_ref.at[peer], dst_ref=out_ref.at[me],
                send_sem=send_sems.at[k], recv_sem=recv_sems.at[k], device_id=(px, py, c), device_id_type=MESH))
            recvs.append(pltpu.make_async_remote_copy(
                src_ref=x_ref.at[me], dst_ref=out_ref.at[peer],
                send_sem=send_sems.at[k], recv_sem=recv_sems.at[k], device_id=(px, py, c), device_id_type=MESH))
        return mine, sends, recvs

    def start(cin, cout, csem):
        mine, sends, _ = parts(cin[0], cout[0], *csem)
        mine.start()
        for cp in sends:
            cp.start()

    def finish(cin, cout, csem):
        mine, sends, recvs = parts(cin[0], cout[0], *csem)
        for cp in recvs:
            cp.wait_recv()
        for cp in sends:
            cp.wait_send()
        mine.wait()

    return _Comm([v4], [jax.ShapeDtypeStruct(v4.shape, v4.dtype)],
                 [pltpu.SemaphoreType.DMA((3,)), pltpu.SemaphoreType.DMA((3,)), pltpu.SemaphoreType.DMA], start, finish)


def _run_comm(comm, name):
    nci, nco = len(comm.ins), len(comm.outs)

    def body(*refs):
        comm.start(refs[:nci], refs[nci:nci + nco], refs[nci + nco:])
        comm.finish(refs[:nci], refs[nci:nci + nco], refs[nci + nco:])

    anyspec = pl.BlockSpec(memory_space=pl.ANY)
    return pl.pallas_call(body, name=name, out_shape=list(comm.outs), in_specs=[anyspec] * nci,
                          out_specs=[anyspec] * nco, scratch_shapes=list(comm.sems))(*comm.ins)


def _all_gather(v, name):
    return _run_comm(_ag_comm(v), name)[0]


def _pcall(body, args, *, name, grid, in_specs, out_specs, out_shape, scratch=(), sem=None, comm=None):
    out_shape, out_specs = list(out_shape), list(out_specs)
    if comm is None:
        return pl.pallas_call(body, name=name, grid=grid, in_specs=list(in_specs), out_specs=out_specs,
                              out_shape=out_shape, scratch_shapes=list(scratch), compiler_params=_cp(sem))(*args)
    ni, no, ns = len(args), len(out_shape), len(scratch)
    nci, nco = len(comm.ins), len(comm.outs)

    def wrapped(*refs):
        ins, cin = refs[:ni], refs[ni:ni + nci]
        o0 = ni + nci
        outs, cout = refs[o0:o0 + no], refs[o0 + no:o0 + no + nco]
        s0 = o0 + no + nco
        scr, csem = refs[s0:s0 + ns], refs[s0 + ns:]
        first = functools.reduce(jnp.logical_and, [pl.program_id(a) == 0 for a in range(len(grid))])
        last = functools.reduce(jnp.logical_and, [pl.program_id(a) == g - 1 for a, g in enumerate(grid)])

        @pl.when(first)
        def _():
            comm.start(cin, cout, csem)

        body(*ins, *outs, *scr)

        @pl.when(last)
        def _():
            comm.finish(cin, cout, csem)

    anyspec = pl.BlockSpec(memory_space=pl.ANY)
    res = pl.pallas_call(
        wrapped, name=name, grid=grid, in_specs=list(in_specs) + [anyspec] * nci,
        out_specs=out_specs + [anyspec] * nco, out_shape=out_shape + list(comm.outs),
        scratch_shapes=list(scratch) + list(comm.sems),
        compiler_params=_cp(("arbitrary",) * len(grid)))(*args, *comm.ins)
    return list(res[:no]) + list(res[no:])


def _matmul(a, b, *, name, ta=False, tb=False, out_dtype=F32, tm=1024, tn=1024, tk=2048,
            exact=False, a_silu=False, bias=None, comm=None):
    (K, M) = a.shape if ta else a.shape[::-1]
    (N, K2) = b.shape if tb else b.shape[::-1]
    assert K == K2, (a.shape, b.shape, ta, tb)
    tm, tn, tk = _pick(M, tm), _pick(N, tn), _pick(K, tk)
    nk = K // tk
    a_spec = pl.BlockSpec((tk, tm), lambda i, j, k: (k, i)) if ta else pl.BlockSpec((tm, tk), lambda i, j, k: (i, k))
    b_spec = pl.BlockSpec((tn, tk), lambda i, j, k: (j, k)) if tb else pl.BlockSpec((tk, tn), lambda i, j, k: (k, j))
    dims = (((0,) if ta else (1,), (1,) if tb else (0,)), ((), ()))
    in_specs, args = [a_spec, b_spec], [a, b]
    if bias is not None:
        if bias.shape[0] == 1:
            in_specs.append(pl.BlockSpec((1, tn), lambda i, j, k: (0, j)))
        else:
            in_specs.append(pl.BlockSpec((tm, tn), lambda i, j, k: (i, j)))
        args.append(bias)

    def body(*refs):
        a_ref, b_ref = refs[0], refs[1]
        bias_ref = refs[2] if bias is not None else None
        o_ref = refs[2 + (bias is not None)]
        av, bv = a_ref[...], b_ref[...]
        if a_silu:
            av = _silu(av.astype(F32))
        if exact:
            p = lax.dot_general(av.astype(F32), bv.astype(F32), dims, preferred_element_type=F32, precision=HI)
        else:
            p = lax.dot_general(av.astype(BF16), bv.astype(BF16), dims, preferred_element_type=F32)

        def fin(r):
            if bias_ref is not None:
                r = r + bias_ref[...]
            o_ref[...] = r.astype(o_ref.dtype)

        if nk == 1:
            fin(p)
        else:
            acc = refs[-1]
            k = pl.program_id(2)

            @pl.when(k == 0)
            def _():
                acc[...] = p

            @pl.when(k > 0)
            def _():
                acc[...] += p

            @pl.when(k == nk - 1)
            def _():
                fin(acc[...])

    res = _pcall(
        body, args, name=name,
        out_shape=[jax.ShapeDtypeStruct((M, N), out_dtype)],
        grid=(M // tm, N // tn, nk),
        in_specs=in_specs,
        out_specs=[pl.BlockSpec((tm, tn), lambda i, j, k: (i, j))],
        scratch=[pltpu.VMEM((tm, tn), F32)] if nk > 1 else [],
        sem=("parallel", "parallel", "arbitrary"), comm=comm)
    return res[0] if comm is None else res


def _rowmap(fn, rows, bcasts, outs, accs, *, name, tr=256, cw=None, comm=None):
    L = rows[0][0].shape[0]
    tr = _pick(L, tr)
    nr, nb, no, na = len(rows), len(bcasts), len(outs), len(accs)
    if cw is None:
        ncol = 1
        widths = [w for (_, _, w) in rows]
    else:
        wtot = rows[0][2]
        ncol = wtot // cw
        widths = [cw] * nr
    in_specs, args = [], []
    for (arr, off, w), bw in zip(rows, widths):
        assert off % bw == 0
        in_specs.append(pl.BlockSpec((tr, bw), functools.partial(lambda j, i, o: (i, o + j), o=off // bw)))
        args.append(arr)
    for arr in bcasts:
        bw = arr.shape[1] if cw is None else cw
        in_specs.append(pl.BlockSpec((arr.shape[0], bw), lambda j, i: (0, j)))
        args.append(arr)
    out_shape, out_specs = [], []
    for spec in outs:
        if len(spec) == 2:
            (w, dt), off = spec, 0
            bw = w if cw is None else cw
        else:
            w, dt, off, bw = spec
        out_shape.append(jax.ShapeDtypeStruct((L, w), dt))
        out_specs.append(pl.BlockSpec((tr, bw), functools.partial(lambda j, i, o: (i, o + j), o=off // bw)))
    for (r, w) in accs:
        bw = w if cw is None else cw
        out_shape.append(jax.ShapeDtypeStruct((r, w), F32))
        out_specs.append(pl.BlockSpec((r, bw), lambda j, i: (0, j)))

    def body(*refs):
        ins = [r[...] for r in refs[:nr + nb]]
        o_refs = refs[nr + nb:nr + nb + no]
        a_refs = refs[nr + nb + no:]
        o, a = fn(*ins)
        for ref, val in zip(o_refs, o):
            ref[...] = val.astype(ref.dtype)
        if na:
            @pl.when(pl.program_id(1) == 0)
            def _():
                for ref in a_refs:
                    ref[...] = jnp.zeros_like(ref)

            for ref, val in zip(a_refs, a):
                ref[...] += val

    return _pcall(body, args, name=name, out_shape=out_shape, grid=(ncol, L // tr), in_specs=in_specs,
                  out_specs=out_specs, sem=("parallel", "arbitrary"), comm=comm)


def _csum(v):
    return jnp.sum(v, axis=0, keepdims=True)


def _shift_rows(v, s, rows):
    if s == 0:
        return v
    n = v.shape[0]
    r = pltpu.roll(v, s % n, 0)
    if s > 0:
        return jnp.where(rows >= s, r, 0.0)
    return jnp.where(rows < n + s, r, 0.0)


def _conv_fwd(proj, off, width, w, b, *, name, tc=256):
    L = proj.shape[0]
    tc = _pick(width, tc)

    def body(x_ref, w_ref, b_ref, o_ref):
        x = x_ref[...]
        rows = lax.broadcasted_iota(jnp.int32, x.shape, 0)
        acc = jnp.zeros_like(x) + b_ref[...]
        for k in range(SSD_CONV_W):
            acc = acc + w_ref[k:k + 1, :] * _shift_rows(x, SSD_CONV_W - 1 - k, rows)
        o_ref[...] = _silu(acc)

    return pl.pallas_call(
        body, name=name, out_shape=jax.ShapeDtypeStruct((L, width), F32), grid=(width // tc,),
        in_specs=[pl.BlockSpec((L, tc), functools.partial(lambda j, o: (0, o + j), o=off // tc)),
                  pl.BlockSpec((SSD_CONV_W, tc), lambda j: (0, j)), pl.BlockSpec((1, tc), lambda j: (0, j))],
        out_specs=pl.BlockSpec((L, tc), lambda j: (0, j)),
        compiler_params=_cp(("parallel",)),
    )(proj, w, b)


def _conv_bwd(proj, off, width, w, b, dys, *, name, tc=256, comm=None):
    L = proj.shape[0]
    tc = _pick(width, tc)
    ntile = [d.shape[1] // tc for d in dys]
    assert all(d.shape[1] % tc == 0 for d in dys) and sum(ntile) == width // tc
    first = [sum(ntile[:i]) for i in range(len(dys))]

    def body(x_ref, w_ref, b_ref, *rest):
        dy_refs, (dx_ref, dw_ref, db_ref) = rest[:len(dys)], rest[len(dys):]
        j = pl.program_id(0)
        dy = dy_refs[0][...]
        for i in range(1, len(dys)):
            dy = jnp.where(j >= first[i], dy_refs[i][...], dy)
        x = x_ref[...]
        rows = lax.broadcasted_iota(jnp.int32, x.shape, 0)
        xs = [_shift_rows(x, SSD_CONV_W - 1 - k, rows) for k in range(SSD_CONV_W)]
        pre = jnp.zeros_like(x) + b_ref[...]
        for k in range(SSD_CONV_W):
            pre = pre + w_ref[k:k + 1, :] * xs[k]
        dpre = dy * _dsilu(pre)
        dx = jnp.zeros_like(x)
        for k in range(SSD_CONV_W):
            dx = dx + w_ref[k:k + 1, :] * _shift_rows(dpre, -(SSD_CONV_W - 1 - k), rows)
            dw_ref[k:k + 1, :] = _csum(dpre * xs[k])
        dx_ref[...] = dx.astype(dx_ref.dtype)
        db_ref[...] = _csum(dpre)

    dy_specs = [pl.BlockSpec((L, tc), functools.partial(lambda j, f, n: (0, jnp.clip(j - f, 0, n - 1)), f=f, n=n))
                for f, n in zip(first, ntile)]
    return _pcall(
        body, (proj, w, b, *dys), name=name,
        out_shape=[jax.ShapeDtypeStruct((L, width), BF16), jax.ShapeDtypeStruct((SSD_CONV_W, width), F32),
                   jax.ShapeDtypeStruct((1, width), F32)],
        grid=(width // tc,),
        in_specs=[pl.BlockSpec((L, tc), functools.partial(lambda j, o: (0, o + j), o=off // tc)),
                  pl.BlockSpec((SSD_CONV_W, tc), lambda j: (0, j)), pl.BlockSpec((1, tc), lambda j: (0, j))] + dy_specs,
        out_specs=[pl.BlockSpec((L, tc), lambda j: (0, j)), pl.BlockSpec((SSD_CONV_W, tc), lambda j: (0, j)),
                   pl.BlockSpec((1, tc), lambda j: (0, j))],
        sem=("parallel",), comm=comm)


def _tri(Q):
    ri = lax.broadcasted_iota(jnp.int32, (Q, Q), 0)
    ci = lax.broadcasted_iota(jnp.int32, (Q, Q), 1)
    return ri >= ci, ri <= ci


def _ssd_prep(dt_raw, bias, alog, *, name):
    L, W = dt_raw.shape
    Q = SSD_CHUNK

    def body(r_ref, b_ref, al_ref, dt_ref, a_ref):
        lower, _ = _tri(Q)
        dt = _softplus(r_ref[...] + b_ref[...])
        dt_ref[...] = dt
        a_ref[...] = _hi(lower.astype(F32), dt * (-jnp.exp(al_ref[...])))

    blk = pl.BlockSpec((Q, W), lambda c: (c, 0))
    one = pl.BlockSpec((1, W), lambda c: (0, 0))
    sd = jax.ShapeDtypeStruct((L, W), F32)
    return _pcall(body, (dt_raw, bias, alog), name=name, out_shape=[sd, sd], grid=(L // Q,),
                  in_specs=[blk, one, one], out_specs=[blk, blk], sem=("parallel",))


def _ssd_post(da, s1, dt, dt_raw, bias, alog, *, name):
    L, W = da.shape
    Q = SSD_CHUNK

    def body(da_ref, s1_ref, dt_ref, r_ref, b_ref, al_ref, o_ref, db_ref, dal_ref):
        _, upper = _tri(Q)
        A = -jnp.exp(al_ref[...])
        ddtA = _hi(upper.astype(F32), da_ref[...])
        ddt_raw = (ddtA * A + s1_ref[...]) * jax.nn.sigmoid(r_ref[...] + b_ref[...])
        o_ref[...] = ddt_raw.astype(o_ref.dtype)

        @pl.when(pl.program_id(0) == 0)
        def _():
            db_ref[...] = jnp.zeros_like(db_ref)
            dal_ref[...] = jnp.zeros_like(dal_ref)

        db_ref[...] += _csum(ddt_raw)
        dal_ref[...] += _csum(ddtA * dt_ref[...]) * A

    blk = pl.BlockSpec((Q, W), lambda c: (c, 0))
    one = pl.BlockSpec((1, W), lambda c: (0, 0))
    return _pcall(body, (da, s1, dt, dt_raw, bias, alog), name=name,
                  out_shape=[jax.ShapeDtypeStruct((L, W), BF16), jax.ShapeDtypeStruct((1, W), F32),
                             jax.ShapeDtypeStruct((1, W), F32)],
                  grid=(L // Q,), in_specs=[blk, blk, blk, blk, one, one], out_specs=[blk, one, one],
                  sem=("arbitrary",))


def _head_sum(v, K, KP):
    P = KP // K
    t_r = lax.broadcasted_iota(jnp.int32, (KP, K), 0)
    t_c = lax.broadcasted_iota(jnp.int32, (KP, K), 1)
    Et = ((t_r >= t_c * P) & (t_r < (t_c + 1) * P)).astype(BF16)
    hi = v.astype(BF16)
    lo = (v - hi.astype(F32)).astype(BF16)
    return _nn(hi, Et) + _nn(lo, Et)


def _half_masks():
    li = lax.broadcasted_iota(jnp.int32, (1, LANE), 1)
    return [(li < SSD_HEAD_DIM).astype(F32), (li >= SSD_HEAD_DIM).astype(F32)]


def _ssd_specs(K, KP, d_inner, rev, nc):
    Q, N, G = SSD_CHUNK, SSD_D_STATE, SSD_N_GROUPS
    cidx = (lambda c: nc - 1 - c) if rev else (lambda c: c)
    b_off, c_off = d_inner // N, d_inner // N + G
    return [
        pl.BlockSpec((Q, KP), lambda g, c: (cidx(c), g)),
        pl.BlockSpec((Q, N), lambda g, c: (cidx(c), b_off + g)),
        pl.BlockSpec((Q, N), lambda g, c: (cidx(c), c_off + g)),
        pl.BlockSpec((None, Q, K), lambda g, c: (g, cidx(c), 0)),
        pl.BlockSpec((None, Q, K), lambda g, c: (g, cidx(c), 0)),
        pl.BlockSpec((None, K, Q), lambda g, c: (g, 0, cidx(c))),
        pl.BlockSpec((1, KP), lambda g, c: (0, g)),
    ]


def _expand_heads(vc, K):
    Q = vc.shape[0]
    left = lax.broadcasted_iota(jnp.int32, (Q, LANE), 1) < SSD_HEAD_DIM
    parts = []
    for pr in range(K // 2):
        a = jnp.broadcast_to(vc[:, 2 * pr:2 * pr + 1], (Q, LANE))
        b = jnp.broadcast_to(vc[:, 2 * pr + 1:2 * pr + 2], (Q, LANE))
        parts.append(jnp.where(left, a, b))
    return jnp.concatenate(parts, axis=1) if len(parts) > 1 else parts[0]


def _ssd_fwd(xbc, dt_c, a_c, a_r, d_full, *, d_inner, name, comm=None):
    L = xbc.shape[0]
    G, N, Q, P = SSD_N_GROUPS, SSD_D_STATE, SSD_CHUNK, SSD_HEAD_DIM
    KP = d_inner // G
    K = KP // P
    nc = L // Q
    npair = KP // LANE

    def body(xs_ref, b_ref, c_ref, dtc_ref, ac_ref, ar_ref, df_ref, y_ref, st_ref, S):
        @pl.when(pl.program_id(1) == 0)
        def _():
            S[...] = jnp.zeros_like(S)

        lower, _ = _tri(Q)
        st_ref[...] = S[...]
        xs = xs_ref[...]
        Bm, Cm = b_ref[...], c_ref[...]
        Bb, Cb = Bm.astype(BF16), Cm.astype(BF16)
        a_c, a_r = ac_ref[...], ar_ref[...]
        a_f = _expand_heads(a_c, K)
        X = xs * _expand_heads(dtc_ref[...], K)
        ea = jnp.exp(a_f)
        alast = a_f[Q - 1:Q, :]
        tail = jnp.exp(alast - a_f)
        cb = _nt(Cb, Bb)
        Sv = S[...]
        yoff = _nn(Cb, Sv.astype(BF16)) * ea
        skip = xs * df_ref[...]
        masks = _half_masks()
        for pr in range(npair):
            Xp = X[:, pr * LANE:(pr + 1) * LANE]
            acc = yoff[:, pr * LANE:(pr + 1) * LANE] + skip[:, pr * LANE:(pr + 1) * LANE]
            for hh in range(2):
                k = 2 * pr + hh
                seg = a_c[:, k:k + 1] - a_r[k:k + 1, :]
                dec = jnp.where(lower, jnp.exp(jnp.minimum(seg, 0.0)), 0.0)
                acc = acc + _nn((cb * dec).astype(BF16), (Xp * masks[hh]).astype(BF16))
            y_ref[:, pr * LANE:(pr + 1) * LANE] = acc
        Bt = Bm.T
        S[...] = Sv * jnp.exp(alast) + _nn(Bt.astype(BF16), (X * tail).astype(BF16))

    return _pcall(
        body, (xbc, xbc, xbc, dt_c, a_c, a_r, d_full), name=name,
        out_shape=[jax.ShapeDtypeStruct((L, d_inner), F32), jax.ShapeDtypeStruct((G, nc, N, KP), F32)],
        grid=(G, nc),
        in_specs=_ssd_specs(K, KP, d_inner, False, nc),
        out_specs=[pl.BlockSpec((Q, KP), lambda g, c: (c, g)), pl.BlockSpec((None, None, N, KP), lambda g, c: (g, c, 0, 0))],
        scratch=[pltpu.VMEM((N, KP), F32)],
        sem=("parallel", "arbitrary"), comm=comm)


def _ssd_bwd(xbc, dt_c, a_c, a_r, d_full, states, dy, *, d_inner, name, comm=None):
    L = xbc.shape[0]
    G, N, Q, P = SSD_N_GROUPS, SSD_D_STATE, SSD_CHUNK, SSD_HEAD_DIM
    KP = d_inner // G
    K = KP // P
    nc = L // Q
    npair = KP // LANE

    def body(xs_ref, b_ref, c_ref, dtc_ref, ac_ref, ar_ref, df_ref, st_ref, dy_ref,
             dxs_ref, db_ref, dc_ref, da_ref, s1_ref, dd_ref, dS):
        @pl.when(pl.program_id(1) == 0)
        def _():
            dS[...] = jnp.zeros_like(dS)
            dd_ref[...] = jnp.zeros_like(dd_ref)

        lower, upper = _tri(Q)
        a_c, a_r = ac_ref[...], ar_ref[...]
        a_f, dt_f = _expand_heads(a_c, K), _expand_heads(dtc_ref[...], K)
        xs = xs_ref[...]
        Bm, Cm = b_ref[...], c_ref[...]
        Bb, Cb = Bm.astype(BF16), Cm.astype(BF16)
        dY = dy_ref[...]
        X = xs * dt_f
        ea = jnp.exp(a_f)
        alast = a_f[Q - 1:Q, :]
        tail = jnp.exp(alast - a_f)
        el = jnp.exp(alast)
        Sv, dSn = st_ref[...], dS[...]
        Sb, dSb = Sv.astype(BF16), dSn.astype(BF16)
        cb = _nt(Cb, Bb)
        cbT = _nt(Bb, Cb)
        yoff_raw = _nn(Cb, Sb)
        dYe = dY * ea
        dC = _nt(dYe.astype(BF16), Sb)
        dS[...] = dSn * el + _nn(Cm.T.astype(BF16), dYe.astype(BF16))
        Gx = _nn(Bb, dSb)
        dB = _nt((X * tail).astype(BF16), dSb)
        dtl = Gx * X * tail
        da_f = dYe * yoff_raw - dtl
        dalast_f = _csum(dtl) + _csum(dSn * Sv) * el
        da_c = _head_sum(da_f, K, KP)
        onek = lax.broadcasted_iota(jnp.int32, (1, K), 1)
        masks = _half_masks()
        dcb = jnp.zeros((Q, Q), F32)
        dcbT = jnp.zeros((Q, Q), F32)
        dX_parts = []
        for pr in range(npair):
            Xp = X[:, pr * LANE:(pr + 1) * LANE]
            dYp = dY[:, pr * LANE:(pr + 1) * LANE]
            dXp = Gx[:, pr * LANE:(pr + 1) * LANE] * tail[:, pr * LANE:(pr + 1) * LANE]
            for hh in range(2):
                k = 2 * pr + hh
                Xk = (Xp * masks[hh]).astype(BF16)
                dYk = (dYp * masks[hh]).astype(BF16)
                seg = a_c[:, k:k + 1] - a_r[k:k + 1, :]
                dec = jnp.where(lower, jnp.exp(jnp.minimum(seg, 0.0)), 0.0)
                decT = jnp.where(upper, jnp.exp(jnp.minimum(-seg, 0.0)), 0.0)
                dM = _nt(dYk, Xk)
                dMT = _nt(Xk, dYk)
                MT = cbT * decT
                dcb = dcb + dM * dec
                dcbT = dcbT + dMT * decT
                da_k = jnp.sum(dM * cb * dec, axis=1, keepdims=True) - jnp.sum(dMT * MT, axis=1, keepdims=True)
                da_c = da_c + da_k * (onek == k).astype(F32)
                dXp = dXp + _nn(MT.astype(BF16), dYk)
            dX_parts.append(dXp)
        dX = jnp.concatenate(dX_parts, axis=1) if npair > 1 else dX_parts[0]
        dC = dC + _nn(dcb.astype(BF16), Bb)
        dB = dB + _nn(dcbT.astype(BF16), Cb)
        lastrow = (lax.broadcasted_iota(jnp.int32, (Q, 1), 0) == Q - 1).astype(F32)
        da_ref[...] = da_c + lastrow * _head_sum(dalast_f, K, KP)
        s1_ref[...] = _head_sum(dX * xs, K, KP)
        dd_ref[...] += _csum(dY * xs)
        dxs_ref[...] = dX * dt_f + dY * df_ref[...]
        db_ref[...] = dB
        dc_ref[...] = dC

    rc = lambda c: nc - 1 - c
    tok = jax.ShapeDtypeStruct((G, L, K), F32)
    tok_spec = pl.BlockSpec((None, Q, K), lambda g, c: (g, rc(c), 0))
    return _pcall(
        body, (xbc, xbc, xbc, dt_c, a_c, a_r, d_full, states, dy), name=name,
        out_shape=[jax.ShapeDtypeStruct((L, d_inner), F32), jax.ShapeDtypeStruct((L, G * N), F32),
                   jax.ShapeDtypeStruct((L, G * N), F32), tok, tok, jax.ShapeDtypeStruct((1, d_inner), F32)],
        grid=(G, nc),
        in_specs=_ssd_specs(K, KP, d_inner, True, nc) + [
            pl.BlockSpec((None, None, N, KP), lambda g, c: (g, rc(c), 0, 0)),
            pl.BlockSpec((Q, KP), lambda g, c: (rc(c), g))],
        out_specs=[pl.BlockSpec((Q, KP), lambda g, c: (rc(c), g)), pl.BlockSpec((Q, N), lambda g, c: (rc(c), g)),
                   pl.BlockSpec((Q, N), lambda g, c: (rc(c), g)), tok_spec, tok_spec,
                   pl.BlockSpec((1, KP), lambda g, c: (0, g))],
        scratch=[pltpu.VMEM((N, KP), F32)],
        sem=("parallel", "arbitrary"), comm=comm)


def _slopes():
    n = DIL_N_GROUPS * DIL_HEADS
    s = 2.0 ** (-8.0 * np.arange(1, n + 1) / n)
    return s.reshape(DIL_N_GROUPS, DIL_HEADS).astype(np.float32)


def _attn_scores(qh, kh, slope_d, cur, valid_blk, transposed):
    B = DIL_BLOCK
    scale = DIL_HEAD_DIM ** -0.5
    if transposed:
        s = _nt(kh, qh) * scale
        kj = lax.broadcasted_iota(jnp.int32, (B, B), 0)
        qi = lax.broadcasted_iota(jnp.int32, (B, B), 1)
    else:
        s = _nt(qh, kh) * scale
        qi = lax.broadcasted_iota(jnp.int32, (B, B), 0)
        kj = lax.broadcasted_iota(jnp.int32, (B, B), 1)
    if cur:
        delta = qi - kj
        ok = kj <= qi
    else:
        delta = qi + B - kj
        ok = (kj >= qi) & valid_blk
    return jnp.where(ok, s - slope_d * delta.astype(F32), NEG)


def _attn_fwd(q, kv, g, *, name):
    L = q.shape[0]
    window, d = DIL_PATTERNS[g]
    assert window // d == DIL_BLOCK and L % (d * DIL_BLOCK) == 0
    M, B, W = L // d, DIL_BLOCK, DIL_W
    nb = M // B
    slopes = _slopes()[g]
    qv = q.reshape(M, d * DIL_N_GROUPS * W)
    kvv = kv.reshape(M, d * 2 * DIL_N_GROUPS * W)
    prev = lambda m: jnp.maximum(m - 1, 0)

    def body(q_ref, kc_ref, kp_ref, vc_ref, vp_ref, o_ref, lse_ref):
        has_prev = pl.program_id(1) > 0
        lse = jnp.zeros((B, DIL_HEADS), F32)
        onek = lax.broadcasted_iota(jnp.int32, (1, DIL_HEADS), 1)
        for h in range(DIL_HEADS):
            sl = slice(h * DIL_HEAD_DIM, (h + 1) * DIL_HEAD_DIM)
            qh = q_ref[:, sl]
            sd = float(slopes[h]) * d
            sc = _attn_scores(qh, kc_ref[:, sl], sd, True, None, False)
            sp = _attn_scores(qh, kp_ref[:, sl], sd, False, has_prev, False)
            m = jnp.maximum(jnp.max(sc, axis=1, keepdims=True), jnp.max(sp, axis=1, keepdims=True))
            pc, pp = jnp.exp(sc - m), jnp.exp(sp - m)
            den = jnp.sum(pc, axis=1, keepdims=True) + jnp.sum(pp, axis=1, keepdims=True)
            o = _nn(pc.astype(BF16), vc_ref[:, sl]) + _nn(pp.astype(BF16), vp_ref[:, sl])
            o_ref[:, sl] = o / den
            lse = lse + (m + jnp.log(den)) * (onek == h).astype(F32)
        lse_ref[...] = lse

    o, lse = pl.pallas_call(
        body, name=name,
        out_shape=[jax.ShapeDtypeStruct((M, d * W), F32), jax.ShapeDtypeStruct((d, M, DIL_HEADS), F32)],
        grid=(d, nb),
        in_specs=[pl.BlockSpec((B, W), lambda r, m: (m, r * 3 + g)),
                  pl.BlockSpec((B, W), lambda r, m: (m, r * 6 + g)),
                  pl.BlockSpec((B, W), lambda r, m: (prev(m), r * 6 + g)),
                  pl.BlockSpec((B, W), lambda r, m: (m, r * 6 + 3 + g)),
                  pl.BlockSpec((B, W), lambda r, m: (prev(m), r * 6 + 3 + g))],
        out_specs=[pl.BlockSpec((B, W), lambda r, m: (m, r)),
                   pl.BlockSpec((None, B, DIL_HEADS), lambda r, m: (r, m, 0))],
        compiler_params=_cp(("parallel", "parallel")),
    )(qv, kvv, kvv, kvv, kvv)
    return o.reshape(L, W), lse


def _attn_bwd_q(q, kv, do, lse, dl, g, *, name):
    L = q.shape[0]
    _, d = DIL_PATTERNS[g]
    M, B, W = L // d, DIL_BLOCK, DIL_W
    nb = M // B
    slopes = _slopes()[g]
    scale = DIL_HEAD_DIM ** -0.5
    qv = q.reshape(M, d * DIL_N_GROUPS * W)
    kvv = kv.reshape(M, d * 2 * DIL_N_GROUPS * W)
    dov = do.reshape(M, d * W)
    prev = lambda m: jnp.maximum(m - 1, 0)

    def body(q_ref, kc_ref, kp_ref, vc_ref, vp_ref, do_ref, lse_ref, dl_ref, dq_ref):
        has_prev = pl.program_id(1) > 0
        lse_all, dl_all = lse_ref[...], dl_ref[...]
        for h in range(DIL_HEADS):
            sl = slice(h * DIL_HEAD_DIM, (h + 1) * DIL_HEAD_DIM)
            qh, doh = q_ref[:, sl], do_ref[:, sl]
            sd = float(slopes[h]) * d
            lse_h, dl_h = lse_all[:, h:h + 1], dl_all[:, h:h + 1]
            pc = jnp.exp(_attn_scores(qh, kc_ref[:, sl], sd, True, None, False) - lse_h)
            pp = jnp.exp(_attn_scores(qh, kp_ref[:, sl], sd, False, has_prev, False) - lse_h)
            dsc = pc * (_nt(doh, vc_ref[:, sl]) - dl_h)
            dsp = pp * (_nt(doh, vp_ref[:, sl]) - dl_h)
            dq = _nn(dsc.astype(BF16), kc_ref[:, sl]) + _nn(dsp.astype(BF16), kp_ref[:, sl])
            dq_ref[:, sl] = (dq * scale).astype(dq_ref.dtype)

    col = pl.BlockSpec((None, B, DIL_HEADS), lambda r, m: (r, m, 0))
    dq = pl.pallas_call(
        body, name=name,
        out_shape=jax.ShapeDtypeStruct((M, d * W), BF16),
        grid=(d, nb),
        in_specs=[pl.BlockSpec((B, W), lambda r, m: (m, r * 3 + g)),
                  pl.BlockSpec((B, W), lambda r, m: (m, r * 6 + g)),
                  pl.BlockSpec((B, W), lambda r, m: (prev(m), r * 6 + g)),
                  pl.BlockSpec((B, W), lambda r, m: (m, r * 6 + 3 + g)),
                  pl.BlockSpec((B, W), lambda r, m: (prev(m), r * 6 + 3 + g)),
                  pl.BlockSpec((B, W), lambda r, m: (m, r)), col, col],
        out_specs=pl.BlockSpec((B, W), lambda r, m: (m, r)),
        compiler_params=_cp(("parallel", "parallel")),
    )(qv, kvv, kvv, kvv, kvv, dov, lse, dl)
    return dq.reshape(L, W)


def _attn_bwd_kv(q, kv, do, lse_t, dl_t, g, *, name):
    L = q.shape[0]
    _, d = DIL_PATTERNS[g]
    M, B, W = L // d, DIL_BLOCK, DIL_W
    nb = M // B
    slopes = _slopes()[g]
    scale = DIL_HEAD_DIM ** -0.5
    qv = q.reshape(M, d * DIL_N_GROUPS * W)
    kvv = kv.reshape(M, d * 2 * DIL_N_GROUPS * W)
    dov = do.reshape(M, d * W)
    nxt = lambda m: jnp.minimum(m + 1, nb - 1)

    def body(k_ref, v_ref, qc_ref, qn_ref, doc_ref, don_ref, lsec_ref, lsen_ref, dlc_ref, dln_ref, dk_ref, dv_ref):
        has_next = pl.program_id(1) < nb - 1
        lsec, lsen, dlc, dln = lsec_ref[...], lsen_ref[...], dlc_ref[...], dln_ref[...]
        for h in range(DIL_HEADS):
            sl = slice(h * DIL_HEAD_DIM, (h + 1) * DIL_HEAD_DIM)
            kh, vh = k_ref[:, sl], v_ref[:, sl]
            sd = float(slopes[h]) * d
            ptc = jnp.exp(_attn_scores(qc_ref[:, sl], kh, sd, True, None, True) - lsec[h:h + 1, :])
            ptn = jnp.exp(_attn_scores(qn_ref[:, sl], kh, sd, False, has_next, True) - lsen[h:h + 1, :])
            dv = _nn(ptc.astype(BF16), doc_ref[:, sl]) + _nn(ptn.astype(BF16), don_ref[:, sl])
            dstc = ptc * (_nt(vh, doc_ref[:, sl]) - dlc[h:h + 1, :])
            dstn = ptn * (_nt(vh, don_ref[:, sl]) - dln[h:h + 1, :])
            dk = _nn(dstc.astype(BF16), qc_ref[:, sl]) + _nn(dstn.astype(BF16), qn_ref[:, sl])
            dk_ref[:, sl] = (dk * scale).astype(dk_ref.dtype)
            dv_ref[:, sl] = dv.astype(dv_ref.dtype)

    rowc = pl.BlockSpec((None, DIL_HEADS, B), lambda r, m: (r, 0, m))
    rown = pl.BlockSpec((None, DIL_HEADS, B), lambda r, m: (r, 0, nxt(m)))
    dk, dv = pl.pallas_call(
        body, name=name,
        out_shape=[jax.ShapeDtypeStruct((M, d * W), BF16), jax.ShapeDtypeStruct((M, d * W), BF16)],
        grid=(d, nb),
        in_specs=[pl.BlockSpec((B, W), lambda r, m: (m, r * 6 + g)),
                  pl.BlockSpec((B, W), lambda r, m: (m, r * 6 + 3 + g)),
                  pl.BlockSpec((B, W), lambda r, m: (m, r * 3 + g)),
                  pl.BlockSpec((B, W), lambda r, m: (nxt(m), r * 3 + g)),
                  pl.BlockSpec((B, W), lambda r, m: (m, r)),
                  pl.BlockSpec((B, W), lambda r, m: (nxt(m), r)),
                  rowc, rown, rowc, rown],
        out_specs=[pl.BlockSpec((B, W), lambda r, m: (m, r)), pl.BlockSpec((B, W), lambda r, m: (m, r))],
        compiler_params=_cp(("parallel", "parallel")),
    )(kvv, kvv, qv, qv, dov, dov, lse_t, lse_t, dl_t, dl_t)
    return dk.reshape(L, W), dv.reshape(L, W)


ATT_TB = 2048
KV_W = 2 * DIL_HEAD_DIM


def _tn(a, b):
    return lax.dot_general(a, b, (((0,), (0,)), ((), ())), preferred_element_type=F32)


def _slope_rows(g):
    return jnp.asarray(np.repeat(_slopes()[g][:, None], LANE, axis=1))


def _attn_bias(slope_row, d):
    B = DIL_BLOCK
    qi = lax.broadcasted_iota(jnp.int32, (B, B), 0)
    kj = lax.broadcasted_iota(jnp.int32, (B, B), 1)
    sd = slope_row * float(d)
    cur = jnp.where(kj <= qi, -(qi - kj).astype(F32) * sd, NEG)
    prv = jnp.where(kj >= qi, -(qi + B - kj).astype(F32) * sd, NEG)
    return cur, prv


def _sub_rows(j, r, d):
    base = j * DIL_BLOCK * d + r
    return pl.ds(base, DIL_BLOCK, stride=d) if d > 1 else pl.ds(base, DIL_BLOCK)


def _attn_geometry(L, g):
    window, d = DIL_PATTERNS[g]
    tb = min(ATT_TB, L)
    assert window // d == DIL_BLOCK and tb % (d * DIL_BLOCK) == 0 and L % tb == 0
    return d, tb, L // tb, tb // (d * DIL_BLOCK)


def _kv_specs(g, tb, nb):
    E, nh = DIL_HEAD_DIM, DIL_N_GROUPS * DIL_HEADS
    prev = lambda b: jnp.maximum(b - 1, 0)
    return [pl.BlockSpec((tb, E), lambda b, h: (b, g * DIL_HEADS + h)),
            pl.BlockSpec((tb, E), lambda b, h: (b, nh + g * DIL_HEADS + h)),
            pl.BlockSpec((tb, E), lambda b, h: (prev(b), g * DIL_HEADS + h)),
            pl.BlockSpec((tb, E), lambda b, h: (prev(b), nh + g * DIL_HEADS + h))]


def _attn_fwd(qz, kv, g, *, name):
    L = qz.shape[0]
    d, tb, nb, nj = _attn_geometry(L, g)
    B, E = DIL_BLOCK, DIL_HEAD_DIM
    scale = E ** -0.5

    def body(sl_ref, q_ref, kc_ref, vc_ref, kp_ref, vp_ref, o_ref, lse_ref):
        b, h = pl.program_id(0), pl.program_id(1)

        @pl.when(h == 0)
        def _():
            lse_ref[...] = jnp.zeros_like(lse_ref)

        bias_c, bias_p = _attn_bias(sl_ref[pl.ds(h, 1), :], d)
        bias_p0 = jnp.where(b > 0, bias_p, NEG)
        oneh = (lax.broadcasted_iota(jnp.int32, (1, LANE), 1) == h).astype(F32)
        for r in range(d):
            for j in range(nj):
                rows = _sub_rows(j, r, d)
                qs = q_ref[rows, :].astype(BF16)
                kc, vc = kc_ref[rows, :].astype(BF16), vc_ref[rows, :].astype(BF16)
                if j > 0:
                    prows, bp = _sub_rows(j - 1, r, d), bias_p
                    kq, vq = kc_ref[prows, :].astype(BF16), vc_ref[prows, :].astype(BF16)
                else:
                    prows, bp = _sub_rows(nj - 1, r, d), bias_p0
                    kq, vq = kp_ref[prows, :].astype(BF16), vp_ref[prows, :].astype(BF16)
                sc = _nt(qs, kc) * scale + bias_c
                sp = _nt(qs, kq) * scale + bp
                m = jnp.maximum(jnp.max(sc, axis=1, keepdims=True), jnp.max(sp, axis=1, keepdims=True))
                pc, pp = jnp.exp(sc - m), jnp.exp(sp - m)
                den = jnp.sum(pc, axis=1, keepdims=True) + jnp.sum(pp, axis=1, keepdims=True)
                o = _nn(pc.astype(BF16), vc) + _nn(pp.astype(BF16), vq)
                o_ref[rows, :] = o / den
                lse_ref[rows, :] = lse_ref[rows, :] + (m + jnp.log(den)) * oneh

    return _pcall(
        body, (_slope_rows(g), qz, kv, kv, kv, kv), name=name,
        out_shape=[jax.ShapeDtypeStruct((L, DIL_W), F32), jax.ShapeDtypeStruct((L, LANE), F32)],
        grid=(nb, DIL_HEADS),
        in_specs=[pl.BlockSpec((DIL_HEADS, LANE), lambda b, h: (0, 0)),
                  pl.BlockSpec((tb, E), lambda b, h: (b, g * DIL_HEADS + h))] + _kv_specs(g, tb, nb),
        out_specs=[pl.BlockSpec((tb, E), lambda b, h: (b, h)), pl.BlockSpec((tb, LANE), lambda b, h: (b, 0))],
        sem=("parallel", "arbitrary"))


def _attn_bwd(qz, kv, do, lse, dl, dqz, dk, dv, g, *, name):
    L = qz.shape[0]
    d, tb, nb, nj = _attn_geometry(L, g)
    B, E = DIL_BLOCK, DIL_HEAD_DIM
    scale = E ** -0.5
    nxt = lambda b: jnp.minimum(b + 1, nb - 1)
    fresh = dk is None

    def body(sl_ref, qc_ref, qn_ref, kc_ref, vc_ref, kp_ref, vp_ref, doc_ref, don_ref, lsec_ref, lsen_ref,
             dlc_ref, dln_ref, *rest):
        dq_ref, dk_ref, dv_ref = rest[-3:]
        b, h = pl.program_id(0), pl.program_id(1)
        bias_c, bias_p = _attn_bias(sl_ref[pl.ds(h, 1), :], d)
        bias_first = jnp.where(b > 0, bias_p, NEG)
        bias_last = jnp.where(b < nb - 1, bias_p, NEG)
        oneh = (lax.broadcasted_iota(jnp.int32, (1, LANE), 1) == h).astype(F32)

        def col(ref, rows):
            return jnp.sum(ref[rows, :] * oneh, axis=1, keepdims=True)

        def pair(q, do_, lse_, dl_, k_, v_, bias):
            p = jnp.exp(_nt(q, k_) * scale + bias - lse_)
            ds = p * (_nt(do_, v_) - dl_)
            return p.astype(BF16), ds.astype(BF16)

        for r in range(d):
            rows0 = _sub_rows(0, r, d)
            qj, doj = qc_ref[rows0, :].astype(BF16), doc_ref[rows0, :].astype(BF16)
            lsej, dlj = col(lsec_ref, rows0), col(dlc_ref, rows0)
            prows = _sub_rows(nj - 1, r, d)
            kq, vq = kp_ref[prows, :].astype(BF16), vp_ref[prows, :].astype(BF16)
            _, ds = pair(qj, doj, lsej, dlj, kq, vq, bias_first)
            dq_carry = _nn(ds, kq)
            for j in range(nj):
                rows = _sub_rows(j, r, d)
                kj, vj = kc_ref[rows, :].astype(BF16), vc_ref[rows, :].astype(BF16)
                p, ds = pair(qj, doj, lsej, dlj, kj, vj, bias_c)
                dq_ref[rows, :] = (dq_carry + _nn(ds, kj)) * scale
                dkj, dvj = _tn(ds, qj), _tn(p, doj)
                if j < nj - 1:
                    nrows = _sub_rows(j + 1, r, d)
                    qn, don = qc_ref[nrows, :].astype(BF16), doc_ref[nrows, :].astype(BF16)
                    lsen, dln, bias = col(lsec_ref, nrows), col(dlc_ref, nrows), bias_p
                else:
                    qn, don = qn_ref[rows0, :].astype(BF16), don_ref[rows0, :].astype(BF16)
                    lsen, dln, bias = col(lsen_ref, rows0), col(dln_ref, rows0), bias_last
                p2, ds2 = pair(qn, don, lsen, dln, kj, vj, bias)
                dk_ref[rows, :] = (dkj + _tn(ds2, qn)) * scale
                dv_ref[rows, :] = dvj + _tn(p2, don)
                dq_carry = _nn(ds2, kj)
                qj, doj, lsej, dlj = qn, don, lsen, dln

    hb = lambda b, h: (b, g * DIL_HEADS + h)
    anyspec = pl.BlockSpec(memory_space=pl.ANY)
    args = [_slope_rows(g), qz, qz, kv, kv, kv, kv, do, do, lse, lse, dl, dl, dqz] + ([] if fresh else [dk, dv])
    in_specs = ([pl.BlockSpec((DIL_HEADS, LANE), lambda b, h: (0, 0)),
                 pl.BlockSpec((tb, E), hb), pl.BlockSpec((tb, E), lambda b, h: (nxt(b), g * DIL_HEADS + h))]
                + _kv_specs(g, tb, nb)
                + [pl.BlockSpec((tb, E), lambda b, h: (b, h)), pl.BlockSpec((tb, E), lambda b, h: (nxt(b), h)),
                   pl.BlockSpec((tb, LANE), lambda b, h: (b, 0)), pl.BlockSpec((tb, LANE), lambda b, h: (nxt(b), 0)),
                   pl.BlockSpec((tb, LANE), lambda b, h: (b, 0)), pl.BlockSpec((tb, LANE), lambda b, h: (nxt(b), 0)),
                   anyspec] + ([] if fresh else [anyspec, anyspec]))
    aliases = {13: 0} if fresh else {13: 0, 14: 1, 15: 2}
    dkv_sd = jax.ShapeDtypeStruct((L, DIL_N_GROUPS * DIL_W), F32)
    return pl.pallas_call(
        body, name=name,
        out_shape=[jax.ShapeDtypeStruct(dqz.shape, F32), dkv_sd, dkv_sd],
        grid=(nb, DIL_HEADS), in_specs=in_specs,
        out_specs=[pl.BlockSpec((tb, E), hb), pl.BlockSpec((tb, E), hb), pl.BlockSpec((tb, E), hb)],
        input_output_aliases=aliases,
        compiler_params=_cp(("parallel", "parallel")))(*args)


def _head_expand():
    r = lax.broadcasted_iota(jnp.int32, (LANE, DIL_W), 0)
    c = lax.broadcasted_iota(jnp.int32, (LANE, DIL_W), 1)
    E = ((c >= r * DIL_HEAD_DIM) & (c < (r + 1) * DIL_HEAD_DIM)).astype(F32)
    r2 = lax.broadcasted_iota(jnp.int32, (DIL_W, LANE), 0)
    c2 = lax.broadcasted_iota(jnp.int32, (DIL_W, LANE), 1)
    Et = ((r2 >= c2 * DIL_HEAD_DIM) & (r2 < (c2 + 1) * DIL_HEAD_DIM)).astype(F32)
    return E, Et


def _merge_weights(l0, l1, l2):
    m = jnp.maximum(jnp.maximum(l0, l1), l2)
    e = [jnp.exp(l - m) for l in (l0, l1, l2)]
    tot = e[0] + e[1] + e[2]
    return [v / tot for v in e]


def _merge_fwd(os_, lses, qz, z_off, *, name):
    def fn(o0, o1, o2, l0, l1, l2, z):
        E, _ = _head_expand()
        w = _merge_weights(l0, l1, l2)
        om = sum(_hi(wg, E) * og for wg, og in zip(w, (o0, o1, o2)))
        return [om * _silu(z)], []

    rows = [(o, 0, DIL_W) for o in os_] + [(l, 0, LANE) for l in lses] + [(qz, z_off, DIL_W)]
    return _rowmap(fn, rows, [], [(DIL_W, BF16)], [], name=name)[0]


def _merge_bwd(os_, lses, qz, z_off, dog, *, name, comm=None):
    def fn(o0, o1, o2, l0, l1, l2, z, dg):
        E, Et = _head_expand()
        dg = dg.astype(F32)
        w = _merge_weights(l0, l1, l2)
        wf = [_hi(wg, E) for wg in w]
        os3 = (o0, o1, o2)
        om = sum(a * b for a, b in zip(wf, os3))
        dom = dg * _silu(z)
        dz = dg * om * _dsilu(z)
        dw = [_hi(dom * og, Et) for og in os3]
        tot = sum(a * b for a, b in zip(w, dw))
        return [wf[0] * dom, wf[1] * dom, wf[2] * dom, w[0] * tot, w[1] * tot, w[2] * tot, dz], []

    rows = ([(o, 0, DIL_W) for o in os_] + [(l, 0, LANE) for l in lses] + [(qz, z_off, DIL_W), (dog, 0, DIL_W)])
    outs = [(DIL_W, F32)] * 3 + [(LANE, F32)] * 3 + [(qz.shape[1], F32, z_off, DIL_W)]
    return _rowmap(fn, rows, [], outs, [], name=name, comm=comm)


def _adamw(gparts, w, m, v, *, name, tr=128):
    n, R, C = gparts.shape
    tr = _pick(R, tr)
    c1 = 1.0 - ADAM_B1 ** ADAM_STEP
    c2 = 1.0 - ADAM_B2 ** ADAM_STEP

    def body(g_ref, w_ref, m_ref, v_ref, go_ref, d_ref, mo_ref, vo_ref):
        g = g_ref[0].astype(F32)
        for i in range(1, n):
            g = g + g_ref[i].astype(F32)
        mn = ADAM_B1 * m_ref[...] + (1.0 - ADAM_B1) * g
        vn = ADAM_B2 * v_ref[...] + (1.0 - ADAM_B2) * jnp.square(g)
        d_ref[...] = -ADAM_LR * ((mn / c1) / (jnp.sqrt(vn / c2) + ADAM_EPS) + ADAM_WD * w_ref[...])
        go_ref[...] = g
        mo_ref[...] = mn
        vo_ref[...] = vn

    blk = pl.BlockSpec((tr, C), lambda i: (i, 0))
    sd = jax.ShapeDtypeStruct((R, C), F32)
    return pl.pallas_call(
        body, name=name, out_shape=[sd, sd, sd, sd], grid=(R // tr,),
        in_specs=[pl.BlockSpec((n, tr, C), lambda i: (0, i, 0)), blk, blk, blk],
        out_specs=[blk, blk, blk, blk],
        compiler_params=_cp(("parallel",)),
    )(gparts, w, m, v)


def _sum_parts(parts, *, name):
    n, R, C = parts.shape

    def body(p_ref, o_ref):
        s = p_ref[0]
        for i in range(1, n):
            s = s + p_ref[i]
        o_ref[...] = s

    return pl.pallas_call(
        body, name=name, out_shape=jax.ShapeDtypeStruct((R, C), F32),
        in_specs=[pl.BlockSpec(memory_space=pltpu.VMEM)], out_specs=pl.BlockSpec(memory_space=pltpu.VMEM),
    )(parts)


def _cols_from(g):
    _, R, Cs = g.shape
    return jnp.transpose(g, (1, 0, 2)).reshape(R, N_DEV * Cs)


def _col_parts(dw):
    R, C = dw.shape
    return jnp.transpose(dw.reshape(R, N_DEV, C // N_DEV), (1, 0, 2))


def _ag_cols(w_loc):
    if w_loc.shape[1] % LANE == 0:
        return _ag_comm(w_loc, cols=True), (lambda g: g)
    return _ag_comm(w_loc), _cols_from


def _rs_cols(dw):
    if (dw.shape[1] // N_DEV) % LANE == 0:
        return _a2a_comm(dw, cols=True)
    return _a2a_comm(_col_parts(dw))


def kernel(x, c, ada_w, ada_b, ln_g, ln_b, a_in_w, a_conv_w, a_conv_b, a_dt_bias, a_A_log, a_D, a_norm_g, a_out_w, kv_w, b_in_w, b_out_w, loss_target, m_ada_w, m_ada_b, m_ln_g, m_ln_b, m_a_in_w, m_a_conv_w, m_a_conv_b, m_a_dt_bias, m_a_A_log, m_a_D, m_a_norm_g, m_a_out_w, m_kv_w, m_b_in_w, m_b_out_w, v_ada_w, v_ada_b, v_ln_g, v_ln_b, v_a_in_w, v_a_conv_w, v_a_conv_b, v_a_dt_bias, v_a_A_log, v_a_D, v_a_norm_g, v_a_out_w, v_kv_w, v_b_in_w, v_b_out_w):
    L, D = x.shape[1], x.shape[2]
    H = a_dt_bias.shape[1]
    d_inner = H * SSD_HEAD_DIM
    G, N, P = SSD_N_GROUPS, SSD_D_STATE, SSD_HEAD_DIM
    K = H // G
    KP = K * P
    conv_dim = d_inner + 2 * G * N
    in_dim = d_inner + conv_dim + H
    in_pad = d_inner + conv_dim + LANE
    assert H <= LANE and KP % LANE == 0 and L % SSD_CHUNK == 0
    me = 4 * lax.axis_index("x") + 2 * lax.axis_index("y") + lax.axis_index("c")
    x2d, tgt = x[0], loss_target[0]

    c_all = _all_gather(c, "ag_c").reshape(N_DEV, D)
    mods = []
    for l in range(DEPTH):
        ab = lax.dynamic_slice(ada_b[l], (me * (3 * D // N_DEV),), (3 * D // N_DEV,))[None]
        mods.append(_matmul(c_all, ada_w[l], name=f"mod{l}", exact=True, a_silu=True, bias=ab))
    mod_all = _all_gather(jnp.stack(mods), "ag_mod")
    mod_me = lax.dynamic_index_in_dim(jnp.transpose(mod_all, (2, 1, 0, 3)).reshape(N_DEV, DEPTH, 3 * D), me, 0, False)
    shift = [mod_me[l, None, 0:D] for l in range(DEPTH)]
    scale = [mod_me[l, None, D:2 * D] for l in range(DEPTH)]
    gate = [mod_me[l, None, 2 * D:3 * D] for l in range(DEPTH)]

    w_in = _cols_from(_all_gather(a_in_w[0].astype(BF16), "ag_a_in"))
    w_in = jnp.pad(w_in, ((0, 0), (0, in_pad - in_dim)))
    conv_w = _all_gather(a_conv_w[0], "ag_conv_w")
    conv_w = jnp.transpose(conv_w, (1, 0, 2)).reshape(SSD_CONV_W, conv_dim)
    conv_b = _all_gather(a_conv_b, "ag_conv_b").reshape(1, conv_dim)
    norm_g = _all_gather(a_norm_g, "ag_norm_g").reshape(1, d_inner)

    def modulate(xin, l, name):
        fn = lambda xv, sc, sh: ([xv * (1.0 + sc) + sh], [])
        return _rowmap(fn, [(xin, 0, D)], [scale[l], shift[l]], [(D, BF16)], [], name=name)[0]

    def ln_fwd(xin, y, l, name):
        def fn(xv, yv, gt, g, b):
            u = DEEPNORM_ALPHA * xv + (1.0 + gt) * yv
            mu = jnp.mean(u, axis=1, keepdims=True)
            uc = u - mu
            var = jnp.mean(uc * uc, axis=1, keepdims=True)
            o = uc * lax.rsqrt(var + LN_EPS) * g + b
            return [o, o], []
        return _rowmap(fn, [(xin, 0, D), (y, 0, D)], [gate[l], ln_g[l:l + 1], ln_b[l:l + 1]],
                       [(D, F32), (D, BF16)], [], name=name)

    def ln_bwd(xin, y, dout, l, name):
        def fn(xv, yv, do, gt, g, b):
            u = DEEPNORM_ALPHA * xv + (1.0 + gt) * yv
            mu = jnp.mean(u, axis=1, keepdims=True)
            uc = u - mu
            var = jnp.mean(uc * uc, axis=1, keepdims=True)
            rs = lax.rsqrt(var + LN_EPS)
            xh = uc * rs
            dxh = do * g
            du = rs * (dxh - jnp.mean(dxh, axis=1, keepdims=True) - xh * jnp.mean(dxh * xh, axis=1, keepdims=True))
            return [DEEPNORM_ALPHA * du, (1.0 + gt) * du], [_csum(du * yv), _csum(do * xh), _csum(do)]
        return _rowmap(fn, [(xin, 0, D), (y, 0, D), (dout, 0, D)], [gate[l], ln_g[l:l + 1], ln_b[l:l + 1]],
                       [(D, F32), (D, BF16)], [(1, D)] * 3, name=name)

    def mod_bwd(xin, dh, dx_acc, l, name):
        def fn(xv, dhv, dxa, sc):
            return [dxa + dhv * (1.0 + sc)], [_csum(dhv * xv), _csum(dhv)]
        return _rowmap(fn, [(xin, 0, D), (dh, 0, D), (dx_acc, 0, D)], [scale[l]], [(D, F32)], [(1, D)] * 2, name=name)

    h0 = modulate(x2d, 0, "mod_h0")
    proj, g_aout = _matmul(h0, w_in, name="mm_a_in", tn=1152,
                           comm=_ag_comm(a_out_w[0].astype(BF16)))
    w_aout = g_aout.reshape(d_inner, D)
    xbc = _conv_fwd(proj, d_inner, conv_dim, conv_w, conv_b, name="conv_fwd")
    dt_raw = proj[:, d_inner + conv_dim:]
    padh = lambda a: jnp.pad(a, ((0, 0), (0, LANE - H)))
    bias_p, alog_p = padh(a_dt_bias), padh(a_A_log)
    dt_p, a_p = _ssd_prep(dt_raw, bias_p, alog_p, name="ssd_prep")
    dt_c = jnp.transpose(dt_p[:, :H].reshape(L, G, K), (1, 0, 2))
    a_c = jnp.transpose(a_p[:, :H].reshape(L, G, K), (1, 0, 2))
    a_r = jnp.transpose(a_c, (0, 2, 1))
    d_full = jnp.repeat(a_D.reshape(H), P)[None]
    ssd_in = (xbc, dt_c, a_c, a_r, d_full)
    cm_kv, fix_kv = _ag_cols(kv_w.astype(BF16))
    y_ssd, states, w_kv = _ssd_fwd(*ssd_in, d_inner=d_inner, name="ssd_fwd", comm=cm_kv)
    w_kv = fix_kv(w_kv)

    gw = d_inner // G

    def gnorm_fn(yv, zv, g):
        yg = yv * _silu(zv)
        r = lax.rsqrt(jnp.mean(yg * yg, axis=1, keepdims=True) + RMS_EPS)
        return [yg * r * g], []
    yn = _rowmap(gnorm_fn, [(y_ssd, 0, d_inner), (proj, 0, d_inner)], [norm_g], [(d_inner, BF16)], [],
                 name="gnorm_fwd", cw=gw, tr=1024)[0]
    cm_bin, fix_bin = _ag_cols(b_in_w[0].astype(BF16))
    ya, w_bin = _matmul(yn, w_aout, name="mm_a_out", comm=cm_bin)
    w_bin = fix_bin(w_bin)
    x1, x1b = ln_fwd(x2d, ya, 0, "ln0_fwd")

    cm_bout, fix_bout = _ag_cols(b_out_w[0].astype(BF16))
    kv, w_bout = _matmul(x1b, w_kv, name="mm_kv", comm=cm_bout)
    w_bout = fix_bout(w_bout)
    h1 = modulate(x1, 1, "mod_h1")
    qz = _matmul(h1, w_bin, name="mm_b_in")
    z_off = DIL_N_GROUPS * DIL_W
    os_, lses = [], []
    for g in range(DIL_N_GROUPS):
        o, lse = _attn_fwd(qz, kv, g, name=f"attn_fwd{g}")
        os_.append(o)
        lses.append(lse)
    og = _merge_fwd(os_, lses, qz, z_off, name="merge_fwd")
    yb = _matmul(og, w_bout, name="mm_b_out")
    x2, _ = ln_fwd(x1, yb, 1, "ln1_fwd")

    def loss_fn(xv, tv):
        e = xv - tv
        return [e * (1.0 / D)], [_csum(e * e) * (0.5 / D)]
    dx2, loss_cols = _rowmap(loss_fn, [(x2, 0, D), (tgt, 0, D)], [], [(D, F32)], [(1, D)], name="loss")
    loss = lax.psum(jnp.sum(loss_cols), ("x", "y", "c"))

    dx1a, dyb, dgate1, dlng1, dlnb1 = ln_bwd(x1, yb, dx2, 1, "ln1_bwd")
    dw_bout = _matmul(og, dyb, name="mm_b_out_dw", ta=True, out_dtype=BF16)
    dog = _matmul(dyb, w_bout, name="mm_b_out_dx", tb=True, out_dtype=BF16)
    do0, do1, do2, dl0, dl1, dl2, dqz, r_bout = _merge_bwd(os_, lses, qz, z_off, dog, name="merge_bwd",
                                                           comm=_rs_cols(dw_bout))
    dk = dv = None
    for g, (do_g, dl_g) in enumerate(zip((do0, do1, do2), (dl0, dl1, dl2))):
        dqz, dk, dv = _attn_bwd(qz, kv, do_g, lses[g], dl_g, dqz, dk, dv, g, name=f"attn_bwd{g}")
    dw_bin = _matmul(h1, dqz, name="mm_b_in_dw", ta=True, out_dtype=BF16)
    dh1 = _matmul(dqz, w_bin, name="mm_b_in_dx", tb=True)
    dx1b, dscale1, dshift1 = mod_bwd(x1, dh1, dx1a, 1, "mod1_bwd")
    kw = DIL_N_GROUPS * DIL_W
    dw_kv = jnp.concatenate([_matmul(x1b, dk, name="mm_k_dw", ta=True, out_dtype=BF16),
                             _matmul(x1b, dv, name="mm_v_dw", ta=True, out_dtype=BF16)], axis=1)
    dx1k = _matmul(dk, w_kv[:, :kw], name="mm_k_dx", tb=True, bias=dx1b)
    dx1 = _matmul(dv, w_kv[:, kw:], name="mm_v_dx", tb=True, bias=dx1k)

    dxa, dya, dgate0, dlng0, dlnb0 = ln_bwd(x2d, ya, dx1, 0, "ln0_bwd")
    dw_aout = _matmul(yn, dya, name="mm_a_out_dw", ta=True, out_dtype=BF16)
    dyn = _matmul(dya, w_aout, name="mm_a_out_dx", tb=True)

    def gnorm_bwd_fn(yv, zv, dn, g):
        sz = _silu(zv)
        yg = yv * sz
        r = lax.rsqrt(jnp.mean(yg * yg, axis=1, keepdims=True) + RMS_EPS)
        nrm = yg * r
        dnn = dn * g
        dyg = r * (dnn - nrm * jnp.mean(dnn * nrm, axis=1, keepdims=True))
        return [dyg * sz, dyg * yv * _dsilu(zv)], [_csum(dn * nrm)]
    dy_ssd, dz, dnorm_g = _rowmap(gnorm_bwd_fn, [(y_ssd, 0, d_inner), (proj, 0, d_inner), (dyn, 0, d_inner)],
                                  [norm_g], [(d_inner, F32), (d_inner, BF16)], [(1, d_inner)], name="gnorm_bwd", cw=gw,
                                  tr=1024)
    dxs, dB, dC, da_t, s1_t, dD_f, r_kv, r_aout = _ssd_bwd(
        *ssd_in, states, dy_ssd, d_inner=d_inner, name="ssd_bwd",
        comm=_comm_join([_rs_cols(dw_kv), _a2a_comm(dw_aout.reshape(N_DEV, d_inner // N_DEV, D))]))
    dxbc_raw, dconv_w, dconv_b, r_bin = _conv_bwd(proj, d_inner, conv_dim, conv_w, conv_b, (dxs, dB, dC),
                                                  name="conv_bwd", tc=128, comm=_rs_cols(dw_bin))
    tokp = lambda t: padh(jnp.transpose(t, (1, 0, 2)).reshape(L, H))
    ddt_raw, ddt_bias_p, dA_log_p = _ssd_post(tokp(da_t), tokp(s1_t), dt_p, dt_raw, bias_p, alog_p, name="ssd_post")
    ddt_bias, dA_log = ddt_bias_p[:, :H], dA_log_p[:, :H]
    dD = jnp.sum(dD_f.reshape(H, P), axis=1)[None]
    dproj = jnp.concatenate([dz, dxbc_raw, ddt_raw], axis=1)
    dw_in = _matmul(h0, dproj, name="mm_a_in_dw", ta=True, out_dtype=BF16, tn=1152)[:, :in_dim]
    cs_in = in_dim // N_DEV
    by_c = jnp.transpose(dw_in.reshape(D, N_DEV // 2, 2, cs_in), (2, 1, 0, 3))
    my_c = lax.axis_index("c")
    keep = lax.dynamic_index_in_dim(by_c, my_c, 0, False)
    give = lax.dynamic_index_in_dim(by_c, 1 - my_c, 0, False)
    got = _run_comm(_pair_comm(give), "rs_a_in_pair")[0]
    pair_sum = _rowmap(lambda a, b: ([a.astype(F32) + b.astype(F32)], []),
                       [(keep.reshape(-1, cs_in), 0, cs_in), (got.reshape(-1, cs_in), 0, cs_in)], [],
                       [(cs_in, BF16)], [], name="rs_a_in_add", tr=512)[0].reshape(N_DEV // 2, D, cs_in)
    dh0, r_in = _matmul(dproj, w_in, name="mm_a_in_dx", tb=True, tk=1152,
                        comm=_quad_comm(pair_sum))
    grad_x, dscale0, dshift0 = mod_bwd(x2d, dh0, dxa, 0, "mod0_bwd")

    dmod = jnp.concatenate([dshift0, dscale0, dgate0, dshift1, dscale1, dgate1], axis=1)
    pieces = [dmod, dlng0, dlng1, dlnb0, dlnb1, ddt_bias, dA_log, dD,
              dconv_w.reshape(1, -1), dconv_b, dnorm_g]
    sizes = [p.shape[1] for p in pieces]
    tot = sum(sizes)
    tot_pad = -(-tot // (8 * LANE)) * (8 * LANE)
    packed = jnp.pad(jnp.concatenate(pieces, axis=1), ((0, 0), (0, tot_pad - tot))).reshape(tot_pad // LANE, LANE)
    packed_all = _all_gather(packed, "ag_small")
    small = _sum_parts(packed_all, name="sum_small").reshape(tot_pad)
    offs = np.cumsum([0] + sizes)
    seg = lambda i: small[int(offs[i]):int(offs[i + 1])]
    g_ada_b = seg(0).reshape(DEPTH, 3 * D)
    g_ln_g = jnp.stack([seg(1), seg(2)])
    g_ln_b = jnp.stack([seg(3), seg(4)])
    g_dt_bias, g_A_log, g_D = seg(5)[None], seg(6)[None], seg(7)[None]
    cs = conv_dim // N_DEV
    g_conv_w = lax.dynamic_slice(seg(8).reshape(SSD_CONV_W, conv_dim), (0, me * cs), (SSD_CONV_W, cs))[None]
    g_conv_b = lax.dynamic_slice(seg(9), (me * cs,), (cs,))[None]
    ns = d_inner // N_DEV
    g_norm_g = lax.dynamic_slice(seg(10), (me * ns,), (ns,))[None]

    ms = 3 * D // N_DEV
    dmod_all = packed_all.reshape(N_DEV, tot_pad)[:, :DEPTH * 3 * D].reshape(N_DEV, DEPTH, 3 * D)
    dmod_cols = lax.dynamic_slice(dmod_all, (0, 0, me * ms), (N_DEV, DEPTH, ms))
    c_t = jnp.transpose(c_all)
    g_ada_w = jnp.stack([_matmul(c_t, dmod_cols[:, l], name=f"mm_ada_dw{l}", exact=True, a_silu=True)
                         for l in range(DEPTH)])[None]

    def upd(parts, w, m, v, name):
        shp = w.shape
        r2 = lambda a: a.reshape(-1, shp[-1])
        return [o.reshape(shp) for o in _adamw(parts, r2(w), r2(m), r2(v), name=name)]

    res = {}
    res["ada_w"] = upd(g_ada_w.reshape(1, -1, ms), ada_w, m_ada_w, v_ada_w, "adam_ada_w")
    res["a_in_w"] = upd(r_in, a_in_w, m_a_in_w, v_a_in_w, "adam_a_in")
    res["a_out_w"] = upd(r_aout, a_out_w, m_a_out_w, v_a_out_w, "adam_a_out")
    res["kv_w"] = upd(r_kv, kv_w, m_kv_w, v_kv_w, "adam_kv")
    res["b_in_w"] = upd(r_bin, b_in_w, m_b_in_w, v_b_in_w, "adam_b_in")
    res["b_out_w"] = upd(r_bout, b_out_w, m_b_out_w, v_b_out_w, "adam_b_out")

    small_names = ["ada_b", "ln_g", "ln_b", "a_conv_w", "a_conv_b", "a_dt_bias", "a_A_log", "a_D", "a_norm_g"]
    small_g = [g_ada_b, g_ln_g, g_ln_b, g_conv_w, g_conv_b, g_dt_bias, g_A_log, g_D, g_norm_g]
    small_w = [ada_b, ln_g, ln_b, a_conv_w, a_conv_b, a_dt_bias, a_A_log, a_D, a_norm_g]
    small_m = [m_ada_b, m_ln_g, m_ln_b, m_a_conv_w, m_a_conv_b, m_a_dt_bias, m_a_A_log, m_a_D, m_a_norm_g]
    small_v = [v_ada_b, v_ln_g, v_ln_b, v_a_conv_w, v_a_conv_b, v_a_dt_bias, v_a_A_log, v_a_D, v_a_norm_g]
    ssz = [int(np.prod(w.shape)) for w in small_w]
    stot = sum(ssz)
    spad = -(-stot // (8 * LANE)) * (8 * LANE)

    def pack(arrs, fill):
        flat = jnp.concatenate([a.reshape(-1) for a in arrs])
        return jnp.concatenate([flat, jnp.full((spad - stot,), fill, F32)]).reshape(spad // LANE, LANE)

    sres = _adamw(pack(small_g, 0.0)[None], pack(small_w, 0.0), pack(small_m, 0.0), pack(small_v, 1.0), name="adam_small")
    soffs = np.cumsum([0] + ssz)
    for i, nme in enumerate(small_names):
        res[nme] = [r.reshape(-1)[int(soffs[i]):int(soffs[i + 1])].reshape(small_w[i].shape) for r in sres]

    order = ["ada_w", "ada_b", "ln_g", "ln_b", "a_in_w", "a_conv_w", "a_conv_b", "a_dt_bias", "a_A_log", "a_D",
             "a_norm_g", "a_out_w", "kv_w", "b_in_w", "b_out_w"]
    outs = [loss, grad_x[None]]
    for j in range(4):
        outs += [res[nme][j] for nme in order]
    return tuple(outs)
```

```python
import functools
import math

import numpy as np
import jax
import jax.numpy as jnp
from jax import lax
from jax.experimental import pallas as pl
from jax.experimental.pallas import tpu as pltpu

F32, BF16 = jnp.float32, jnp.bfloat16
HI = lax.Precision.HIGHEST
MESH = pl.DeviceIdType.MESH
N_DEV = 8

SSD_HEAD_DIM = 64
SSD_N_GROUPS = 8
SSD_D_STATE = 128
SSD_CONV_W = 4
SSD_CHUNK = 256
DIL_PATTERNS = ((128, 1), (512, 4), (2048, 16))
DIL_N_GROUPS = 3
DIL_HEADS = 8
DIL_HEAD_DIM = 128
DIL_BLOCK = 128
DIL_W = DIL_HEADS * DIL_HEAD_DIM
DEPTH = 2
DEEPNORM_ALPHA = (2 * DEPTH) ** 0.25
LN_EPS = 1e-5
RMS_EPS = 1e-5
ADAM_LR, ADAM_B1, ADAM_B2, ADAM_EPS, ADAM_WD, ADAM_STEP = 0.001, 0.9, 0.999, 1e-08, 0.01, 10
LANE = 128
NEG = -1e30
VMEM_LIMIT = 56 * 1024 * 1024


def _cp(sem=None):
    return pltpu.CompilerParams(dimension_semantics=sem, vmem_limit_bytes=VMEM_LIMIT)


def _silu(x):
    return x * jax.nn.sigmoid(x)


def _dsilu(x):
    s = jax.nn.sigmoid(x)
    return s * (1.0 + x * (1.0 - s))


def _softplus(x):
    return jnp.maximum(x, 0.0) + jnp.log(1.0 + jnp.exp(-jnp.abs(x)))


def _nt(a, b):
    return lax.dot_general(a, b, (((1,), (1,)), ((), ())), preferred_element_type=F32)


def _nn(a, b):
    return jnp.dot(a, b, preferred_element_type=F32)


def _hi(a, b):
    return jnp.dot(a, b, preferred_element_type=F32, precision=HI)


def _pick(n, pref, align=LANE):
    if n <= pref:
        return n
    for t in range(pref - pref % align, 0, -align):
        if n % t == 0:
            return t
    return n


class _Comm:
    def __init__(self, ins, outs, sems, start, finish):
        self.ins, self.outs, self.sems, self.start, self.finish = ins, outs, sems, start, finish


def _comm_join(comms):
    ins = [a for c in comms for a in c.ins]
    outs = [a for c in comms for a in c.outs]
    sems = [a for c in comms for a in c.sems]

    def split(refs, attr):
        res, i = [], 0
        for c in comms:
            n = len(getattr(c, attr))
            res.append(refs[i:i + n])
            i += n
        return res

    def start(cin, cout, csem):
        for c, a, b, d in zip(comms, split(cin, "ins"), split(cout, "outs"), split(csem, "sems")):
            c.start(a, b, d)

    def finish(cin, cout, csem):
        for c, a, b, d in zip(comms, split(cin, "ins"), split(cout, "outs"), split(csem, "sems")):
            c.finish(a, b, d)

    return _Comm(ins, outs, sems, start, finish)


def _ag_comm(v, cols=False):
    if cols:
        R, Cs = v.shape
        assert Cs % LANE == 0
        out_sd = jax.ShapeDtypeStruct((R, N_DEV * Cs), v.dtype)
    else:
        out_sd = jax.ShapeDtypeStruct((N_DEV,) + v.shape, v.dtype)

    def parts(x_ref, out_ref, send_sems, recv_sems, local_sem):
        x, y, c = lax.axis_index("x"), lax.axis_index("y"), lax.axis_index("c")
        me, sibling = (x, y, c), (x, y, 1 - c)
        chips = [(1 - x, y), (x, 1 - y), (1 - x, 1 - y)]

        def slab(px, py, pc):
            k = 4 * px + 2 * py + pc
            if cols:
                return out_ref.at[:, pl.ds(pl.multiple_of(k * Cs, LANE), Cs)]
            return out_ref.at[k]

        def copy(k, block, to, src=None):
            return pltpu.make_async_remote_copy(
                src_ref=slab(*block) if src is None else src, dst_ref=slab(*block),
                send_sem=send_sems.at[k], recv_sem=recv_sems.at[k], device_id=to, device_id_type=MESH)

        mine = pltpu.make_async_copy(x_ref, slab(*me), local_sem)
        first = [copy(0, me, sibling, src=x_ref)]
        first += [copy(1 + j, me, (*chip, c), src=x_ref) for j, chip in enumerate(chips)]
        passed = [copy(4 + j, (*chip, c), sibling) for j, chip in enumerate(chips)]
        return me, sibling, chips, c, copy, mine, first, passed

    def start(cin, cout, csem):
        _, _, _, _, _, mine, first, _ = parts(cin[0], cout[0], *csem)
        mine.start()
        for cp in first:
            cp.start()

    def finish(cin, cout, csem):
        me, sibling, chips, c, copy, mine, first, passed = parts(cin[0], cout[0], *csem)
        for j, chip in enumerate(chips):
            copy(1 + j, (*chip, c), me).wait_recv()
            passed[j].start()
        copy(0, sibling, me).wait_recv()
        for j, chip in enumerate(chips):
            copy(4 + j, (*chip, 1 - c), me).wait_recv()
        for cp in first + passed:
            cp.wait_send()
        mine.wait()

    return _Comm([v], [out_sd],
                 [pltpu.SemaphoreType.DMA((7,)), pltpu.SemaphoreType.DMA((7,)), pltpu.SemaphoreType.DMA], start, finish)


def _a2a_comm(v, cols=False):
    if cols:
        R, C = v.shape
        Cs = C // N_DEV
        assert Cs % LANE == 0
        out_sd = jax.ShapeDtypeStruct((N_DEV, R, Cs), v.dtype)
    else:
        out_sd = jax.ShapeDtypeStruct(v.shape, v.dtype)

    def parts(x_ref, out_ref, send_sems, recv_sems, local_sem):
        x, y, c = lax.axis_index("x"), lax.axis_index("y"), lax.axis_index("c")
        me = 4 * x + 2 * y + c

        def src(k):
            if cols:
                return x_ref.at[:, pl.ds(pl.multiple_of(k * Cs, LANE), Cs)]
            return x_ref.at[k]

        mine = pltpu.make_async_copy(src(me), out_ref.at[me], local_sem)
        sends, recvs = [], []
        for k, mask in enumerate(range(1, N_DEV)):
            px = 1 - x if (mask >> 2) & 1 else x
            py = 1 - y if (mask >> 1) & 1 else y
            pc = 1 - c if mask & 1 else c
            peer = 4 * px + 2 * py + pc
            sends.append(pltpu.make_async_remote_copy(
                src_ref=src(peer), dst_ref=out_ref.at[me],
                send_sem=send_sems.at[k], recv_sem=recv_sems.at[k], device_id=(px, py, pc), device_id_type=MESH))
            recvs.append(pltpu.make_async_remote_copy(
                src_ref=src(me), dst_ref=out_ref.at[peer],
                send_sem=send_sems.at[k], recv_sem=recv_sems.at[k], device_id=(px, py, pc), device_id_type=MESH))
        return mine, sends, recvs

    def start(cin, cout, csem):
        mine, sends, _ = parts(cin[0], cout[0], *csem)
        mine.start()
        for cp in sends:
            cp.start()

    def finish(cin, cout, csem):
        mine, sends, recvs = parts(cin[0], cout[0], *csem)
        for cp in recvs:
            cp.wait_recv()
        for cp in sends:
            cp.wait_send()
        mine.wait()

    return _Comm([v], [out_sd],
                 [pltpu.SemaphoreType.DMA((7,)), pltpu.SemaphoreType.DMA((7,)), pltpu.SemaphoreType.DMA], start, finish)


def _pair_comm(v4):
    def copy(x_ref, out_ref, send_sem, recv_sem):
        x, y, c = lax.axis_index("x"), lax.axis_index("y"), lax.axis_index("c")
        return pltpu.make_async_remote_copy(src_ref=x_ref, dst_ref=out_ref, send_sem=send_sem, recv_sem=recv_sem,
                                            device_id=(x, y, 1 - c), device_id_type=MESH)

    def start(cin, cout, csem):
        copy(cin[0], cout[0], *csem).start()

    def finish(cin, cout, csem):
        copy(cin[0], cout[0], *csem).wait()

    return _Comm([v4], [jax.ShapeDtypeStruct(v4.shape, v4.dtype)],
                 [pltpu.SemaphoreType.DMA, pltpu.SemaphoreType.DMA], start, finish)


def _quad_comm(v4):
    def parts(x_ref, out_ref, send_sems, recv_sems, local_sem):
        x, y, c = lax.axis_index("x"), lax.axis_index("y"), lax.axis_index("c")
        me = 2 * x + y
        mine = pltpu.make_async_copy(x_ref.at[me], out_ref.at[me], local_sem)
        sends, recvs = [], []
        for k, mask in enumerate(range(1, 4)):
            px = 1 - x if (mask >> 1) & 1 else x
            py = 1 - y if mask & 1 else y
            peer = 2 * px + py
            sends.append(pltpu.make_async_remote_copy(
                src_ref=x_ref.at[peer], dst_ref=out_ref.at[me],
                send_sem=send_sems.at[k], recv_sem=recv_sems.at[k], device_id=(px, py, c), device_id_type=MESH))
            recvs.append(pltpu.make_async_remote_copy(
                src_ref=x_ref.at[me], dst_ref=out_ref.at[peer],
                send_sem=send_sems.at[k], recv_sem=recv_sems.at[k], device_id=(px, py, c), device_id_type=MESH))
        return mine, sends, recvs

    def start(cin, cout, csem):
        mine, sends, _ = parts(cin[0], cout[0], *csem)
        mine.start()
        for cp in sends:
            cp.start()

    def finish(cin, cout, csem):
        mine, sends, recvs = parts(cin[0], cout[0], *csem)
        for cp in recvs:
            cp.wait_recv()
        for cp in sends:
            cp.wait_send()
        mine.wait()

    return _Comm([v4], [jax.ShapeDtypeStruct(v4.shape, v4.dtype)],
                 [pltpu.SemaphoreType.DMA((3,)), pltpu.SemaphoreType.DMA((3,)), pltpu.SemaphoreType.DMA], start, finish)


def _run_comm(comm, name):
    nci, nco = len(comm.ins), len(comm.outs)

    def body(*refs):
        comm.start(refs[:nci], refs[nci:nci + nco], refs[nci + nco:])
        comm.finish(refs[:nci], refs[nci:nci + nco], refs[nci + nco:])

    anyspec = pl.BlockSpec(memory_space=pl.ANY)
    return pl.pallas_call(body, name=name, out_shape=list(comm.outs), in_specs=[anyspec] * nci,
                          out_specs=[anyspec] * nco, scratch_shapes=list(comm.sems))(*comm.ins)


def _all_gather(v, name):
    return _run_comm(_ag_comm(v), name)[0]


def _pcall(body, args, *, name, grid, in_specs, out_specs, out_shape, scratch=(), sem=None, comm=None):
    out_shape, out_specs = list(out_shape), list(out_specs)
    if comm is None:
        return pl.pallas_call(body, name=name, grid=grid, in_specs=list(in_specs), out_specs=out_specs,
                              out_shape=out_shape, scratch_shapes=list(scratch), compiler_params=_cp(sem))(*args)
    ni, no, ns = len(args), len(out_shape), len(scratch)
    nci, nco = len(comm.ins), len(comm.outs)

    def wrapped(*refs):
        ins, cin = refs[:ni], refs[ni:ni + nci]
        o0 = ni + nci
        outs, cout = refs[o0:o0 + no], refs[o0 + no:o0 + no + nco]
        s0 = o0 + no + nco
        scr, csem = refs[s0:s0 + ns], refs[s0 + ns:]
        first = functools.reduce(jnp.logical_and, [pl.program_id(a) == 0 for a in range(len(grid))])
        last = functools.reduce(jnp.logical_and, [pl.program_id(a) == g - 1 for a, g in enumerate(grid)])

        @pl.when(first)
        def _():
            comm.start(cin, cout, csem)

        body(*ins, *outs, *scr)

        @pl.when(last)
        def _():
            comm.finish(cin, cout, csem)

    anyspec = pl.BlockSpec(memory_space=pl.ANY)
    res = pl.pallas_call(
        wrapped, name=name, grid=grid, in_specs=list(in_specs) + [anyspec] * nci,
        out_specs=out_specs + [anyspec] * nco, out_shape=out_shape + list(comm.outs),
        scratch_shapes=list(scratch) + list(comm.sems),
        compiler_params=_cp(("arbitrary",) * len(grid)))(*args, *comm.ins)
    return list(res[:no]) + list(res[no:])


def _matmul(a, b, *, name, ta=False, tb=False, out_dtype=F32, tm=1024, tn=1024, tk=2048,
            exact=False, a_silu=False, bias=None, comm=None):
    (K, M) = a.shape if ta else a.shape[::-1]
    (N, K2) = b.shape if tb else b.shape[::-1]
    assert K == K2, (a.shape, b.shape, ta, tb)
    tm, tn, tk = _pick(M, tm), _pick(N, tn), _pick(K, tk)
    nk = K // tk
    a_spec = pl.BlockSpec((tk, tm), lambda i, j, k: (k, i)) if ta else pl.BlockSpec((tm, tk), lambda i, j, k: (i, k))
    b_spec = pl.BlockSpec((tn, tk), lambda i, j, k: (j, k)) if tb else pl.BlockSpec((tk, tn), lambda i, j, k: (k, j))
    dims = (((0,) if ta else (1,), (1,) if tb else (0,)), ((), ()))
    in_specs, args = [a_spec, b_spec], [a, b]
    if bias is not None:
        if bias.shape[0] == 1:
            in_specs.append(pl.BlockSpec((1, tn), lambda i, j, k: (0, j)))
        else:
            in_specs.append(pl.BlockSpec((tm, tn), lambda i, j, k: (i, j)))
        args.append(bias)

    def body(*refs):
        a_ref, b_ref = refs[0], refs[1]
        bias_ref = refs[2] if bias is not None else None
        o_ref = refs[2 + (bias is not None)]
        av, bv = a_ref[...], b_ref[...]
        if a_silu:
            av = _silu(av.astype(F32))
        if exact:
            p = lax.dot_general(av.astype(F32), bv.astype(F32), dims, preferred_element_type=F32, precision=HI)
        else:
            p = lax.dot_general(av.astype(BF16), bv.astype(BF16), dims, preferred_element_type=F32)

        def fin(r):
            if bias_ref is not None:
                r = r + bias_ref[...]
            o_ref[...] = r.astype(o_ref.dtype)

        if nk == 1:
            fin(p)
        else:
            acc = refs[-1]
            k = pl.program_id(2)

            @pl.when(k == 0)
            def _():
                acc[...] = p

            @pl.when(k > 0)
            def _():
                acc[...] += p

            @pl.when(k == nk - 1)
            def _():
                fin(acc[...])

    res = _pcall(
        body, args, name=name,
        out_shape=[jax.ShapeDtypeStruct((M, N), out_dtype)],
        grid=(M // tm, N // tn, nk),
        in_specs=in_specs,
        out_specs=[pl.BlockSpec((tm, tn), lambda i, j, k: (i, j))],
        scratch=[pltpu.VMEM((tm, tn), F32)] if nk > 1 else [],
        sem=("parallel", "parallel", "arbitrary"), comm=comm)
    return res[0] if comm is None else res


def _rowmap(fn, rows, bcasts, outs, accs, *, name, tr=256, cw=None, comm=None):
    L = rows[0][0].shape[0]
    tr = _pick(L, tr)
    nr, nb, no, na = len(rows), len(bcasts), len(outs), len(accs)
    if cw is None:
        ncol = 1
        widths = [w for (_, _, w) in rows]
    else:
        wtot = rows[0][2]
        ncol = wtot // cw
        widths = [cw] * nr
    in_specs, args = [], []
    for (arr, off, w), bw in zip(rows, widths):
        assert off % bw == 0
        in_specs.append(pl.BlockSpec((tr, bw), functools.partial(lambda j, i, o: (i, o + j), o=off // bw)))
        args.append(arr)
    for arr in bcasts:
        bw = arr.shape[1] if cw is None else cw
        in_specs.append(pl.BlockSpec((arr.shape[0], bw), lambda j, i: (0, j)))
        args.append(arr)
    out_shape, out_specs = [], []
    for spec in outs:
        if len(spec) == 2:
            (w, dt), off = spec, 0
            bw = w if cw is None else cw
        else:
            w, dt, off, bw = spec
        out_shape.append(jax.ShapeDtypeStruct((L, w), dt))
        out_specs.append(pl.BlockSpec((tr, bw), functools.partial(lambda j, i, o: (i, o + j), o=off // bw)))
    for (r, w) in accs:
        bw = w if cw is None else cw
        out_shape.append(jax.ShapeDtypeStruct((r, w), F32))
        out_specs.append(pl.BlockSpec((r, bw), lambda j, i: (0, j)))

    def body(*refs):
        ins = [r[...] for r in refs[:nr + nb]]
        o_refs = refs[nr + nb:nr + nb + no]
        a_refs = refs[nr + nb + no:]
        o, a = fn(*ins)
        for ref, val in zip(o_refs, o):
            ref[...] = val.astype(ref.dtype)
        if na:
            @pl.when(pl.program_id(1) == 0)
            def _():
                for ref in a_refs:
                    ref[...] = jnp.zeros_like(ref)

            for ref, val in zip(a_refs, a):
                ref[...] += val

    return _pcall(body, args, name=name, out_shape=out_shape, grid=(ncol, L // tr), in_specs=in_specs,
                  out_specs=out_specs, sem=("parallel", "arbitrary"), comm=comm)


def _csum(v):
    return jnp.sum(v, axis=0, keepdims=True)


def _shift_rows(v, s, rows):
    if s == 0:
        return v
    n = v.shape[0]
    r = pltpu.roll(v, s % n, 0)
    if s > 0:
        return jnp.where(rows >= s, r, 0.0)
    return jnp.where(rows < n + s, r, 0.0)


def _conv_fwd(proj, off, width, w, b, *, name, tc=256):
    L = proj.shape[0]
    tc = _pick(width, tc)

    def body(x_ref, w_ref, b_ref, o_ref):
        x = x_ref[...]
        rows = lax.broadcasted_iota(jnp.int32, x.shape, 0)
        acc = jnp.zeros_like(x) + b_ref[...]
        for k in range(SSD_CONV_W):
            acc = acc + w_ref[k:k + 1, :] * _shift_rows(x, SSD_CONV_W - 1 - k, rows)
        o_ref[...] = _silu(acc)

    return pl.pallas_call(
        body, name=name, out_shape=jax.ShapeDtypeStruct((L, width), F32), grid=(width // tc,),
        in_specs=[pl.BlockSpec((L, tc), functools.partial(lambda j, o: (0, o + j), o=off // tc)),
                  pl.BlockSpec((SSD_CONV_W, tc), lambda j: (0, j)), pl.BlockSpec((1, tc), lambda j: (0, j))],
        out_specs=pl.BlockSpec((L, tc), lambda j: (0, j)),
        compiler_params=_cp(("parallel",)),
    )(proj, w, b)


def _conv_bwd(proj, off, width, w, b, dys, *, name, tc=256, comm=None):
    L = proj.shape[0]
    tc = _pick(width, tc)
    ntile = [d.shape[1] // tc for d in dys]
    assert all(d.shape[1] % tc == 0 for d in dys) and sum(ntile) == width // tc
    first = [sum(ntile[:i]) for i in range(len(dys))]

    def body(x_ref, w_ref, b_ref, *rest):
        dy_refs, (dx_ref, dw_ref, db_ref) = rest[:len(dys)], rest[len(dys):]
        j = pl.program_id(0)
        dy = dy_refs[0][...]
        for i in range(1, len(dys)):
            dy = jnp.where(j >= first[i], dy_refs[i][...], dy)
        x = x_ref[...]
        rows = lax.broadcasted_iota(jnp.int32, x.shape, 0)
        xs = [_shift_rows(x, SSD_CONV_W - 1 - k, rows) for k in range(SSD_CONV_W)]
        pre = jnp.zeros_like(x) + b_ref[...]
        for k in range(SSD_CONV_W):
            pre = pre + w_ref[k:k + 1, :] * xs[k]
        dpre = dy * _dsilu(pre)
        dx = jnp.zeros_like(x)
        for k in range(SSD_CONV_W):
            dx = dx + w_ref[k:k + 1, :] * _shift_rows(dpre, -(SSD_CONV_W - 1 - k), rows)
            dw_ref[k:k + 1, :] = _csum(dpre * xs[k])
        dx_ref[...] = dx.astype(dx_ref.dtype)
        db_ref[...] = _csum(dpre)

    dy_specs = [pl.BlockSpec((L, tc), functools.partial(lambda j, f, n: (0, jnp.clip(j - f, 0, n - 1)), f=f, n=n))
                for f, n in zip(first, ntile)]
    return _pcall(
        body, (proj, w, b, *dys), name=name,
        out_shape=[jax.ShapeDtypeStruct((L, width), BF16), jax.ShapeDtypeStruct((SSD_CONV_W, width), F32),
                   jax.ShapeDtypeStruct((1, width), F32)],
        grid=(width // tc,),
        in_specs=[pl.BlockSpec((L, tc), functools.partial(lambda j, o: (0, o + j), o=off // tc)),
                  pl.BlockSpec((SSD_CONV_W, tc), lambda j: (0, j)), pl.BlockSpec((1, tc), lambda j: (0, j))] + dy_specs,
        out_specs=[pl.BlockSpec((L, tc), lambda j: (0, j)), pl.BlockSpec((SSD_CONV_W, tc), lambda j: (0, j)),
                   pl.BlockSpec((1, tc), lambda j: (0, j))],
        sem=("parallel",), comm=comm)


def _tri(Q):
    ri = lax.broadcasted_iota(jnp.int32, (Q, Q), 0)
    ci = lax.broadcasted_iota(jnp.int32, (Q, Q), 1)
    return ri >= ci, ri <= ci


def _ssd_prep(dt_raw, bias, alog, *, name):
    L, W = dt_raw.shape
    Q = SSD_CHUNK

    def body(r_ref, b_ref, al_ref, dt_ref, a_ref):
        lower, _ = _tri(Q)
        dt = _softplus(r_ref[...] + b_ref[...])
        dt_ref[...] = dt
        a_ref[...] = _hi(lower.astype(F32), dt * (-jnp.exp(al_ref[...])))

    blk = pl.BlockSpec((Q, W), lambda c: (c, 0))
    one = pl.BlockSpec((1, W), lambda c: (0, 0))
    sd = jax.ShapeDtypeStruct((L, W), F32)
    return _pcall(body, (dt_raw, bias, alog), name=name, out_shape=[sd, sd], grid=(L // Q,),
                  in_specs=[blk, one, one], out_specs=[blk, blk], sem=("parallel",))


def _ssd_post(da, dar, s1, dt, dt_raw, bias, alog, *, name):
    L, W = da.shape
    Q = SSD_CHUNK

    def body(da_ref, dar_ref, s1_ref, dt_ref, r_ref, b_ref, al_ref, o_ref, db_ref, dal_ref):
        _, upper = _tri(Q)
        A = -jnp.exp(al_ref[...])
        ddtA = _hi(upper.astype(F32), da_ref[...] - dar_ref[...])
        ddt_raw = (ddtA * A + s1_ref[...]) * jax.nn.sigmoid(r_ref[...] + b_ref[...])
        o_ref[...] = ddt_raw.astype(o_ref.dtype)

        @pl.when(pl.program_id(0) == 0)
        def _():
            db_ref[...] = jnp.zeros_like(db_ref)
            dal_ref[...] = jnp.zeros_like(dal_ref)

        db_ref[...] += _csum(ddt_raw)
        dal_ref[...] += _csum(ddtA * dt_ref[...]) * A

    blk = pl.BlockSpec((Q, W), lambda c: (c, 0))
    one = pl.BlockSpec((1, W), lambda c: (0, 0))
    return _pcall(body, (da, dar, s1, dt, dt_raw, bias, alog), name=name,
                  out_shape=[jax.ShapeDtypeStruct((L, W), BF16), jax.ShapeDtypeStruct((1, W), F32),
                             jax.ShapeDtypeStruct((1, W), F32)],
                  grid=(L // Q,), in_specs=[blk, blk, blk, blk, blk, one, one], out_specs=[blk, one, one],
                  sem=("arbitrary",))


def _head_sum(v, K, KP):
    P = KP // K
    t_r = lax.broadcasted_iota(jnp.int32, (KP, K), 0)
    t_c = lax.broadcasted_iota(jnp.int32, (KP, K), 1)
    Et = ((t_r >= t_c * P) & (t_r < (t_c + 1) * P)).astype(BF16)
    hi = v.astype(BF16)
    lo = (v - hi.astype(F32)).astype(BF16)
    return _nn(hi, Et) + _nn(lo, Et)


def _half_masks():
    li = lax.broadcasted_iota(jnp.int32, (1, LANE), 1)
    return [(li < SSD_HEAD_DIM).astype(F32), (li >= SSD_HEAD_DIM).astype(F32)]


def _ssd_specs(K, KP, d_inner, rev, nc):
    Q, N, G = SSD_CHUNK, SSD_D_STATE, SSD_N_GROUPS
    cidx = (lambda c: nc - 1 - c) if rev else (lambda c: c)
    b_off, c_off = d_inner // N, d_inner // N + G
    return [
        pl.BlockSpec((Q, KP), lambda g, c: (cidx(c), g)),
        pl.BlockSpec((Q, N), lambda g, c: (cidx(c), b_off + g)),
        pl.BlockSpec((Q, N), lambda g, c: (cidx(c), c_off + g)),
        pl.BlockSpec((None, Q, K), lambda g, c: (g, cidx(c), 0)),
        pl.BlockSpec((None, Q, K), lambda g, c: (g, cidx(c), 0)),
        pl.BlockSpec((None, K, Q), lambda g, c: (g, 0, cidx(c))),
        pl.BlockSpec((1, KP), lambda g, c: (0, g)),
    ]


def _expand_heads(vc, K):
    Q = vc.shape[0]
    left = lax.broadcasted_iota(jnp.int32, (Q, LANE), 1) < SSD_HEAD_DIM
    parts = []
    for pr in range(K // 2):
        a = jnp.broadcast_to(vc[:, 2 * pr:2 * pr + 1], (Q, LANE))
        b = jnp.broadcast_to(vc[:, 2 * pr + 1:2 * pr + 2], (Q, LANE))
        parts.append(jnp.where(left, a, b))
    return jnp.concatenate(parts, axis=1) if len(parts) > 1 else parts[0]


def _ssd_fwd(xbc, dt_c, a_c, a_r, d_full, *, d_inner, name, comm=None):
    L = xbc.shape[0]
    G, N, Q, P = SSD_N_GROUPS, SSD_D_STATE, SSD_CHUNK, SSD_HEAD_DIM
    KP = d_inner // G
    K = KP // P
    nc = L // Q
    npair = KP // LANE

    def body(xs_ref, b_ref, c_ref, dtc_ref, ac_ref, ar_ref, df_ref, y_ref, st_ref, S):
        @pl.when(pl.program_id(1) == 0)
        def _():
            S[...] = jnp.zeros_like(S)

        lower, _ = _tri(Q)
        st_ref[...] = S[...]
        xs = xs_ref[...]
        Bm, Cm = b_ref[...], c_ref[...]
        Bb, Cb = Bm.astype(BF16), Cm.astype(BF16)
        a_c, a_r = ac_ref[...], ar_ref[...]
        a_f = _expand_heads(a_c, K)
        X = xs * _expand_heads(dtc_ref[...], K)
        ea = jnp.exp(a_f)
        alast = a_f[Q - 1:Q, :]
        tail = jnp.exp(alast - a_f)
        cb = _nt(Cb, Bb)
        Sv = S[...]
        yoff = _nn(Cb, Sv.astype(BF16)) * ea
        skip = xs * df_ref[...]
        masks = _half_masks()
        for pr in range(npair):
            Xp = X[:, pr * LANE:(pr + 1) * LANE]
            acc = yoff[:, pr * LANE:(pr + 1) * LANE] + skip[:, pr * LANE:(pr + 1) * LANE]
            for hh in range(2):
                k = 2 * pr + hh
                seg = a_c[:, k:k + 1] - a_r[k:k + 1, :]
                dec = jnp.where(lower, jnp.exp(jnp.minimum(seg, 0.0)), 0.0)
                acc = acc + _nn((cb * dec).astype(BF16), (Xp * masks[hh]).astype(BF16))
            y_ref[:, pr * LANE:(pr + 1) * LANE] = acc
        Bt = Bm.T
        S[...] = Sv * jnp.exp(alast) + _nn(Bt.astype(BF16), (X * tail).astype(BF16))

    return _pcall(
        body, (xbc, xbc, xbc, dt_c, a_c, a_r, d_full), name=name,
        out_shape=[jax.ShapeDtypeStruct((L, d_inner), F32), jax.ShapeDtypeStruct((G, nc, N, KP), F32)],
        grid=(G, nc),
        in_specs=_ssd_specs(K, KP, d_inner, False, nc),
        out_specs=[pl.BlockSpec((Q, KP), lambda g, c: (c, g)), pl.BlockSpec((None, None, N, KP), lambda g, c: (g, c, 0, 0))],
        scratch=[pltpu.VMEM((N, KP), F32)],
        sem=("parallel", "arbitrary"), comm=comm)


def _ssd_bwd(xbc, dt_c, a_c, a_r, d_full, states, dy, *, d_inner, name, comm=None):
    L = xbc.shape[0]
    G, N, Q, P = SSD_N_GROUPS, SSD_D_STATE, SSD_CHUNK, SSD_HEAD_DIM
    KP = d_inner // G
    K = KP // P
    nc = L // Q
    npair = KP // LANE

    def body(xs_ref, b_ref, c_ref, dtc_ref, ac_ref, ar_ref, df_ref, st_ref, dy_ref,
             dxs_ref, db_ref, dc_ref, da_ref, dar_ref, s1_ref, dd_ref, dS):
        @pl.when(pl.program_id(1) == 0)
        def _():
            dS[...] = jnp.zeros_like(dS)
            dd_ref[...] = jnp.zeros_like(dd_ref)

        lower, _ = _tri(Q)
        a_c, a_r = ac_ref[...], ar_ref[...]
        a_f, dt_f = _expand_heads(a_c, K), _expand_heads(dtc_ref[...], K)
        xs = xs_ref[...]
        Bm, Cm = b_ref[...], c_ref[...]
        Bb, Cb = Bm.astype(BF16), Cm.astype(BF16)
        dY = dy_ref[...]
        X = xs * dt_f
        ea = jnp.exp(a_f)
        alast = a_f[Q - 1:Q, :]
        tail = jnp.exp(alast - a_f)
        el = jnp.exp(alast)
        Sv, dSn = st_ref[...], dS[...]
        Sb, dSb = Sv.astype(BF16), dSn.astype(BF16)
        cb = _nt(Cb, Bb)
        yoff_raw = _nn(Cb, Sb)
        dYe = dY * ea
        dC = _nt(dYe.astype(BF16), Sb)
        dS[...] = dSn * el + _nn(Cm.T.astype(BF16), dYe.astype(BF16))
        Gx = _nn(Bb, dSb)
        dB = _nt((X * tail).astype(BF16), dSb)
        dtl = Gx * X * tail
        da_f = dYe * yoff_raw - dtl
        dalast_f = _csum(dtl) + _csum(dSn * Sv) * el
        da_c = _head_sum(da_f, K, KP)
        onek = lax.broadcasted_iota(jnp.int32, (1, K), 1)
        onek_col = lax.broadcasted_iota(jnp.int32, (K, 1), 0)
        masks = _half_masks()
        dcb = jnp.zeros((Q, Q), F32)
        da_r = jnp.zeros((K, Q), F32)
        dX_parts = []
        for pr in range(npair):
            Xp = X[:, pr * LANE:(pr + 1) * LANE]
            dYp = dY[:, pr * LANE:(pr + 1) * LANE]
            dXp = Gx[:, pr * LANE:(pr + 1) * LANE] * tail[:, pr * LANE:(pr + 1) * LANE]
            for hh in range(2):
                k = 2 * pr + hh
                Xk = (Xp * masks[hh]).astype(BF16)
                dYk = (dYp * masks[hh]).astype(BF16)
                seg = a_c[:, k:k + 1] - a_r[k:k + 1, :]
                dec = jnp.where(lower, jnp.exp(jnp.minimum(seg, 0.0)), 0.0)
                Mk = cb * dec
                dM = _nt(dYk, Xk)
                dcb = dcb + dM * dec
                Gk = dM * Mk
                da_c = da_c + jnp.sum(Gk, axis=1, keepdims=True) * (onek == k).astype(F32)
                da_r = da_r + (onek_col == k).astype(F32) * jnp.sum(Gk, axis=0, keepdims=True)
                dXp = dXp + _tn(Mk.astype(BF16), dYk)
            dX_parts.append(dXp)
        dX = jnp.concatenate(dX_parts, axis=1) if npair > 1 else dX_parts[0]
        dcbb = dcb.astype(BF16)
        dC = dC + _nn(dcbb, Bb)
        dB = dB + _tn(dcbb, Cb)
        lastrow = (lax.broadcasted_iota(jnp.int32, (Q, 1), 0) == Q - 1).astype(F32)
        da_ref[...] = da_c + lastrow * _head_sum(dalast_f, K, KP)
        dar_ref[...] = da_r
        s1_ref[...] = _head_sum(dX * xs, K, KP)
        dd_ref[...] += _csum(dY * xs)
        dxs_ref[...] = dX * dt_f + dY * df_ref[...]
        db_ref[...] = dB
        dc_ref[...] = dC

    rc = lambda c: nc - 1 - c
    tok = jax.ShapeDtypeStruct((G, L, K), F32)
    tok_spec = pl.BlockSpec((None, Q, K), lambda g, c: (g, rc(c), 0))
    return _pcall(
        body, (xbc, xbc, xbc, dt_c, a_c, a_r, d_full, states, dy), name=name,
        out_shape=[jax.ShapeDtypeStruct((L, d_inner), F32), jax.ShapeDtypeStruct((L, G * N), F32),
                   jax.ShapeDtypeStruct((L, G * N), F32), tok, jax.ShapeDtypeStruct((G, K, L), F32), tok,
                   jax.ShapeDtypeStruct((1, d_inner), F32)],
        grid=(G, nc),
        in_specs=_ssd_specs(K, KP, d_inner, True, nc) + [
            pl.BlockSpec((None, None, N, KP), lambda g, c: (g, rc(c), 0, 0)),
            pl.BlockSpec((Q, KP), lambda g, c: (rc(c), g))],
        out_specs=[pl.BlockSpec((Q, KP), lambda g, c: (rc(c), g)), pl.BlockSpec((Q, N), lambda g, c: (rc(c), g)),
                   pl.BlockSpec((Q, N), lambda g, c: (rc(c), g)), tok_spec,
                   pl.BlockSpec((None, K, Q), lambda g, c: (g, 0, rc(c))), tok_spec,
                   pl.BlockSpec((1, KP), lambda g, c: (0, g))],
        scratch=[pltpu.VMEM((N, KP), F32)],
        sem=("parallel", "arbitrary"), comm=comm)


def _slopes():
    n = DIL_N_GROUPS * DIL_HEADS
    s = 2.0 ** (-8.0 * np.arange(1, n + 1) / n)
    return s.reshape(DIL_N_GROUPS, DIL_HEADS).astype(np.float32)


ATT_TB = 2048


def _tn(a, b):
    return lax.dot_general(a, b, (((0,), (0,)), ((), ())), preferred_element_type=F32)


def _slope_rows(g):
    return jnp.asarray(np.repeat(_slopes()[g][:, None], LANE, axis=1))


def _attn_bias(slope_row, d):
    B = DIL_BLOCK
    qi = lax.broadcasted_iota(jnp.int32, (B, B), 0)
    kj = lax.broadcasted_iota(jnp.int32, (B, B), 1)
    sd = slope_row * float(d)
    cur = jnp.where(kj <= qi, -(qi - kj).astype(F32) * sd, NEG)
    prv = jnp.where(kj >= qi, -(qi + B - kj).astype(F32) * sd, NEG)
    return cur, prv


def _sub_rows(j, r, d):
    base = j * DIL_BLOCK * d + r
    return pl.ds(base, DIL_BLOCK, stride=d) if d > 1 else pl.ds(base, DIL_BLOCK)


def _attn_geometry(L, g):
    window, d = DIL_PATTERNS[g]
    tb = min(ATT_TB, L)
    assert window // d == DIL_BLOCK and tb % (d * DIL_BLOCK) == 0 and L % tb == 0
    return d, tb, L // tb, tb // (d * DIL_BLOCK)


def _kv_specs(g, tb, nb):
    E, nh = DIL_HEAD_DIM, DIL_N_GROUPS * DIL_HEADS
    prev = lambda b: jnp.maximum(b - 1, 0)
    return [pl.BlockSpec((tb, E), lambda b, h: (b, g * DIL_HEADS + h)),
            pl.BlockSpec((tb, E), lambda b, h: (b, nh + g * DIL_HEADS + h)),
            pl.BlockSpec((tb, E), lambda b, h: (prev(b), g * DIL_HEADS + h)),
            pl.BlockSpec((tb, E), lambda b, h: (prev(b), nh + g * DIL_HEADS + h))]


def _attn_fwd(qz, kv, g, *, name):
    L = qz.shape[0]
    d, tb, nb, nj = _attn_geometry(L, g)
    B, E = DIL_BLOCK, DIL_HEAD_DIM
    scale = E ** -0.5

    def body(sl_ref, q_ref, kc_ref, vc_ref, kp_ref, vp_ref, o_ref, lse_ref):
        b, h = pl.program_id(0), pl.program_id(1)

        @pl.when(h == 0)
        def _():
            lse_ref[...] = jnp.zeros_like(lse_ref)

        bias_c, bias_p = _attn_bias(sl_ref[pl.ds(h, 1), :], d)
        bias_p0 = jnp.where(b > 0, bias_p, NEG)
        oneh = (lax.broadcasted_iota(jnp.int32, (1, LANE), 1) == h).astype(F32)
        for r in range(d):
            for j in range(nj):
                rows = _sub_rows(j, r, d)
                qs = q_ref[rows, :].astype(BF16)
                kc, vc = kc_ref[rows, :].astype(BF16), vc_ref[rows, :].astype(BF16)
                if j > 0:
                    prows, bp = _sub_rows(j - 1, r, d), bias_p
                    kq, vq = kc_ref[prows, :].astype(BF16), vc_ref[prows, :].astype(BF16)
                else:
                    prows, bp = _sub_rows(nj - 1, r, d), bias_p0
                    kq, vq = kp_ref[prows, :].astype(BF16), vp_ref[prows, :].astype(BF16)
                sc = _nt(qs, kc) * scale + bias_c
                sp = _nt(qs, kq) * scale + bp
                m = jnp.maximum(jnp.max(sc, axis=1, keepdims=True), jnp.max(sp, axis=1, keepdims=True))
                pc, pp = jnp.exp(sc - m), jnp.exp(sp - m)
                den = jnp.sum(pc, axis=1, keepdims=True) + jnp.sum(pp, axis=1, keepdims=True)
                o = _nn(pc.astype(BF16), vc) + _nn(pp.astype(BF16), vq)
                o_ref[rows, :] = o / den
                lse_ref[rows, :] = lse_ref[rows, :] + (m + jnp.log(den)) * oneh

    return _pcall(
        body, (_slope_rows(g), qz, kv, kv, kv, kv), name=name,
        out_shape=[jax.ShapeDtypeStruct((L, DIL_W), F32), jax.ShapeDtypeStruct((L, LANE), F32)],
        grid=(nb, DIL_HEADS),
        in_specs=[pl.BlockSpec((DIL_HEADS, LANE), lambda b, h: (0, 0)),
                  pl.BlockSpec((tb, E), lambda b, h: (b, g * DIL_HEADS + h))] + _kv_specs(g, tb, nb),
        out_specs=[pl.BlockSpec((tb, E), lambda b, h: (b, h)), pl.BlockSpec((tb, LANE), lambda b, h: (b, 0))],
        sem=("parallel", "arbitrary"))


def _attn_bwd(qz, kv, do, lse, dl, dqz, dk, dv, g, *, name):
    L = qz.shape[0]
    d, tb, nb, nj = _attn_geometry(L, g)
    B, E = DIL_BLOCK, DIL_HEAD_DIM
    scale = E ** -0.5
    nxt = lambda b: jnp.minimum(b + 1, nb - 1)
    fresh = dk is None

    def body(sl_ref, qc_ref, qn_ref, kc_ref, vc_ref, kp_ref, vp_ref, doc_ref, don_ref, lsec_ref, lsen_ref,
             dlc_ref, dln_ref, *rest):
        dq_ref, dk_ref, dv_ref = rest[-3:]
        b, h = pl.program_id(0), pl.program_id(1)
        bias_c, bias_p = _attn_bias(sl_ref[pl.ds(h, 1), :], d)
        bias_first = jnp.where(b > 0, bias_p, NEG)
        bias_last = jnp.where(b < nb - 1, bias_p, NEG)
        oneh = (lax.broadcasted_iota(jnp.int32, (1, LANE), 1) == h).astype(F32)

        def col(ref, rows):
            return jnp.sum(ref[rows, :] * oneh, axis=1, keepdims=True)

        def pair(q, do_, lse_, dl_, k_, v_, bias):
            p = jnp.exp(_nt(q, k_) * scale + bias - lse_)
            ds = p * (_nt(do_, v_) - dl_)
            return p.astype(BF16), ds.astype(BF16)

        for r in range(d):
            rows0 = _sub_rows(0, r, d)
            qj, doj = qc_ref[rows0, :].astype(BF16), doc_ref[rows0, :].astype(BF16)
            lsej, dlj = col(lsec_ref, rows0), col(dlc_ref, rows0)
            prows = _sub_rows(nj - 1, r, d)
            kq, vq = kp_ref[prows, :].astype(BF16), vp_ref[prows, :].astype(BF16)
            _, ds = pair(qj, doj, lsej, dlj, kq, vq, bias_first)
            dq_carry = _nn(ds, kq)
            for j in range(nj):
                rows = _sub_rows(j, r, d)
                kj, vj = kc_ref[rows, :].astype(BF16), vc_ref[rows, :].astype(BF16)
                p, ds = pair(qj, doj, lsej, dlj, kj, vj, bias_c)
                dq_ref[rows, :] = (dq_carry + _nn(ds, kj)) * scale
                dkj, dvj = _tn(ds, qj), _tn(p, doj)
                if j < nj - 1:
                    nrows = _sub_rows(j + 1, r, d)
                    qn, don = qc_ref[nrows, :].astype(BF16), doc_ref[nrows, :].astype(BF16)
                    lsen, dln, bias = col(lsec_ref, nrows), col(dlc_ref, nrows), bias_p
                else:
                    qn, don = qn_ref[rows0, :].astype(BF16), don_ref[rows0, :].astype(BF16)
                    lsen, dln, bias = col(lsen_ref, rows0), col(dln_ref, rows0), bias_last
                p2, ds2 = pair(qn, don, lsen, dln, kj, vj, bias)
                dk_ref[rows, :] = (dkj + _tn(ds2, qn)) * scale
                dv_ref[rows, :] = dvj + _tn(p2, don)
                dq_carry = _nn(ds2, kj)
                qj, doj, lsej, dlj = qn, don, lsen, dln

    hb = lambda b, h: (b, g * DIL_HEADS + h)
    anyspec = pl.BlockSpec(memory_space=pl.ANY)
    args = [_slope_rows(g), qz, qz, kv, kv, kv, kv, do, do, lse, lse, dl, dl, dqz] + ([] if fresh else [dk, dv])
    in_specs = ([pl.BlockSpec((DIL_HEADS, LANE), lambda b, h: (0, 0)),
                 pl.BlockSpec((tb, E), hb), pl.BlockSpec((tb, E), lambda b, h: (nxt(b), g * DIL_HEADS + h))]
                + _kv_specs(g, tb, nb)
                + [pl.BlockSpec((tb, E), lambda b, h: (b, h)), pl.BlockSpec((tb, E), lambda b, h: (nxt(b), h)),
                   pl.BlockSpec((tb, LANE), lambda b, h: (b, 0)), pl.BlockSpec((tb, LANE), lambda b, h: (nxt(b), 0)),
                   pl.BlockSpec((tb, LANE), lambda b, h: (b, 0)), pl.BlockSpec((tb, LANE), lambda b, h: (nxt(b), 0)),
                   anyspec] + ([] if fresh else [anyspec, anyspec]))
    aliases = {13: 0} if fresh else {13: 0, 14: 1, 15: 2}
    dkv_sd = jax.ShapeDtypeStruct((L, DIL_N_GROUPS * DIL_W), F32)
    return pl.pallas_call(
        body, name=name,
        out_shape=[jax.ShapeDtypeStruct(dqz.shape, F32), dkv_sd, dkv_sd],
        grid=(nb, DIL_HEADS), in_specs=in_specs,
        out_specs=[pl.BlockSpec((tb, E), hb), pl.BlockSpec((tb, E), hb), pl.BlockSpec((tb, E), hb)],
        input_output_aliases=aliases,
        compiler_params=_cp(("parallel", "parallel")))(*args)


def _head_expand():
    r = lax.broadcasted_iota(jnp.int32, (LANE, DIL_W), 0)
    c = lax.broadcasted_iota(jnp.int32, (LANE, DIL_W), 1)
    E = ((c >= r * DIL_HEAD_DIM) & (c < (r + 1) * DIL_HEAD_DIM)).astype(F32)
    r2 = lax.broadcasted_iota(jnp.int32, (DIL_W, LANE), 0)
    c2 = lax.broadcasted_iota(jnp.int32, (DIL_W, LANE), 1)
    Et = ((r2 >= c2 * DIL_HEAD_DIM) & (r2 < (c2 + 1) * DIL_HEAD_DIM)).astype(F32)
    return E, Et


def _merge_weights(l0, l1, l2):
    m = jnp.maximum(jnp.maximum(l0, l1), l2)
    e = [jnp.exp(l - m) for l in (l0, l1, l2)]
    tot = e[0] + e[1] + e[2]
    return [v / tot for v in e]


def _merge_fwd(os_, lses, qz, z_off, *, name):
    def fn(o0, o1, o2, l0, l1, l2, z):
        E, _ = _head_expand()
        w = _merge_weights(l0, l1, l2)
        om = sum(_hi(wg, E) * og for wg, og in zip(w, (o0, o1, o2)))
        return [om * _silu(z)], []

    rows = [(o, 0, DIL_W) for o in os_] + [(l, 0, LANE) for l in lses] + [(qz, z_off, DIL_W)]
    return _rowmap(fn, rows, [], [(DIL_W, BF16)], [], name=name)[0]


def _merge_bwd(os_, lses, qz, z_off, dog, *, name, comm=None):
    def fn(o0, o1, o2, l0, l1, l2, z, dg):
        E, Et = _head_expand()
        dg = dg.astype(F32)
        w = _merge_weights(l0, l1, l2)
        wf = [_hi(wg, E) for wg in w]
        os3 = (o0, o1, o2)
        om = sum(a * b for a, b in zip(wf, os3))
        dom = dg * _silu(z)
        dz = dg * om * _dsilu(z)
        dw = [_hi(dom * og, Et) for og in os3]
        tot = sum(a * b for a, b in zip(w, dw))
        return [wf[0] * dom, wf[1] * dom, wf[2] * dom, w[0] * tot, w[1] * tot, w[2] * tot, dz], []

    rows = ([(o, 0, DIL_W) for o in os_] + [(l, 0, LANE) for l in lses] + [(qz, z_off, DIL_W), (dog, 0, DIL_W)])
    outs = [(DIL_W, F32)] * 3 + [(LANE, F32)] * 3 + [(qz.shape[1], F32, z_off, DIL_W)]
    return _rowmap(fn, rows, [], outs, [], name=name, comm=comm)


def _adamw(gparts, w, m, v, *, name, tr=128):
    n, R, C = gparts.shape
    tr = _pick(R, tr)
    c1 = 1.0 - ADAM_B1 ** ADAM_STEP
    c2 = 1.0 - ADAM_B2 ** ADAM_STEP

    def body(g_ref, w_ref, m_ref, v_ref, go_ref, d_ref, mo_ref, vo_ref):
        g = g_ref[0].astype(F32)
        for i in range(1, n):
            g = g + g_ref[i].astype(F32)
        mn = ADAM_B1 * m_ref[...] + (1.0 - ADAM_B1) * g
        vn = ADAM_B2 * v_ref[...] + (1.0 - ADAM_B2) * jnp.square(g)
        d_ref[...] = -ADAM_LR * ((mn / c1) / (jnp.sqrt(vn / c2) + ADAM_EPS) + ADAM_WD * w_ref[...])
        go_ref[...] = g
        mo_ref[...] = mn
        vo_ref[...] = vn

    blk = pl.BlockSpec((tr, C), lambda i: (i, 0))
    sd = jax.ShapeDtypeStruct((R, C), F32)
    return pl.pallas_call(
        body, name=name, out_shape=[sd, sd, sd, sd], grid=(R // tr,),
        in_specs=[pl.BlockSpec((n, tr, C), lambda i: (0, i, 0)), blk, blk, blk],
        out_specs=[blk, blk, blk, blk],
        compiler_params=_cp(("parallel",)),
    )(gparts, w, m, v)


def _sum_parts(parts, *, name):
    n, R, C = parts.shape

    def body(p_ref, o_ref):
        s = p_ref[0]
        for i in range(1, n):
            s = s + p_ref[i]
        o_ref[...] = s

    return pl.pallas_call(
        body, name=name, out_shape=jax.ShapeDtypeStruct((R, C), F32),
        in_specs=[pl.BlockSpec(memory_space=pltpu.VMEM)], out_specs=pl.BlockSpec(memory_space=pltpu.VMEM),
    )(parts)


def _cols_from(g):
    _, R, Cs = g.shape
    return jnp.transpose(g, (1, 0, 2)).reshape(R, N_DEV * Cs)


def _col_parts(dw):
    R, C = dw.shape
    return jnp.transpose(dw.reshape(R, N_DEV, C // N_DEV), (1, 0, 2))


def _ag_cols(w_loc):
    if w_loc.shape[1] % LANE == 0:
        return _ag_comm(w_loc, cols=True), (lambda g: g)
    return _ag_comm(w_loc), _cols_from


def _rs_cols(dw):
    if (dw.shape[1] // N_DEV) % LANE == 0:
        return _a2a_comm(dw, cols=True)
    return _a2a_comm(_col_parts(dw))


def kernel(x, c, ada_w, ada_b, ln_g, ln_b, a_in_w, a_conv_w, a_conv_b, a_dt_bias, a_A_log, a_D, a_norm_g, a_out_w, kv_w, b_in_w, b_out_w, loss_target, m_ada_w, m_ada_b, m_ln_g, m_ln_b, m_a_in_w, m_a_conv_w, m_a_conv_b, m_a_dt_bias, m_a_A_log, m_a_D, m_a_norm_g, m_a_out_w, m_kv_w, m_b_in_w, m_b_out_w, v_ada_w, v_ada_b, v_ln_g, v_ln_b, v_a_in_w, v_a_conv_w, v_a_conv_b, v_a_dt_bias, v_a_A_log, v_a_D, v_a_norm_g, v_a_out_w, v_kv_w, v_b_in_w, v_b_out_w):
    L, D = x.shape[1], x.shape[2]
    H = a_dt_bias.shape[1]
    d_inner = H * SSD_HEAD_DIM
    G, N, P = SSD_N_GROUPS, SSD_D_STATE, SSD_HEAD_DIM
    K = H // G
    KP = K * P
    conv_dim = d_inner + 2 * G * N
    in_dim = d_inner + conv_dim + H
    in_pad = d_inner + conv_dim + LANE
    assert H <= LANE and KP % LANE == 0 and L % SSD_CHUNK == 0
    me = 4 * lax.axis_index("x") + 2 * lax.axis_index("y") + lax.axis_index("c")
    x2d, tgt = x[0], loss_target[0]

    c_all = _all_gather(c, "ag_c").reshape(N_DEV, D)
    mods = []
    for l in range(DEPTH):
        ab = lax.dynamic_slice(ada_b[l], (me * (3 * D // N_DEV),), (3 * D // N_DEV,))[None]
        mods.append(_matmul(c_all, ada_w[l], name=f"mod{l}", exact=True, a_silu=True, bias=ab))
    mod_all = _all_gather(jnp.stack(mods), "ag_mod")
    mod_me = lax.dynamic_index_in_dim(jnp.transpose(mod_all, (2, 1, 0, 3)).reshape(N_DEV, DEPTH, 3 * D), me, 0, False)
    shift = [mod_me[l, None, 0:D] for l in range(DEPTH)]
    scale = [mod_me[l, None, D:2 * D] for l in range(DEPTH)]
    gate = [mod_me[l, None, 2 * D:3 * D] for l in range(DEPTH)]

    w_in = _cols_from(_all_gather(a_in_w[0].astype(BF16), "ag_a_in"))
    w_in = jnp.pad(w_in, ((0, 0), (0, in_pad - in_dim)))
    conv_w = _all_gather(a_conv_w[0], "ag_conv_w")
    conv_w = jnp.transpose(conv_w, (1, 0, 2)).reshape(SSD_CONV_W, conv_dim)
    conv_b = _all_gather(a_conv_b, "ag_conv_b").reshape(1, conv_dim)
    norm_g = _all_gather(a_norm_g, "ag_norm_g").reshape(1, d_inner)

    def modulate(xin, l, name):
        fn = lambda xv, sc, sh: ([xv * (1.0 + sc) + sh], [])
        return _rowmap(fn, [(xin, 0, D)], [scale[l], shift[l]], [(D, BF16)], [], name=name)[0]

    def ln_fwd(xin, y, l, name):
        def fn(xv, yv, gt, g, b):
            u = DEEPNORM_ALPHA * xv + (1.0 + gt) * yv
            mu = jnp.mean(u, axis=1, keepdims=True)
            uc = u - mu
            var = jnp.mean(uc * uc, axis=1, keepdims=True)
            o = uc * lax.rsqrt(var + LN_EPS) * g + b
            return [o, o], []
        return _rowmap(fn, [(xin, 0, D), (y, 0, D)], [gate[l], ln_g[l:l + 1], ln_b[l:l + 1]],
                       [(D, F32), (D, BF16)], [], name=name)

    def ln_bwd(xin, y, dout, l, name):
        def fn(xv, yv, do, gt, g, b):
            u = DEEPNORM_ALPHA * xv + (1.0 + gt) * yv
            mu = jnp.mean(u, axis=1, keepdims=True)
            uc = u - mu
            var = jnp.mean(uc * uc, axis=1, keepdims=True)
            rs = lax.rsqrt(var + LN_EPS)
            xh = uc * rs
            dxh = do * g
            du = rs * (dxh - jnp.mean(dxh, axis=1, keepdims=True) - xh * jnp.mean(dxh * xh, axis=1, keepdims=True))
            return [DEEPNORM_ALPHA * du, (1.0 + gt) * du], [_csum(du * yv), _csum(do * xh), _csum(do)]
        return _rowmap(fn, [(xin, 0, D), (y, 0, D), (dout, 0, D)], [gate[l], ln_g[l:l + 1], ln_b[l:l + 1]],
                       [(D, F32), (D, BF16)], [(1, D)] * 3, name=name)

    def mod_bwd(xin, dh, dx_acc, l, name):
        def fn(xv, dhv, dxa, sc):
            return [dxa + dhv * (1.0 + sc)], [_csum(dhv * xv), _csum(dhv)]
        return _rowmap(fn, [(xin, 0, D), (dh, 0, D), (dx_acc, 0, D)], [scale[l]], [(D, F32)], [(1, D)] * 2, name=name)

    h0 = modulate(x2d, 0, "mod_h0")
    proj, g_aout = _matmul(h0, w_in, name="mm_a_in", tn=1152,
                           comm=_ag_comm(a_out_w[0].astype(BF16)))
    w_aout = g_aout.reshape(d_inner, D)
    xbc = _conv_fwd(proj, d_inner, conv_dim, conv_w, conv_b, name="conv_fwd")
    dt_raw = proj[:, d_inner + conv_dim:]
    padh = lambda a: jnp.pad(a, ((0, 0), (0, LANE - H)))
    bias_p, alog_p = padh(a_dt_bias), padh(a_A_log)
    dt_p, a_p = _ssd_prep(dt_raw, bias_p, alog_p, name="ssd_prep")
    dt_c = jnp.transpose(dt_p[:, :H].reshape(L, G, K), (1, 0, 2))
    a_c = jnp.transpose(a_p[:, :H].reshape(L, G, K), (1, 0, 2))
    a_r = jnp.transpose(a_c, (0, 2, 1))
    d_full = jnp.repeat(a_D.reshape(H), P)[None]
    ssd_in = (xbc, dt_c, a_c, a_r, d_full)
    cm_kv, fix_kv = _ag_cols(kv_w.astype(BF16))
    y_ssd, states, w_kv = _ssd_fwd(*ssd_in, d_inner=d_inner, name="ssd_fwd", comm=cm_kv)
    w_kv = fix_kv(w_kv)

    gw = d_inner // G

    def gnorm_fn(yv, zv, g):
        yg = yv * _silu(zv)
        r = lax.rsqrt(jnp.mean(yg * yg, axis=1, keepdims=True) + RMS_EPS)
        return [yg * r * g], []
    yn = _rowmap(gnorm_fn, [(y_ssd, 0, d_inner), (proj, 0, d_inner)], [norm_g], [(d_inner, BF16)], [],
                 name="gnorm_fwd", cw=gw, tr=1024)[0]
    cm_bin, fix_bin = _ag_cols(b_in_w[0].astype(BF16))
    ya, w_bin = _matmul(yn, w_aout, name="mm_a_out", comm=cm_bin)
    w_bin = fix_bin(w_bin)
    x1, x1b = ln_fwd(x2d, ya, 0, "ln0_fwd")

    cm_bout, fix_bout = _ag_cols(b_out_w[0].astype(BF16))
    kv, w_bout = _matmul(x1b, w_kv, name="mm_kv", comm=cm_bout)
    w_bout = fix_bout(w_bout)
    h1 = modulate(x1, 1, "mod_h1")
    qz = _matmul(h1, w_bin, name="mm_b_in")
    z_off = DIL_N_GROUPS * DIL_W
    os_, lses = [], []
    for g in range(DIL_N_GROUPS):
        o, lse = _attn_fwd(qz, kv, g, name=f"attn_fwd{g}")
        os_.append(o)
        lses.append(lse)
    og = _merge_fwd(os_, lses, qz, z_off, name="merge_fwd")
    yb = _matmul(og, w_bout, name="mm_b_out")
    x2, _ = ln_fwd(x1, yb, 1, "ln1_fwd")

    def loss_fn(xv, tv):
        e = xv - tv
        return [e * (1.0 / D)], [_csum(e * e) * (0.5 / D)]
    dx2, loss_cols = _rowmap(loss_fn, [(x2, 0, D), (tgt, 0, D)], [], [(D, F32)], [(1, D)], name="loss")
    loss = lax.psum(jnp.sum(loss_cols), ("x", "y", "c"))

    dx1a, dyb, dgate1, dlng1, dlnb1 = ln_bwd(x1, yb, dx2, 1, "ln1_bwd")
    dw_bout = _matmul(og, dyb, name="mm_b_out_dw", ta=True, out_dtype=BF16)
    dog = _matmul(dyb, w_bout, name="mm_b_out_dx", tb=True, out_dtype=BF16)
    do0, do1, do2, dl0, dl1, dl2, dqz, r_bout = _merge_bwd(os_, lses, qz, z_off, dog, name="merge_bwd",
                                                           comm=_rs_cols(dw_bout))
    dk = dv = None
    for g, (do_g, dl_g) in enumerate(zip((do0, do1, do2), (dl0, dl1, dl2))):
        dqz, dk, dv = _attn_bwd(qz, kv, do_g, lses[g], dl_g, dqz, dk, dv, g, name=f"attn_bwd{g}")
    dw_bin = _matmul(h1, dqz, name="mm_b_in_dw", ta=True, out_dtype=BF16)
    dh1 = _matmul(dqz, w_bin, name="mm_b_in_dx", tb=True)
    dx1b, dscale1, dshift1 = mod_bwd(x1, dh1, dx1a, 1, "mod1_bwd")
    kw = DIL_N_GROUPS * DIL_W
    dw_kv = jnp.concatenate([_matmul(x1b, dk, name="mm_k_dw", ta=True, out_dtype=BF16),
                             _matmul(x1b, dv, name="mm_v_dw", ta=True, out_dtype=BF16)], axis=1)
    dx1k = _matmul(dk, w_kv[:, :kw], name="mm_k_dx", tb=True, bias=dx1b)
    dx1 = _matmul(dv, w_kv[:, kw:], name="mm_v_dx", tb=True, bias=dx1k)

    dxa, dya, dgate0, dlng0, dlnb0 = ln_bwd(x2d, ya, dx1, 0, "ln0_bwd")
    dw_aout = _matmul(yn, dya, name="mm_a_out_dw", ta=True, out_dtype=BF16)
    dyn = _matmul(dya, w_aout, name="mm_a_out_dx", tb=True)

    def gnorm_bwd_fn(yv, zv, dn, g):
        sz = _silu(zv)
        yg = yv * sz
        r = lax.rsqrt(jnp.mean(yg * yg, axis=1, keepdims=True) + RMS_EPS)
        nrm = yg * r
        dnn = dn * g
        dyg = r * (dnn - nrm * jnp.mean(dnn * nrm, axis=1, keepdims=True))
        return [dyg * sz, dyg * yv * _dsilu(zv)], [_csum(dn * nrm)]
    dy_ssd, dz, dnorm_g = _rowmap(gnorm_bwd_fn, [(y_ssd, 0, d_inner), (proj, 0, d_inner), (dyn, 0, d_inner)],
                                  [norm_g], [(d_inner, F32), (d_inner, BF16)], [(1, d_inner)], name="gnorm_bwd", cw=gw,
                                  tr=1024)
    dxs, dB, dC, da_t, dar_t, s1_t, dD_f, r_kv = _ssd_bwd(
        *ssd_in, states, dy_ssd, d_inner=d_inner, name="ssd_bwd", comm=_rs_cols(dw_kv))
    dxbc_raw, dconv_w, dconv_b, r_aout = _conv_bwd(
        proj, d_inner, conv_dim, conv_w, conv_b, (dxs, dB, dC), name="conv_bwd", tc=128,
        comm=_a2a_comm(dw_aout.reshape(N_DEV, d_inner // N_DEV, D)))
    tokp = lambda t: padh(jnp.transpose(t, (1, 0, 2)).reshape(L, H))
    dar_tok = padh(jnp.transpose(dar_t, (2, 0, 1)).reshape(L, H))
    ddt_raw, ddt_bias_p, dA_log_p = _ssd_post(tokp(da_t), dar_tok, tokp(s1_t), dt_p, dt_raw, bias_p, alog_p,
                                              name="ssd_post")
    ddt_bias, dA_log = ddt_bias_p[:, :H], dA_log_p[:, :H]
    dD = jnp.sum(dD_f.reshape(H, P), axis=1)[None]
    dproj = jnp.concatenate([dz, dxbc_raw, ddt_raw], axis=1)
    dw_in, r_bin = _matmul(h0, dproj, name="mm_a_in_dw", ta=True, out_dtype=BF16, tn=1152, comm=_rs_cols(dw_bin))
    dw_in = dw_in[:, :in_dim]
    cs_in = in_dim // N_DEV
    by_c = jnp.transpose(dw_in.reshape(D, N_DEV // 2, 2, cs_in), (2, 1, 0, 3))
    my_c = lax.axis_index("c")
    keep = lax.dynamic_index_in_dim(by_c, my_c, 0, False)
    give = lax.dynamic_index_in_dim(by_c, 1 - my_c, 0, False)
    got = _run_comm(_pair_comm(give), "rs_a_in_pair")[0]
    pair_sum = _rowmap(lambda a, b: ([a.astype(F32) + b.astype(F32)], []),
                       [(keep.reshape(-1, cs_in), 0, cs_in), (got.reshape(-1, cs_in), 0, cs_in)], [],
                       [(cs_in, BF16)], [], name="rs_a_in_add", tr=512)[0].reshape(N_DEV // 2, D, cs_in)
    dh0, r_in = _matmul(dproj, w_in, name="mm_a_in_dx", tb=True, tk=1152,
                        comm=_quad_comm(pair_sum))
    grad_x, dscale0, dshift0 = mod_bwd(x2d, dh0, dxa, 0, "mod0_bwd")

    dmod = jnp.concatenate([dshift0, dscale0, dgate0, dshift1, dscale1, dgate1], axis=1)
    pieces = [dmod, dlng0, dlng1, dlnb0, dlnb1, ddt_bias, dA_log, dD,
              dconv_w.reshape(1, -1), dconv_b, dnorm_g]
    sizes = [p.shape[1] for p in pieces]
    tot = sum(sizes)
    tot_pad = -(-tot // (8 * LANE)) * (8 * LANE)
    packed = jnp.pad(jnp.concatenate(pieces, axis=1), ((0, 0), (0, tot_pad - tot))).reshape(tot_pad // LANE, LANE)
    packed_all = _all_gather(packed, "ag_small")
    small = _sum_parts(packed_all, name="sum_small").reshape(tot_pad)
    offs = np.cumsum([0] + sizes)
    seg = lambda i: small[int(offs[i]):int(offs[i + 1])]
    g_ada_b = seg(0).reshape(DEPTH, 3 * D)
    g_ln_g = jnp.stack([seg(1), seg(2)])
    g_ln_b = jnp.stack([seg(3), seg(4)])
    g_dt_bias, g_A_log, g_D = seg(5)[None], seg(6)[None], seg(7)[None]
    cs = conv_dim // N_DEV
    g_conv_w = lax.dynamic_slice(seg(8).reshape(SSD_CONV_W, conv_dim), (0, me * cs), (SSD_CONV_W, cs))[None]
    g_conv_b = lax.dynamic_slice(seg(9), (me * cs,), (cs,))[None]
    ns = d_inner // N_DEV
    g_norm_g = lax.dynamic_slice(seg(10), (me * ns,), (ns,))[None]

    ms = 3 * D // N_DEV
    dmod_all = packed_all.reshape(N_DEV, tot_pad)[:, :DEPTH * 3 * D].reshape(N_DEV, DEPTH, 3 * D)
    dmod_cols = lax.dynamic_slice(dmod_all, (0, 0, me * ms), (N_DEV, DEPTH, ms))
    c_t = jnp.transpose(c_all)
    g_ada_w = jnp.stack([_matmul(c_t, dmod_cols[:, l], name=f"mm_ada_dw{l}", exact=True, a_silu=True)
                         for l in range(DEPTH)])[None]

    def upd(parts, w, m, v, name):
        shp = w.shape
        r2 = lambda a: a.reshape(-1, shp[-1])
        return [o.reshape(shp) for o in _adamw(parts, r2(w), r2(m), r2(v), name=name)]

    res = {}
    res["ada_w"] = upd(g_ada_w.reshape(1, -1, ms), ada_w, m_ada_w, v_ada_w, "adam_ada_w")
    res["a_in_w"] = upd(r_in, a_in_w, m_a_in_w, v_a_in_w, "adam_a_in")
    res["a_out_w"] = upd(r_aout, a_out_w, m_a_out_w, v_a_out_w, "adam_a_out")
    res["kv_w"] = upd(r_kv, kv_w, m_kv_w, v_kv_w, "adam_kv")
    res["b_in_w"] = upd(r_bin, b_in_w, m_b_in_w, v_b_in_w, "adam_b_in")
    res["b_out_w"] = upd(r_bout, b_out_w, m_b_out_w, v_b_out_w, "adam_b_out")

    small_names = ["ada_b", "ln_g", "ln_b", "a_conv_w", "a_conv_b", "a_dt_bias", "a_A_log", "a_D", "a_norm_g"]
    small_g = [g_ada_b, g_ln_g, g_ln_b, g_conv_w, g_conv_b, g_dt_bias, g_A_log, g_D, g_norm_g]
    small_w = [ada_b, ln_g, ln_b, a_conv_w, a_conv_b, a_dt_bias, a_A_log, a_D, a_norm_g]
    small_m = [m_ada_b, m_ln_g, m_ln_b, m_a_conv_w, m_a_conv_b, m_a_dt_bias, m_a_A_log, m_a_D, m_a_norm_g]
    small_v = [v_ada_b, v_ln_g, v_ln_b, v_a_conv_w, v_a_conv_b, v_a_dt_bias, v_a_A_log, v_a_D, v_a_norm_g]
    ssz = [int(np.prod(w.shape)) for w in small_w]
    stot = sum(ssz)
    spad = -(-stot // (8 * LANE)) * (8 * LANE)

    def pack(arrs, fill):
        flat = jnp.concatenate([a.reshape(-1) for a in arrs])
        return jnp.concatenate([flat, jnp.full((spad - stot,), fill, F32)]).reshape(spad // LANE, LANE)

    sres = _adamw(pack(small_g, 0.0)[None], pack(small_w, 0.0), pack(small_m, 0.0), pack(small_v, 1.0), name="adam_small")
    soffs = np.cumsum([0] + ssz)
    for i, nme in enumerate(small_names):
        res[nme] = [r.reshape(-1)[int(soffs[i]):int(soffs[i + 1])].reshape(small_w[i].shape) for r in sres]

    order = ["ada_w", "ada_b", "ln_g", "ln_b", "a_in_w", "a_conv_w", "a_conv_b", "a_dt_bias", "a_A_log", "a_D",
             "a_norm_g", "a_out_w", "kv_w", "b_in_w", "b_out_w"]
    outs = [loss, grad_x[None]]
    for j in range(4):
        outs += [res[nme][j] for nme in order]
    return tuple(outs)
```

```python
import functools
import math

import numpy as np
import jax
import jax.numpy as jnp
from jax import lax
from jax.experimental import pallas as pl
from jax.experimental.pallas import tpu as pltpu

F32, BF16 = jnp.float32, jnp.bfloat16
HI = lax.Precision.HIGHEST
MESH = pl.DeviceIdType.MESH
N_DEV = 8

SSD_HEAD_DIM = 64
SSD_N_GROUPS = 8
SSD_D_STATE = 128
SSD_CONV_W = 4
SSD_CHUNK = 256
DIL_PATTERNS = ((128, 1), (512, 4), (2048, 16))
DIL_N_GROUPS = 3
DIL_HEADS = 8
DIL_HEAD_DIM = 128
DIL_BLOCK = 128
DIL_W = DIL_HEADS * DIL_HEAD_DIM
DEPTH = 2
DEEPNORM_ALPHA = (2 * DEPTH) ** 0.25
LN_EPS = 1e-5
RMS_EPS = 1e-5
ADAM_LR, ADAM_B1, ADAM_B2, ADAM_EPS, ADAM_WD, ADAM_STEP = 0.001, 0.9, 0.999, 1e-08, 0.01, 10
LANE = 128
NEG = -1e30
VMEM_LIMIT = 56 * 1024 * 1024


def _cp(sem=None):
    return pltpu.CompilerParams(dimension_semantics=sem, vmem_limit_bytes=VMEM_LIMIT)


def _silu(x):
    return x * jax.nn.sigmoid(x)


def _dsilu(x):
    s = jax.nn.sigmoid(x)
    return s * (1.0 + x * (1.0 - s))


def _softplus(x):
    return jnp.maximum(x, 0.0) + jnp.log(1.0 + jnp.exp(-jnp.abs(x)))


def _nt(a, b):
    return lax.dot_general(a, b, (((1,), (1,)), ((), ())), preferred_element_type=F32)


def _nn(a, b):
    return jnp.dot(a, b, preferred_element_type=F32)


def _hi(a, b):
    return jnp.dot(a, b, preferred_element_type=F32, precision=HI)


def _pick(n, pref, align=LANE):
    if n <= pref:
        return n
    for t in range(pref - pref % align, 0, -align):
        if n % t == 0:
            return t
    return n


class _Comm:
    def __init__(self, ins, outs, sems, start, finish):
        self.ins, self.outs, self.sems, self.start, self.finish = ins, outs, sems, start, finish


def _comm_join(comms):
    ins = [a for c in comms for a in c.ins]
    outs = [a for c in comms for a in c.outs]
    sems = [a for c in comms for a in c.sems]

    def split(refs, attr):
        res, i = [], 0
        for c in comms:
            n = len(getattr(c, attr))
            res.append(refs[i:i + n])
            i += n
        return res

    def start(cin, cout, csem):
        for c, a, b, d in zip(comms, split(cin, "ins"), split(cout, "outs"), split(csem, "sems")):
            c.start(a, b, d)

    def finish(cin, cout, csem):
        for c, a, b, d in zip(comms, split(cin, "ins"), split(cout, "outs"), split(csem, "sems")):
            c.finish(a, b, d)

    return _Comm(ins, outs, sems, start, finish)


def _ag_comm(v, cols=False):
    if cols:
        R, Cs = v.shape
        assert Cs % LANE == 0
        out_sd = jax.ShapeDtypeStruct((R, N_DEV * Cs), v.dtype)
    else:
        out_sd = jax.ShapeDtypeStruct((N_DEV,) + v.shape, v.dtype)

    def parts(x_ref, out_ref, send_sems, recv_sems, local_sem):
        x, y, c = lax.axis_index("x"), lax.axis_index("y"), lax.axis_index("c")
        me, sibling = (x, y, c), (x, y, 1 - c)
        chips = [(1 - x, y), (x, 1 - y), (1 - x, 1 - y)]

        def slab(px, py, pc):
            k = 4 * px + 2 * py + pc
            if cols:
                return out_ref.at[:, pl.ds(pl.multiple_of(k * Cs, LANE), Cs)]
            return out_ref.at[k]

        def copy(k, block, to, src=None):
            return pltpu.make_async_remote_copy(
                src_ref=slab(*block) if src is None else src, dst_ref=slab(*block),
                send_sem=send_sems.at[k], recv_sem=recv_sems.at[k], device_id=to, device_id_type=MESH)

        mine = pltpu.make_async_copy(x_ref, slab(*me), local_sem)
        first = [copy(0, me, sibling, src=x_ref)]
        first += [copy(1 + j, me, (*chip, c), src=x_ref) for j, chip in enumerate(chips)]
        passed = [copy(4 + j, (*chip, c), sibling) for j, chip in enumerate(chips)]
        return me, sibling, chips, c, copy, mine, first, passed

    def start(cin, cout, csem):
        _, _, _, _, _, mine, first, _ = parts(cin[0], cout[0], *csem)
        mine.start()
        for cp in first:
            cp.start()

    def finish(cin, cout, csem):
        me, sibling, chips, c, copy, mine, first, passed = parts(cin[0], cout[0], *csem)
        for j, chip in enumerate(chips):
            copy(1 + j, (*chip, c), me).wait_recv()
            passed[j].start()
        copy(0, sibling, me).wait_recv()
        for j, chip in enumerate(chips):
            copy(4 + j, (*chip, 1 - c), me).wait_recv()
        for cp in first + passed:
            cp.wait_send()
        mine.wait()

    return _Comm([v], [out_sd],
                 [pltpu.SemaphoreType.DMA((7,)), pltpu.SemaphoreType.DMA((7,)), pltpu.SemaphoreType.DMA], start, finish)


def _a2a_comm(v, cols=False):
    if cols:
        R, C = v.shape
        Cs = C // N_DEV
        assert Cs % LANE == 0
        out_sd = jax.ShapeDtypeStruct((N_DEV, R, Cs), v.dtype)
    else:
        out_sd = jax.ShapeDtypeStruct(v.shape, v.dtype)

    def parts(x_ref, out_ref, send_sems, recv_sems, local_sem):
        x, y, c = lax.axis_index("x"), lax.axis_index("y"), lax.axis_index("c")
        me = 4 * x + 2 * y + c

        def src(k):
            if cols:
                return x_ref.at[:, pl.ds(pl.multiple_of(k * Cs, LANE), Cs)]
            return x_ref.at[k]

        mine = pltpu.make_async_copy(src(me), out_ref.at[me], local_sem)
        sends, recvs = [], []
        for k, mask in enumerate(range(1, N_DEV)):
            px = 1 - x if (mask >> 2) & 1 else x
            py = 1 - y if (mask >> 1) & 1 else y
            pc = 1 - c if mask & 1 else c
            peer = 4 * px + 2 * py + pc
            sends.append(pltpu.make_async_remote_copy(
                src_ref=src(peer), dst_ref=out_ref.at[me],
                send_sem=send_sems.at[k], recv_sem=recv_sems.at[k], device_id=(px, py, pc), device_id_type=MESH))
            recvs.append(pltpu.make_async_remote_copy(
                src_ref=src(me), dst_ref=out_ref.at[peer],
                send_sem=send_sems.at[k], recv_sem=recv_sems.at[k], device_id=(px, py, pc), device_id_type=MESH))
        return mine, sends, recvs

    def start(cin, cout, csem):
        mine, sends, _ = parts(cin[0], cout[0], *csem)
        mine.start()
        for cp in sends:
            cp.start()

    def finish(cin, cout, csem):
        mine, sends, recvs = parts(cin[0], cout[0], *csem)
        for cp in recvs:
            cp.wait_recv()
        for cp in sends:
            cp.wait_send()
        mine.wait()

    return _Comm([v], [out_sd],
                 [pltpu.SemaphoreType.DMA((7,)), pltpu.SemaphoreType.DMA((7,)), pltpu.SemaphoreType.DMA], start, finish)


def _pair_comm(v4):
    def copy(x_ref, out_ref, send_sem, recv_sem):
        x, y, c = lax.axis_index("x"), lax.axis_index("y"), lax.axis_index("c")
        return pltpu.make_async_remote_copy(src_ref=x_ref, dst_ref=out_ref, send_sem=send_sem, recv_sem=recv_sem,
                                            device_id=(x, y, 1 - c), device_id_type=MESH)

    def start(cin, cout, csem):
        copy(cin[0], cout[0], *csem).start()

    def finish(cin, cout, csem):
        copy(cin[0], cout[0], *csem).wait()

    return _Comm([v4], [jax.ShapeDtypeStruct(v4.shape, v4.dtype)],
                 [pltpu.SemaphoreType.DMA, pltpu.SemaphoreType.DMA], start, finish)


def _quad_comm(v4):
    def parts(x_ref, out_ref, send_sems, recv_sems, local_sem):
        x, y, c = lax.axis_index("x"), lax.axis_index("y"), lax.axis_index("c")
        me = 2 * x + y
        mine = pltpu.make_async_copy(x_ref.at[me], out_ref.at[me], local_sem)
        sends, recvs = [], []
        for k, mask in enumerate(range(1, 4)):
            px = 1 - x if (mask >> 1) & 1 else x
            py = 1 - y if mask & 1 else y
            peer = 2 * px + py
            sends.append(pltpu.make_async_remote_copy(
                src_ref=x_ref.at[peer], dst_ref=out_ref.at[me],
                send_sem=send_sems.at[k], recv_sem=recv_sems.at[k], device_id=(px, py, c), device_id_type=MESH))
            recvs.append(pltpu.make_async_remote_copy(
                src_ref=x_ref.at[me], dst_ref=out_ref.at[peer],
                send_sem=send_sems.at[k], recv_sem=recv_sems.at[k], device_id=(px, py, c), device_id_type=MESH))
        return mine, sends, recvs

    def start(cin, cout, csem):
        mine, sends, _ = parts(cin[0], cout[0], *csem)
        mine.start()
        for cp in sends:
            cp.start()

    def finish(cin, cout, csem):
        mine, sends, recvs = parts(cin[0], cout[0], *csem)
        for cp in recvs:
            cp.wait_recv()
        for cp in sends:
            cp.wait_send()
        mine.wait()

    return _Comm([v4], [jax.ShapeDtypeStruct(v4.shape, v4.dtype)],
                 [pltpu.SemaphoreType.DMA((3,)), pltpu.SemaphoreType.DMA((3,)), pltpu.SemaphoreType.DMA], start, finish)


def _run_comm(comm, name):
    nci, nco = len(comm.ins), len(comm.outs)

    def body(*refs):
        comm.start(refs[:nci], refs[nci:nci + nco], refs[nci + nco:])
        comm.finish(refs[:nci], refs[nci:nci + nco], refs[nci + nco:])

    anyspec = pl.BlockSpec(memory_space=pl.ANY)
    return pl.pallas_call(body, name=name, out_shape=list(comm.outs), in_specs=[anyspec] * nci,
                          out_specs=[anyspec] * nco, scratch_shapes=list(comm.sems))(*comm.ins)


def _all_gather(v, name):
    return _run_comm(_ag_comm(v), name)[0]


def _pcall(body, args, *, name, grid, in_specs, out_specs, out_shape, scratch=(), sem=None, comm=None, aliases=None):
    out_shape, out_specs = list(out_shape), list(out_specs)
    aliases = dict(aliases or {})
    if comm is None:
        return pl.pallas_call(body, name=name, grid=grid, in_specs=list(in_specs), out_specs=out_specs,
                              out_shape=out_shape, scratch_shapes=list(scratch), input_output_aliases=aliases,
                              compiler_params=_cp(sem))(*args)
    ni, no, ns = len(args), len(out_shape), len(scratch)
    nci, nco = len(comm.ins), len(comm.outs)

    def wrapped(*refs):
        ins, cin = refs[:ni], refs[ni:ni + nci]
        o0 = ni + nci
        outs, cout = refs[o0:o0 + no], refs[o0 + no:o0 + no + nco]
        s0 = o0 + no + nco
        scr, csem = refs[s0:s0 + ns], refs[s0 + ns:]
        first = functools.reduce(jnp.logical_and, [pl.program_id(a) == 0 for a in range(len(grid))])
        last = functools.reduce(jnp.logical_and, [pl.program_id(a) == g - 1 for a, g in enumerate(grid)])

        @pl.when(first)
        def _():
            comm.start(cin, cout, csem)

        body(*ins, *outs, *scr)

        @pl.when(last)
        def _():
            comm.finish(cin, cout, csem)

    anyspec = pl.BlockSpec(memory_space=pl.ANY)
    res = pl.pallas_call(
        wrapped, name=name, grid=grid, in_specs=list(in_specs) + [anyspec] * nci,
        out_specs=out_specs + [anyspec] * nco, out_shape=out_shape + list(comm.outs),
        scratch_shapes=list(scratch) + list(comm.sems), input_output_aliases=aliases,
        compiler_params=_cp(("arbitrary",) * len(grid)))(*args, *comm.ins)
    return list(res[:no]) + list(res[no:])


def _matmul(a, b, *, name, ta=False, tb=False, out_dtype=F32, tm=1024, tn=1024, tk=2048,
            exact=False, a_silu=False, bias=None, comm=None, b_koff=0, out_cols=None, out_buf=None):
    (K, M) = a.shape if ta else a.shape[::-1]
    (N, K2) = b.shape if tb else b.shape[::-1]
    assert K == K2 or (tb and K + b_koff <= K2), (a.shape, b.shape, ta, tb)
    tm, tn, tk = _pick(M, tm), _pick(N, tn), _pick(K, tk)
    nk = K // tk
    assert b_koff % tk == 0
    ko = b_koff // tk
    a_spec = pl.BlockSpec((tk, tm), lambda i, j, k: (k, i)) if ta else pl.BlockSpec((tm, tk), lambda i, j, k: (i, k))
    b_spec = pl.BlockSpec((tn, tk), lambda i, j, k: (j, k + ko)) if tb else pl.BlockSpec((tk, tn), lambda i, j, k: (k, j))
    dims = (((0,) if ta else (1,), (1,) if tb else (0,)), ((), ()))
    in_specs, args = [a_spec, b_spec], [a, b]
    if bias is not None:
        if bias.shape[0] == 1:
            in_specs.append(pl.BlockSpec((1, tn), lambda i, j, k: (0, j)))
        else:
            in_specs.append(pl.BlockSpec((tm, tn), lambda i, j, k: (i, j)))
        args.append(bias)

    aliases = {}
    if out_buf is not None:
        in_specs.append(pl.BlockSpec(memory_space=pl.ANY))
        args.append(out_buf)
        aliases = {len(args) - 1: 0}
    n_in = len(args)
    width, off = out_cols if out_cols is not None else (N, 0)
    assert off % tn == 0

    def body(*refs):
        a_ref, b_ref = refs[0], refs[1]
        bias_ref = refs[2] if bias is not None else None
        o_ref = refs[n_in]
        av, bv = a_ref[...], b_ref[...]
        if a_silu:
            av = _silu(av.astype(F32))
        if exact:
            p = lax.dot_general(av.astype(F32), bv.astype(F32), dims, preferred_element_type=F32, precision=HI)
        else:
            p = lax.dot_general(av.astype(BF16), bv.astype(BF16), dims, preferred_element_type=F32)

        def fin(r):
            if bias_ref is not None:
                r = r + bias_ref[...]
            o_ref[...] = r.astype(o_ref.dtype)

        if nk == 1:
            fin(p)
        else:
            acc = refs[-1]
            k = pl.program_id(2)

            @pl.when(k == 0)
            def _():
                acc[...] = p

            @pl.when(k > 0)
            def _():
                acc[...] += p

            @pl.when(k == nk - 1)
            def _():
                fin(acc[...])

    res = _pcall(
        body, args, name=name,
        out_shape=[jax.ShapeDtypeStruct((M, width), out_dtype)],
        grid=(M // tm, N // tn, nk),
        in_specs=in_specs,
        out_specs=[pl.BlockSpec((tm, tn), lambda i, j, k: (i, j + off // tn))],
        scratch=[pltpu.VMEM((tm, tn), F32)] if nk > 1 else [],
        sem=("parallel", "parallel", "arbitrary"), comm=comm, aliases=aliases)
    return res[0] if comm is None else res


def _rowmap(fn, rows, bcasts, outs, accs, *, name, tr=256, cw=None, comm=None):
    L = rows[0][0].shape[0]
    tr = _pick(L, tr)
    nr, nb, no, na = len(rows), len(bcasts), len(outs), len(accs)
    if cw is None:
        ncol = 1
        widths = [w for (_, _, w) in rows]
    else:
        wtot = rows[0][2]
        ncol = wtot // cw
        widths = [cw] * nr
    in_specs, args = [], []
    for (arr, off, w), bw in zip(rows, widths):
        assert off % bw == 0
        in_specs.append(pl.BlockSpec((tr, bw), functools.partial(lambda j, i, o: (i, o + j), o=off // bw)))
        args.append(arr)
    for arr in bcasts:
        bw = arr.shape[1] if cw is None else cw
        in_specs.append(pl.BlockSpec((arr.shape[0], bw), lambda j, i: (0, j)))
        args.append(arr)
    out_shape, out_specs = [], []
    for spec in outs:
        if len(spec) == 2:
            (w, dt), off = spec, 0
            bw = w if cw is None else cw
        else:
            w, dt, off, bw = spec
        out_shape.append(jax.ShapeDtypeStruct((L, w), dt))
        out_specs.append(pl.BlockSpec((tr, bw), functools.partial(lambda j, i, o: (i, o + j), o=off // bw)))
    for (r, w) in accs:
        bw = w if cw is None else cw
        out_shape.append(jax.ShapeDtypeStruct((r, w), F32))
        out_specs.append(pl.BlockSpec((r, bw), lambda j, i: (0, j)))

    def body(*refs):
        ins = [r[...] for r in refs[:nr + nb]]
        o_refs = refs[nr + nb:nr + nb + no]
        a_refs = refs[nr + nb + no:]
        o, a = fn(*ins)
        for ref, val in zip(o_refs, o):
            ref[...] = val.astype(ref.dtype)
        if na:
            @pl.when(pl.program_id(1) == 0)
            def _():
                for ref in a_refs:
                    ref[...] = jnp.zeros_like(ref)

            for ref, val in zip(a_refs, a):
                ref[...] += val

    return _pcall(body, args, name=name, out_shape=out_shape, grid=(ncol, L // tr), in_specs=in_specs,
                  out_specs=out_specs, sem=("parallel", "arbitrary"), comm=comm)


def _csum(v):
    return jnp.sum(v, axis=0, keepdims=True)


def _shift_rows(v, s, rows):
    if s == 0:
        return v
    n = v.shape[0]
    r = pltpu.roll(v, s % n, 0)
    if s > 0:
        return jnp.where(rows >= s, r, 0.0)
    return jnp.where(rows < n + s, r, 0.0)


def _conv_fwd(proj, off, width, w, b, *, name, tc=256):
    L = proj.shape[0]
    tc = _pick(width, tc)

    def body(x_ref, w_ref, b_ref, o_ref):
        x = x_ref[...]
        rows = lax.broadcasted_iota(jnp.int32, x.shape, 0)
        acc = jnp.zeros_like(x) + b_ref[...]
        for k in range(SSD_CONV_W):
            acc = acc + w_ref[k:k + 1, :] * _shift_rows(x, SSD_CONV_W - 1 - k, rows)
        o_ref[...] = _silu(acc)

    return pl.pallas_call(
        body, name=name, out_shape=jax.ShapeDtypeStruct((L, width), F32), grid=(width // tc,),
        in_specs=[pl.BlockSpec((L, tc), functools.partial(lambda j, o: (0, o + j), o=off // tc)),
                  pl.BlockSpec((SSD_CONV_W, tc), lambda j: (0, j)), pl.BlockSpec((1, tc), lambda j: (0, j))],
        out_specs=pl.BlockSpec((L, tc), lambda j: (0, j)),
        compiler_params=_cp(("parallel",)),
    )(proj, w, b)


def _conv_bwd(proj, off, width, w, b, dys, dproj, *, name, tc=256, comm=None):
    L = proj.shape[0]
    tc = _pick(width, tc)
    ntile = [d.shape[1] // tc for d in dys]
    assert all(d.shape[1] % tc == 0 for d in dys) and sum(ntile) == width // tc
    first = [sum(ntile[:i]) for i in range(len(dys))]

    def body(x_ref, w_ref, b_ref, *rest):
        dy_refs, (dx_ref, dw_ref, db_ref) = rest[:len(dys)], rest[len(dys) + 1:]
        j = pl.program_id(0)
        dy = dy_refs[0][...]
        for i in range(1, len(dys)):
            dy = jnp.where(j >= first[i], dy_refs[i][...], dy)
        x = x_ref[...]
        rows = lax.broadcasted_iota(jnp.int32, x.shape, 0)
        xs = [_shift_rows(x, SSD_CONV_W - 1 - k, rows) for k in range(SSD_CONV_W)]
        pre = jnp.zeros_like(x) + b_ref[...]
        for k in range(SSD_CONV_W):
            pre = pre + w_ref[k:k + 1, :] * xs[k]
        dpre = dy * _dsilu(pre)
        dx = jnp.zeros_like(x)
        for k in range(SSD_CONV_W):
            dx = dx + w_ref[k:k + 1, :] * _shift_rows(dpre, -(SSD_CONV_W - 1 - k), rows)
            dw_ref[k:k + 1, :] = _csum(dpre * xs[k])
        dx_ref[...] = dx.astype(dx_ref.dtype)
        db_ref[...] = _csum(dpre)

    dy_specs = [pl.BlockSpec((L, tc), functools.partial(lambda j, f, n: (0, jnp.clip(j - f, 0, n - 1)), f=f, n=n))
                for f, n in zip(first, ntile)]
    shifted = pl.BlockSpec((L, tc), functools.partial(lambda j, o: (0, o + j), o=off // tc))
    return _pcall(
        body, (proj, w, b, *dys, dproj), name=name,
        out_shape=[jax.ShapeDtypeStruct(dproj.shape, BF16), jax.ShapeDtypeStruct((SSD_CONV_W, width), F32),
                   jax.ShapeDtypeStruct((1, width), F32)],
        grid=(width // tc,),
        in_specs=[shifted, pl.BlockSpec((SSD_CONV_W, tc), lambda j: (0, j)), pl.BlockSpec((1, tc), lambda j: (0, j))]
        + dy_specs + [pl.BlockSpec(memory_space=pl.ANY)],
        out_specs=[shifted, pl.BlockSpec((SSD_CONV_W, tc), lambda j: (0, j)), pl.BlockSpec((1, tc), lambda j: (0, j))],
        sem=("parallel",), comm=comm, aliases={3 + len(dys): 0})


def _tri(Q):
    ri = lax.broadcasted_iota(jnp.int32, (Q, Q), 0)
    ci = lax.broadcasted_iota(jnp.int32, (Q, Q), 1)
    return ri >= ci, ri <= ci


def _ssd_prep(dt_raw, bias, alog, *, name):
    L, W = dt_raw.shape
    Q = SSD_CHUNK

    def body(r_ref, b_ref, al_ref, dt_ref, a_ref):
        lower, _ = _tri(Q)
        dt = _softplus(r_ref[...] + b_ref[...])
        dt_ref[...] = dt
        a_ref[...] = _hi(lower.astype(F32), dt * (-jnp.exp(al_ref[...])))

    blk = pl.BlockSpec((Q, W), lambda c: (c, 0))
    one = pl.BlockSpec((1, W), lambda c: (0, 0))
    sd = jax.ShapeDtypeStruct((L, W), F32)
    return _pcall(body, (dt_raw, bias, alog), name=name, out_shape=[sd, sd], grid=(L // Q,),
                  in_specs=[blk, one, one], out_specs=[blk, blk], sem=("parallel",))


def _ssd_post(da, dar, s1, dt, dt_raw, bias, alog, dproj, col_off, *, name):
    L, W = da.shape
    Q = SSD_CHUNK

    def body(da_ref, dar_ref, s1_ref, dt_ref, r_ref, b_ref, al_ref, buf_ref, o_ref, db_ref, dal_ref):
        _, upper = _tri(Q)
        A = -jnp.exp(al_ref[...])
        ddtA = _hi(upper.astype(F32), da_ref[...] - dar_ref[...])
        ddt_raw = (ddtA * A + s1_ref[...]) * jax.nn.sigmoid(r_ref[...] + b_ref[...])
        o_ref[...] = ddt_raw.astype(o_ref.dtype)

        @pl.when(pl.program_id(0) == 0)
        def _():
            db_ref[...] = jnp.zeros_like(db_ref)
            dal_ref[...] = jnp.zeros_like(dal_ref)

        db_ref[...] += _csum(ddt_raw)
        dal_ref[...] += _csum(ddtA * dt_ref[...]) * A

    blk = pl.BlockSpec((Q, W), lambda c: (c, 0))
    one = pl.BlockSpec((1, W), lambda c: (0, 0))
    return _pcall(body, (da, dar, s1, dt, dt_raw, bias, alog, dproj), name=name,
                  out_shape=[jax.ShapeDtypeStruct(dproj.shape, dproj.dtype), jax.ShapeDtypeStruct((1, W), F32),
                             jax.ShapeDtypeStruct((1, W), F32)],
                  grid=(L // Q,),
                  in_specs=[blk, blk, blk, blk, blk, one, one, pl.BlockSpec(memory_space=pl.ANY)],
                  out_specs=[pl.BlockSpec((Q, W), lambda c: (c, col_off // W)), one, one], aliases={7: 0},
                  sem=("arbitrary",))


def _head_sum(v, K, KP):
    P = KP // K
    t_r = lax.broadcasted_iota(jnp.int32, (KP, K), 0)
    t_c = lax.broadcasted_iota(jnp.int32, (KP, K), 1)
    Et = ((t_r >= t_c * P) & (t_r < (t_c + 1) * P)).astype(BF16)
    hi = v.astype(BF16)
    lo = (v - hi.astype(F32)).astype(BF16)
    return _nn(hi, Et) + _nn(lo, Et)


def _half_masks():
    li = lax.broadcasted_iota(jnp.int32, (1, LANE), 1)
    return [(li < SSD_HEAD_DIM).astype(F32), (li >= SSD_HEAD_DIM).astype(F32)]


def _ssd_specs(K, KP, d_inner, rev, nc):
    Q, N, G = SSD_CHUNK, SSD_D_STATE, SSD_N_GROUPS
    cidx = (lambda c: nc - 1 - c) if rev else (lambda c: c)
    b_off, c_off = d_inner // N, d_inner // N + G
    return [
        pl.BlockSpec((Q, KP), lambda g, c: (cidx(c), g)),
        pl.BlockSpec((Q, N), lambda g, c: (cidx(c), b_off + g)),
        pl.BlockSpec((Q, N), lambda g, c: (cidx(c), c_off + g)),
        pl.BlockSpec((None, Q, K), lambda g, c: (g, cidx(c), 0)),
        pl.BlockSpec((None, Q, K), lambda g, c: (g, cidx(c), 0)),
        pl.BlockSpec((None, K, Q), lambda g, c: (g, 0, cidx(c))),
        pl.BlockSpec((1, KP), lambda g, c: (0, g)),
    ]


def _expand_heads(vc, K):
    Q = vc.shape[0]
    left = lax.broadcasted_iota(jnp.int32, (Q, LANE), 1) < SSD_HEAD_DIM
    parts = []
    for pr in range(K // 2):
        a = jnp.broadcast_to(vc[:, 2 * pr:2 * pr + 1], (Q, LANE))
        b = jnp.broadcast_to(vc[:, 2 * pr + 1:2 * pr + 2], (Q, LANE))
        parts.append(jnp.where(left, a, b))
    return jnp.concatenate(parts, axis=1) if len(parts) > 1 else parts[0]


def _ssd_fwd(xbc, dt_c, a_c, a_r, d_full, *, d_inner, name, comm=None):
    L = xbc.shape[0]
    G, N, Q, P = SSD_N_GROUPS, SSD_D_STATE, SSD_CHUNK, SSD_HEAD_DIM
    KP = d_inner // G
    K = KP // P
    nc = L // Q
    npair = KP // LANE

    def body(xs_ref, b_ref, c_ref, dtc_ref, ac_ref, ar_ref, df_ref, y_ref, st_ref, S):
        @pl.when(pl.program_id(1) == 0)
        def _():
            S[...] = jnp.zeros_like(S)

        lower, _ = _tri(Q)
        st_ref[...] = S[...]
        xs = xs_ref[...]
        Bm, Cm = b_ref[...], c_ref[...]
        Bb, Cb = Bm.astype(BF16), Cm.astype(BF16)
        a_c, a_r = ac_ref[...], ar_ref[...]
        a_f = _expand_heads(a_c, K)
        X = xs * _expand_heads(dtc_ref[...], K)
        ea = jnp.exp(a_f)
        alast = a_f[Q - 1:Q, :]
        tail = jnp.exp(alast - a_f)
        cb = _nt(Cb, Bb)
        Sv = S[...]
        yoff = _nn(Cb, Sv.astype(BF16)) * ea
        skip = xs * df_ref[...]
        masks = _half_masks()
        for pr in range(npair):
            Xp = X[:, pr * LANE:(pr + 1) * LANE]
            acc = yoff[:, pr * LANE:(pr + 1) * LANE] + skip[:, pr * LANE:(pr + 1) * LANE]
            for hh in range(2):
                k = 2 * pr + hh
                seg = a_c[:, k:k + 1] - a_r[k:k + 1, :]
                dec = jnp.where(lower, jnp.exp(jnp.minimum(seg, 0.0)), 0.0)
                acc = acc + _nn((cb * dec).astype(BF16), (Xp * masks[hh]).astype(BF16))
            y_ref[:, pr * LANE:(pr + 1) * LANE] = acc
        Bt = Bm.T
        S[...] = Sv * jnp.exp(alast) + _nn(Bt.astype(BF16), (X * tail).astype(BF16))

    return _pcall(
        body, (xbc, xbc, xbc, dt_c, a_c, a_r, d_full), name=name,
        out_shape=[jax.ShapeDtypeStruct((L, d_inner), F32), jax.ShapeDtypeStruct((G, nc, N, KP), F32)],
        grid=(G, nc),
        in_specs=_ssd_specs(K, KP, d_inner, False, nc),
        out_specs=[pl.BlockSpec((Q, KP), lambda g, c: (c, g)), pl.BlockSpec((None, None, N, KP), lambda g, c: (g, c, 0, 0))],
        scratch=[pltpu.VMEM((N, KP), F32)],
        sem=("parallel", "arbitrary"), comm=comm)


def _ssd_bwd(xbc, dt_c, a_c, a_r, d_full, states, dy, *, d_inner, name, comm=None):
    L = xbc.shape[0]
    G, N, Q, P = SSD_N_GROUPS, SSD_D_STATE, SSD_CHUNK, SSD_HEAD_DIM
    KP = d_inner // G
    K = KP // P
    nc = L // Q
    npair = KP // LANE

    def body(xs_ref, b_ref, c_ref, dtc_ref, ac_ref, ar_ref, df_ref, st_ref, dy_ref,
             dxs_ref, db_ref, dc_ref, da_ref, dar_ref, s1_ref, dd_ref, dS):
        @pl.when(pl.program_id(1) == 0)
        def _():
            dS[...] = jnp.zeros_like(dS)
            dd_ref[...] = jnp.zeros_like(dd_ref)

        lower, _ = _tri(Q)
        a_c, a_r = ac_ref[...], ar_ref[...]
        a_f, dt_f = _expand_heads(a_c, K), _expand_heads(dtc_ref[...], K)
        xs = xs_ref[...]
        Bm, Cm = b_ref[...], c_ref[...]
        Bb, Cb = Bm.astype(BF16), Cm.astype(BF16)
        dY = dy_ref[...]
        X = xs * dt_f
        ea = jnp.exp(a_f)
        alast = a_f[Q - 1:Q, :]
        tail = jnp.exp(alast - a_f)
        el = jnp.exp(alast)
        Sv, dSn = st_ref[...], dS[...]
        Sb, dSb = Sv.astype(BF16), dSn.astype(BF16)
        cb = _nt(Cb, Bb)
        yoff_raw = _nn(Cb, Sb)
        dYe = dY * ea
        dC = _nt(dYe.astype(BF16), Sb)
        dS[...] = dSn * el + _nn(Cm.T.astype(BF16), dYe.astype(BF16))
        Gx = _nn(Bb, dSb)
        dB = _nt((X * tail).astype(BF16), dSb)
        dtl = Gx * X * tail
        da_f = dYe * yoff_raw - dtl
        dalast_f = _csum(dtl) + _csum(dSn * Sv) * el
        da_c = _head_sum(da_f, K, KP)
        onek = lax.broadcasted_iota(jnp.int32, (1, K), 1)
        onek_col = lax.broadcasted_iota(jnp.int32, (K, 1), 0)
        masks = _half_masks()
        dcb = jnp.zeros((Q, Q), F32)
        da_r = jnp.zeros((K, Q), F32)
        dX_parts = []
        for pr in range(npair):
            Xp = X[:, pr * LANE:(pr + 1) * LANE]
            dYp = dY[:, pr * LANE:(pr + 1) * LANE]
            dXp = Gx[:, pr * LANE:(pr + 1) * LANE] * tail[:, pr * LANE:(pr + 1) * LANE]
            for hh in range(2):
                k = 2 * pr + hh
                Xk = (Xp * masks[hh]).astype(BF16)
                dYk = (dYp * masks[hh]).astype(BF16)
                seg = a_c[:, k:k + 1] - a_r[k:k + 1, :]
                dec = jnp.where(lower, jnp.exp(jnp.minimum(seg, 0.0)), 0.0)
                Mk = cb * dec
                dM = _nt(dYk, Xk)
                dcb = dcb + dM * dec
                Gk = dM * Mk
                da_c = da_c + jnp.sum(Gk, axis=1, keepdims=True) * (onek == k).astype(F32)
                da_r = da_r + (onek_col == k).astype(F32) * jnp.sum(Gk, axis=0, keepdims=True)
                dXp = dXp + _tn(Mk.astype(BF16), dYk)
            dX_parts.append(dXp)
        dX = jnp.concatenate(dX_parts, axis=1) if npair > 1 else dX_parts[0]
        dcbb = dcb.astype(BF16)
        dC = dC + _nn(dcbb, Bb)
        dB = dB + _tn(dcbb, Cb)
        lastrow = (lax.broadcasted_iota(jnp.int32, (Q, 1), 0) == Q - 1).astype(F32)
        da_ref[...] = da_c + lastrow * _head_sum(dalast_f, K, KP)
        dar_ref[...] = da_r
        s1_ref[...] = _head_sum(dX * xs, K, KP)
        dd_ref[...] += _csum(dY * xs)
        dxs_ref[...] = dX * dt_f + dY * df_ref[...]
        db_ref[...] = dB
        dc_ref[...] = dC

    rc = lambda c: nc - 1 - c
    tok = jax.ShapeDtypeStruct((G, L, K), F32)
    tok_spec = pl.BlockSpec((None, Q, K), lambda g, c: (g, rc(c), 0))
    return _pcall(
        body, (xbc, xbc, xbc, dt_c, a_c, a_r, d_full, states, dy), name=name,
        out_shape=[jax.ShapeDtypeStruct((L, d_inner), F32), jax.ShapeDtypeStruct((L, G * N), F32),
                   jax.ShapeDtypeStruct((L, G * N), F32), tok, jax.ShapeDtypeStruct((G, K, L), F32), tok,
                   jax.ShapeDtypeStruct((1, d_inner), F32)],
        grid=(G, nc),
        in_specs=_ssd_specs(K, KP, d_inner, True, nc) + [
            pl.BlockSpec((None, None, N, KP), lambda g, c: (g, rc(c), 0, 0)),
            pl.BlockSpec((Q, KP), lambda g, c: (rc(c), g))],
        out_specs=[pl.BlockSpec((Q, KP), lambda g, c: (rc(c), g)), pl.BlockSpec((Q, N), lambda g, c: (rc(c), g)),
                   pl.BlockSpec((Q, N), lambda g, c: (rc(c), g)), tok_spec,
                   pl.BlockSpec((None, K, Q), lambda g, c: (g, 0, rc(c))), tok_spec,
                   pl.BlockSpec((1, KP), lambda g, c: (0, g))],
        scratch=[pltpu.VMEM((N, KP), F32)],
        sem=("parallel", "arbitrary"), comm=comm)


def _slopes():
    n = DIL_N_GROUPS * DIL_HEADS
    s = 2.0 ** (-8.0 * np.arange(1, n + 1) / n)
    return s.reshape(DIL_N_GROUPS, DIL_HEADS).astype(np.float32)


ATT_TB = 2048


def _tn(a, b):
    return lax.dot_general(a, b, (((0,), (0,)), ((), ())), preferred_element_type=F32)


def _slope_rows(g):
    return jnp.asarray(np.repeat(_slopes()[g][:, None], LANE, axis=1))


def _attn_bias(slope_row, d):
    B = DIL_BLOCK
    qi = lax.broadcasted_iota(jnp.int32, (B, B), 0)
    kj = lax.broadcasted_iota(jnp.int32, (B, B), 1)
    sd = slope_row * float(d)
    cur = jnp.where(kj <= qi, -(qi - kj).astype(F32) * sd, NEG)
    prv = jnp.where(kj >= qi, -(qi + B - kj).astype(F32) * sd, NEG)
    return cur, prv


def _sub_rows(j, r, d):
    base = j * DIL_BLOCK * d + r
    return pl.ds(base, DIL_BLOCK, stride=d) if d > 1 else pl.ds(base, DIL_BLOCK)


def _attn_geometry(L, g):
    window, d = DIL_PATTERNS[g]
    tb = min(ATT_TB, L)
    assert window // d == DIL_BLOCK and tb % (d * DIL_BLOCK) == 0 and L % tb == 0
    return d, tb, L // tb, tb // (d * DIL_BLOCK)


def _kv_specs(g, tb, nb):
    E, nh = DIL_HEAD_DIM, DIL_N_GROUPS * DIL_HEADS
    prev = lambda b: jnp.maximum(b - 1, 0)
    return [pl.BlockSpec((tb, E), lambda b, h: (b, g * DIL_HEADS + h)),
            pl.BlockSpec((tb, E), lambda b, h: (b, nh + g * DIL_HEADS + h)),
            pl.BlockSpec((tb, E), lambda b, h: (prev(b), g * DIL_HEADS + h)),
            pl.BlockSpec((tb, E), lambda b, h: (prev(b), nh + g * DIL_HEADS + h))]


def _attn_fwd(qz, kv, g, *, name):
    L = qz.shape[0]
    d, tb, nb, nj = _attn_geometry(L, g)
    B, E = DIL_BLOCK, DIL_HEAD_DIM
    scale = E ** -0.5

    def body(sl_ref, q_ref, kc_ref, vc_ref, kp_ref, vp_ref, o_ref, lse_ref):
        b, h = pl.program_id(0), pl.program_id(1)

        @pl.when(h == 0)
        def _():
            lse_ref[...] = jnp.zeros_like(lse_ref)

        bias_c, bias_p = _attn_bias(sl_ref[pl.ds(h, 1), :], d)
        bias_p0 = jnp.where(b > 0, bias_p, NEG)
        oneh = (lax.broadcasted_iota(jnp.int32, (1, LANE), 1) == h).astype(F32)
        for r in range(d):
            for j in range(nj):
                rows = _sub_rows(j, r, d)
                qs = q_ref[rows, :].astype(BF16)
                kc, vc = kc_ref[rows, :].astype(BF16), vc_ref[rows, :].astype(BF16)
                if j > 0:
                    prows, bp = _sub_rows(j - 1, r, d), bias_p
                    kq, vq = kc_ref[prows, :].astype(BF16), vc_ref[prows, :].astype(BF16)
                else:
                    prows, bp = _sub_rows(nj - 1, r, d), bias_p0
                    kq, vq = kp_ref[prows, :].astype(BF16), vp_ref[prows, :].astype(BF16)
                sc = _nt(qs, kc) * scale + bias_c
                sp = _nt(qs, kq) * scale + bp
                m = jnp.maximum(jnp.max(sc, axis=1, keepdims=True), jnp.max(sp, axis=1, keepdims=True))
                pc, pp = jnp.exp(sc - m), jnp.exp(sp - m)
                den = jnp.sum(pc, axis=1, keepdims=True) + jnp.sum(pp, axis=1, keepdims=True)
                o = _nn(pc.astype(BF16), vc) + _nn(pp.astype(BF16), vq)
                o_ref[rows, :] = o / den
                lse_ref[rows, :] = lse_ref[rows, :] + (m + jnp.log(den)) * oneh

    return _pcall(
        body, (_slope_rows(g), qz, kv, kv, kv, kv), name=name,
        out_shape=[jax.ShapeDtypeStruct((L, DIL_W), F32), jax.ShapeDtypeStruct((L, LANE), F32)],
        grid=(nb, DIL_HEADS),
        in_specs=[pl.BlockSpec((DIL_HEADS, LANE), lambda b, h: (0, 0)),
                  pl.BlockSpec((tb, E), lambda b, h: (b, g * DIL_HEADS + h))] + _kv_specs(g, tb, nb),
        out_specs=[pl.BlockSpec((tb, E), lambda b, h: (b, h)), pl.BlockSpec((tb, LANE), lambda b, h: (b, 0))],
        sem=("parallel", "arbitrary"))


def _attn_bwd(qz, kv, do, lse, dl, dqz, dk, dv, g, *, name):
    L = qz.shape[0]
    d, tb, nb, nj = _attn_geometry(L, g)
    B, E = DIL_BLOCK, DIL_HEAD_DIM
    scale = E ** -0.5
    nxt = lambda b: jnp.minimum(b + 1, nb - 1)
    fresh = dk is None

    def body(sl_ref, qc_ref, qn_ref, kc_ref, vc_ref, kp_ref, vp_ref, doc_ref, don_ref, lsec_ref, lsen_ref,
             dlc_ref, dln_ref, *rest):
        dq_ref, dk_ref, dv_ref = rest[-3:]
        b, h = pl.program_id(0), pl.program_id(1)
        bias_c, bias_p = _attn_bias(sl_ref[pl.ds(h, 1), :], d)
        bias_first = jnp.where(b > 0, bias_p, NEG)
        bias_last = jnp.where(b < nb - 1, bias_p, NEG)
        oneh = (lax.broadcasted_iota(jnp.int32, (1, LANE), 1) == h).astype(F32)

        def col(ref, rows):
            return jnp.sum(ref[rows, :] * oneh, axis=1, keepdims=True)

        def pair(q, do_, lse_, dl_, k_, v_, bias):
            p = jnp.exp(_nt(q, k_) * scale + bias - lse_)
            ds = p * (_nt(do_, v_) - dl_)
            return p.astype(BF16), ds.astype(BF16)

        for r in range(d):
            rows0 = _sub_rows(0, r, d)
            qj, doj = qc_ref[rows0, :].astype(BF16), doc_ref[rows0, :].astype(BF16)
            lsej, dlj = col(lsec_ref, rows0), col(dlc_ref, rows0)
            prows = _sub_rows(nj - 1, r, d)
            kq, vq = kp_ref[prows, :].astype(BF16), vp_ref[prows, :].astype(BF16)
            _, ds = pair(qj, doj, lsej, dlj, kq, vq, bias_first)
            dq_carry = _nn(ds, kq)
            for j in range(nj):
                rows = _sub_rows(j, r, d)
                kj, vj = kc_ref[rows, :].astype(BF16), vc_ref[rows, :].astype(BF16)
                p, ds = pair(qj, doj, lsej, dlj, kj, vj, bias_c)
                dq_ref[rows, :] = (dq_carry + _nn(ds, kj)) * scale
                dkj, dvj = _tn(ds, qj), _tn(p, doj)
                if j < nj - 1:
                    nrows = _sub_rows(j + 1, r, d)
                    qn, don = qc_ref[nrows, :].astype(BF16), doc_ref[nrows, :].astype(BF16)
                    lsen, dln, bias = col(lsec_ref, nrows), col(dlc_ref, nrows), bias_p
                else:
                    qn, don = qn_ref[rows0, :].astype(BF16), don_ref[rows0, :].astype(BF16)
                    lsen, dln, bias = col(lsen_ref, rows0), col(dln_ref, rows0), bias_last
                p2, ds2 = pair(qn, don, lsen, dln, kj, vj, bias)
                dk_ref[rows, :] = (dkj + _tn(ds2, qn)) * scale
                dv_ref[rows, :] = dvj + _tn(p2, don)
                dq_carry = _nn(ds2, kj)
                qj, doj, lsej, dlj = qn, don, lsen, dln

    hb = lambda b, h: (b, g * DIL_HEADS + h)
    anyspec = pl.BlockSpec(memory_space=pl.ANY)
    args = [_slope_rows(g), qz, qz, kv, kv, kv, kv, do, do, lse, lse, dl, dl, dqz] + ([] if fresh else [dk, dv])
    in_specs = ([pl.BlockSpec((DIL_HEADS, LANE), lambda b, h: (0, 0)),
                 pl.BlockSpec((tb, E), hb), pl.BlockSpec((tb, E), lambda b, h: (nxt(b), g * DIL_HEADS + h))]
                + _kv_specs(g, tb, nb)
                + [pl.BlockSpec((tb, E), lambda b, h: (b, h)), pl.BlockSpec((tb, E), lambda b, h: (nxt(b), h)),
                   pl.BlockSpec((tb, LANE), lambda b, h: (b, 0)), pl.BlockSpec((tb, LANE), lambda b, h: (nxt(b), 0)),
                   pl.BlockSpec((tb, LANE), lambda b, h: (b, 0)), pl.BlockSpec((tb, LANE), lambda b, h: (nxt(b), 0)),
                   anyspec] + ([] if fresh else [anyspec, anyspec]))
    aliases = {13: 0} if fresh else {13: 0, 14: 1, 15: 2}
    dkv_sd = jax.ShapeDtypeStruct((L, DIL_N_GROUPS * DIL_W), F32)
    return pl.pallas_call(
        body, name=name,
        out_shape=[jax.ShapeDtypeStruct(dqz.shape, F32), dkv_sd, dkv_sd],
        grid=(nb, DIL_HEADS), in_specs=in_specs,
        out_specs=[pl.BlockSpec((tb, E), hb), pl.BlockSpec((tb, E), hb), pl.BlockSpec((tb, E), hb)],
        input_output_aliases=aliases,
        compiler_params=_cp(("parallel", "parallel")))(*args)


def _head_expand():
    r = lax.broadcasted_iota(jnp.int32, (LANE, DIL_W), 0)
    c = lax.broadcasted_iota(jnp.int32, (LANE, DIL_W), 1)
    E = ((c >= r * DIL_HEAD_DIM) & (c < (r + 1) * DIL_HEAD_DIM)).astype(F32)
    r2 = lax.broadcasted_iota(jnp.int32, (DIL_W, LANE), 0)
    c2 = lax.broadcasted_iota(jnp.int32, (DIL_W, LANE), 1)
    Et = ((r2 >= c2 * DIL_HEAD_DIM) & (r2 < (c2 + 1) * DIL_HEAD_DIM)).astype(F32)
    return E, Et


def _merge_weights(l0, l1, l2):
    m = jnp.maximum(jnp.maximum(l0, l1), l2)
    e = [jnp.exp(l - m) for l in (l0, l1, l2)]
    tot = e[0] + e[1] + e[2]
    return [v / tot for v in e]


def _merge_fwd(os_, lses, qz, z_off, *, name):
    def fn(o0, o1, o2, l0, l1, l2, z):
        E, _ = _head_expand()
        w = _merge_weights(l0, l1, l2)
        om = sum(_hi(wg, E) * og for wg, og in zip(w, (o0, o1, o2)))
        return [om * _silu(z)], []

    rows = [(o, 0, DIL_W) for o in os_] + [(l, 0, LANE) for l in lses] + [(qz, z_off, DIL_W)]
    return _rowmap(fn, rows, [], [(DIL_W, BF16)], [], name=name)[0]


def _merge_bwd(os_, lses, qz, z_off, dog, *, name, comm=None):
    def fn(o0, o1, o2, l0, l1, l2, z, dg):
        E, Et = _head_expand()
        dg = dg.astype(F32)
        w = _merge_weights(l0, l1, l2)
        wf = [_hi(wg, E) for wg in w]
        os3 = (o0, o1, o2)
        om = sum(a * b for a, b in zip(wf, os3))
        dom = dg * _silu(z)
        dz = dg * om * _dsilu(z)
        dw = [_hi(dom * og, Et) for og in os3]
        tot = sum(a * b for a, b in zip(w, dw))
        return [wf[0] * dom, wf[1] * dom, wf[2] * dom, w[0] * tot, w[1] * tot, w[2] * tot, dz], []

    rows = ([(o, 0, DIL_W) for o in os_] + [(l, 0, LANE) for l in lses] + [(qz, z_off, DIL_W), (dog, 0, DIL_W)])
    outs = [(DIL_W, F32)] * 3 + [(LANE, F32)] * 3 + [(qz.shape[1], F32, z_off, DIL_W)]
    return _rowmap(fn, rows, [], outs, [], name=name, comm=comm)


def _adamw(gparts, w, m, v, *, name, tr=128):
    n, R, C = gparts.shape
    tr = _pick(R, tr)
    c1 = 1.0 - ADAM_B1 ** ADAM_STEP
    c2 = 1.0 - ADAM_B2 ** ADAM_STEP

    def body(g_ref, w_ref, m_ref, v_ref, go_ref, d_ref, mo_ref, vo_ref):
        g = g_ref[0].astype(F32)
        for i in range(1, n):
            g = g + g_ref[i].astype(F32)
        mn = ADAM_B1 * m_ref[...] + (1.0 - ADAM_B1) * g
        vn = ADAM_B2 * v_ref[...] + (1.0 - ADAM_B2) * jnp.square(g)
        d_ref[...] = -ADAM_LR * ((mn / c1) / (jnp.sqrt(vn / c2) + ADAM_EPS) + ADAM_WD * w_ref[...])
        go_ref[...] = g
        mo_ref[...] = mn
        vo_ref[...] = vn

    blk = pl.BlockSpec((tr, C), lambda i: (i, 0))
    sd = jax.ShapeDtypeStruct((R, C), F32)
    return pl.pallas_call(
        body, name=name, out_shape=[sd, sd, sd, sd], grid=(R // tr,),
        in_specs=[pl.BlockSpec((n, tr, C), lambda i: (0, i, 0)), blk, blk, blk],
        out_specs=[blk, blk, blk, blk],
        compiler_params=_cp(("parallel",)),
    )(gparts, w, m, v)


def _sum_parts(parts, *, name):
    n, R, C = parts.shape

    def body(p_ref, o_ref):
        s = p_ref[0]
        for i in range(1, n):
            s = s + p_ref[i]
        o_ref[...] = s

    return pl.pallas_call(
        body, name=name, out_shape=jax.ShapeDtypeStruct((R, C), F32),
        in_specs=[pl.BlockSpec(memory_space=pltpu.VMEM)], out_specs=pl.BlockSpec(memory_space=pltpu.VMEM),
    )(parts)


def _cols_from(g):
    _, R, Cs = g.shape
    return jnp.transpose(g, (1, 0, 2)).reshape(R, N_DEV * Cs)


def _col_parts(dw):
    R, C = dw.shape
    return jnp.transpose(dw.reshape(R, N_DEV, C // N_DEV), (1, 0, 2))


def _ag_cols(w_loc):
    if w_loc.shape[1] % LANE == 0:
        return _ag_comm(w_loc, cols=True), (lambda g: g)
    return _ag_comm(w_loc), _cols_from


def _rs_cols(dw):
    if (dw.shape[1] // N_DEV) % LANE == 0:
        return _a2a_comm(dw, cols=True)
    return _a2a_comm(_col_parts(dw))


def kernel(x, c, ada_w, ada_b, ln_g, ln_b, a_in_w, a_conv_w, a_conv_b, a_dt_bias, a_A_log, a_D, a_norm_g, a_out_w, kv_w, b_in_w, b_out_w, loss_target, m_ada_w, m_ada_b, m_ln_g, m_ln_b, m_a_in_w, m_a_conv_w, m_a_conv_b, m_a_dt_bias, m_a_A_log, m_a_D, m_a_norm_g, m_a_out_w, m_kv_w, m_b_in_w, m_b_out_w, v_ada_w, v_ada_b, v_ln_g, v_ln_b, v_a_in_w, v_a_conv_w, v_a_conv_b, v_a_dt_bias, v_a_A_log, v_a_D, v_a_norm_g, v_a_out_w, v_kv_w, v_b_in_w, v_b_out_w):
    L, D = x.shape[1], x.shape[2]
    H = a_dt_bias.shape[1]
    d_inner = H * SSD_HEAD_DIM
    G, N, P = SSD_N_GROUPS, SSD_D_STATE, SSD_HEAD_DIM
    K = H // G
    KP = K * P
    conv_dim = d_inner + 2 * G * N
    in_dim = d_inner + conv_dim + H
    in_pad = d_inner + conv_dim + LANE
    assert H <= LANE and KP % LANE == 0 and L % SSD_CHUNK == 0
    me = 4 * lax.axis_index("x") + 2 * lax.axis_index("y") + lax.axis_index("c")
    x2d, tgt = x[0], loss_target[0]

    c_all = _all_gather(c, "ag_c").reshape(N_DEV, D)
    mods = []
    for l in range(DEPTH):
        ab = lax.dynamic_slice(ada_b[l], (me * (3 * D // N_DEV),), (3 * D // N_DEV,))[None]
        mods.append(_matmul(c_all, ada_w[l], name=f"mod{l}", exact=True, a_silu=True, bias=ab))
    mod_all = _all_gather(jnp.stack(mods), "ag_mod")
    mod_me = lax.dynamic_index_in_dim(jnp.transpose(mod_all, (2, 1, 0, 3)).reshape(N_DEV, DEPTH, 3 * D), me, 0, False)
    shift = [mod_me[l, None, 0:D] for l in range(DEPTH)]
    scale = [mod_me[l, None, D:2 * D] for l in range(DEPTH)]
    gate = [mod_me[l, None, 2 * D:3 * D] for l in range(DEPTH)]

    w_in = _cols_from(_all_gather(a_in_w[0].astype(BF16), "ag_a_in"))
    w_in = jnp.pad(w_in, ((0, 0), (0, in_pad - in_dim)))
    conv_w = _all_gather(a_conv_w[0], "ag_conv_w")
    conv_w = jnp.transpose(conv_w, (1, 0, 2)).reshape(SSD_CONV_W, conv_dim)
    conv_b = _all_gather(a_conv_b, "ag_conv_b").reshape(1, conv_dim)
    norm_g = _all_gather(a_norm_g, "ag_norm_g").reshape(1, d_inner)

    def modulate(xin, l, name):
        fn = lambda xv, sc, sh: ([xv * (1.0 + sc) + sh], [])
        return _rowmap(fn, [(xin, 0, D)], [scale[l], shift[l]], [(D, BF16)], [], name=name)[0]

    def ln_out(xv, yv, gt, g, b):
        u = DEEPNORM_ALPHA * xv + (1.0 + gt) * yv
        mu = jnp.mean(u, axis=1, keepdims=True)
        uc = u - mu
        var = jnp.mean(uc * uc, axis=1, keepdims=True)
        return uc * lax.rsqrt(var + LN_EPS) * g + b

    def ln0_fwd(xin, y, name):
        def fn(xv, yv, gt, g, b, sc, sh):
            o = ln_out(xv, yv, gt, g, b)
            return [o, o, o * (1.0 + sc) + sh], []
        return _rowmap(fn, [(xin, 0, D), (y, 0, D)], [gate[0], ln_g[0:1], ln_b[0:1], scale[1], shift[1]],
                       [(D, F32), (D, BF16), (D, BF16)], [], name=name)

    def ln1_loss(xin, y, target, name):
        def fn(xv, yv, tv, gt, g, b):
            e = ln_out(xv, yv, gt, g, b) - tv
            return [e * (1.0 / D)], [_csum(e * e) * (0.5 / D)]
        return _rowmap(fn, [(xin, 0, D), (y, 0, D), (target, 0, D)], [gate[1], ln_g[1:2], ln_b[1:2]],
                       [(D, F32)], [(1, D)], name=name)

    def ln_bwd(xin, y, dout, l, name):
        def fn(xv, yv, do, gt, g, b):
            u = DEEPNORM_ALPHA * xv + (1.0 + gt) * yv
            mu = jnp.mean(u, axis=1, keepdims=True)
            uc = u - mu
            var = jnp.mean(uc * uc, axis=1, keepdims=True)
            rs = lax.rsqrt(var + LN_EPS)
            xh = uc * rs
            dxh = do * g
            du = rs * (dxh - jnp.mean(dxh, axis=1, keepdims=True) - xh * jnp.mean(dxh * xh, axis=1, keepdims=True))
            return [DEEPNORM_ALPHA * du, (1.0 + gt) * du], [_csum(du * yv), _csum(do * xh), _csum(do)]
        return _rowmap(fn, [(xin, 0, D), (y, 0, D), (dout, 0, D)], [gate[l], ln_g[l:l + 1], ln_b[l:l + 1]],
                       [(D, F32), (D, BF16)], [(1, D)] * 3, name=name)

    def mod_bwd(xin, dh, dx_acc, l, name):
        def fn(xv, dhv, dxa, sc):
            return [dxa + dhv * (1.0 + sc)], [_csum(dhv * xv), _csum(dhv)]
        return _rowmap(fn, [(xin, 0, D), (dh, 0, D), (dx_acc, 0, D)], [scale[l]], [(D, F32)], [(1, D)] * 2, name=name)

    h0 = modulate(x2d, 0, "mod_h0")
    proj, g_aout = _matmul(h0, w_in, name="mm_a_in", tn=1152,
                           comm=_ag_comm(a_out_w[0].astype(BF16)))
    w_aout = g_aout.reshape(d_inner, D)
    xbc = _conv_fwd(proj, d_inner, conv_dim, conv_w, conv_b, name="conv_fwd")
    dt_raw = proj[:, d_inner + conv_dim:]
    padh = lambda a: jnp.pad(a, ((0, 0), (0, LANE - H)))
    bias_p, alog_p = padh(a_dt_bias), padh(a_A_log)
    dt_p, a_p = _ssd_prep(dt_raw, bias_p, alog_p, name="ssd_prep")
    dt_c = jnp.transpose(dt_p[:, :H].reshape(L, G, K), (1, 0, 2))
    a_c = jnp.transpose(a_p[:, :H].reshape(L, G, K), (1, 0, 2))
    a_r = jnp.transpose(a_c, (0, 2, 1))
    d_full = jnp.repeat(a_D.reshape(H), P)[None]
    ssd_in = (xbc, dt_c, a_c, a_r, d_full)
    cm_kv, fix_kv = _ag_cols(kv_w.astype(BF16))
    y_ssd, states, w_kv = _ssd_fwd(*ssd_in, d_inner=d_inner, name="ssd_fwd", comm=cm_kv)
    w_kv = fix_kv(w_kv)

    gw = d_inner // G

    def gnorm_fn(yv, zv, g):
        yg = yv * _silu(zv)
        r = lax.rsqrt(jnp.mean(yg * yg, axis=1, keepdims=True) + RMS_EPS)
        return [yg * r * g], []
    yn = _rowmap(gnorm_fn, [(y_ssd, 0, d_inner), (proj, 0, d_inner)], [norm_g], [(d_inner, BF16)], [],
                 name="gnorm_fwd", cw=gw, tr=1024)[0]
    cm_bin, fix_bin = _ag_cols(b_in_w[0].astype(BF16))
    ya, w_bin = _matmul(yn, w_aout, name="mm_a_out", comm=cm_bin)
    w_bin = fix_bin(w_bin)
    x1, x1b, h1 = ln0_fwd(x2d, ya, "ln0_fwd")

    cm_bout, fix_bout = _ag_cols(b_out_w[0].astype(BF16))
    kv, w_bout = _matmul(x1b, w_kv, name="mm_kv", comm=cm_bout)
    w_bout = fix_bout(w_bout)
    qz = _matmul(h1, w_bin, name="mm_b_in")
    z_off = DIL_N_GROUPS * DIL_W
    os_, lses = [], []
    for g in range(DIL_N_GROUPS):
        o, lse = _attn_fwd(qz, kv, g, name=f"attn_fwd{g}")
        os_.append(o)
        lses.append(lse)
    og = _merge_fwd(os_, lses, qz, z_off, name="merge_fwd")
    yb = _matmul(og, w_bout, name="mm_b_out")
    dx2, loss_cols = ln1_loss(x1, yb, tgt, "ln1_loss")
    loss = lax.psum(jnp.sum(loss_cols), ("x", "y", "c"))

    dx1a, dyb, dgate1, dlng1, dlnb1 = ln_bwd(x1, yb, dx2, 1, "ln1_bwd")
    dw_bout = _matmul(og, dyb, name="mm_b_out_dw", ta=True, out_dtype=BF16)
    dog = _matmul(dyb, w_bout, name="mm_b_out_dx", tb=True, out_dtype=BF16)
    do0, do1, do2, dl0, dl1, dl2, dqz, r_bout = _merge_bwd(os_, lses, qz, z_off, dog, name="merge_bwd",
                                                           comm=_rs_cols(dw_bout))
    dk = dv = None
    for g, (do_g, dl_g) in enumerate(zip((do0, do1, do2), (dl0, dl1, dl2))):
        dqz, dk, dv = _attn_bwd(qz, kv, do_g, lses[g], dl_g, dqz, dk, dv, g, name=f"attn_bwd{g}")
    dw_bin = _matmul(h1, dqz, name="mm_b_in_dw", ta=True, out_dtype=BF16)
    dh1 = _matmul(dqz, w_bin, name="mm_b_in_dx", tb=True)
    dx1b, dscale1, dshift1 = mod_bwd(x1, dh1, dx1a, 1, "mod1_bwd")
    kw = DIL_N_GROUPS * DIL_W
    dw_k = _matmul(x1b, dk, name="mm_k_dw", ta=True, out_dtype=BF16, out_cols=(2 * kw, 0))
    dw_kv = _matmul(x1b, dv, name="mm_v_dw", ta=True, out_dtype=BF16, out_cols=(2 * kw, kw), out_buf=dw_k)
    dx1k = _matmul(dk, w_kv, name="mm_k_dx", tb=True, bias=dx1b)
    dx1 = _matmul(dv, w_kv, name="mm_v_dx", tb=True, bias=dx1k, b_koff=kw)

    dxa, dya, dgate0, dlng0, dlnb0 = ln_bwd(x2d, ya, dx1, 0, "ln0_bwd")
    dw_aout = _matmul(yn, dya, name="mm_a_out_dw", ta=True, out_dtype=BF16)
    dyn = _matmul(dya, w_aout, name="mm_a_out_dx", tb=True)

    def gnorm_bwd_fn(yv, zv, dn, g):
        sz = _silu(zv)
        yg = yv * sz
        r = lax.rsqrt(jnp.mean(yg * yg, axis=1, keepdims=True) + RMS_EPS)
        nrm = yg * r
        dnn = dn * g
        dyg = r * (dnn - nrm * jnp.mean(dnn * nrm, axis=1, keepdims=True))
        return [dyg * sz, dyg * yv * _dsilu(zv)], [_csum(dn * nrm)]
    dy_ssd, dproj, dnorm_g = _rowmap(gnorm_bwd_fn, [(y_ssd, 0, d_inner), (proj, 0, d_inner), (dyn, 0, d_inner)],
                                     [norm_g], [(d_inner, F32), (in_pad, BF16, 0, gw)], [(1, d_inner)],
                                     name="gnorm_bwd", cw=gw, tr=1024)
    dxs, dB, dC, da_t, dar_t, s1_t, dD_f, r_kv = _ssd_bwd(
        *ssd_in, states, dy_ssd, d_inner=d_inner, name="ssd_bwd", comm=_rs_cols(dw_kv))
    dproj, dconv_w, dconv_b, r_aout = _conv_bwd(
        proj, d_inner, conv_dim, conv_w, conv_b, (dxs, dB, dC), dproj, name="conv_bwd", tc=128,
        comm=_a2a_comm(dw_aout.reshape(N_DEV, d_inner // N_DEV, D)))
    tokp = lambda t: padh(jnp.transpose(t, (1, 0, 2)).reshape(L, H))
    dar_tok = padh(jnp.transpose(dar_t, (2, 0, 1)).reshape(L, H))
    dproj, ddt_bias_p, dA_log_p = _ssd_post(tokp(da_t), dar_tok, tokp(s1_t), dt_p, dt_raw, bias_p, alog_p,
                                            dproj, d_inner + conv_dim, name="ssd_post")
    ddt_bias, dA_log = ddt_bias_p[:, :H], dA_log_p[:, :H]
    dD = jnp.sum(dD_f.reshape(H, P), axis=1)[None]
    dw_in, r_bin = _matmul(h0, dproj, name="mm_a_in_dw", ta=True, out_dtype=BF16, tn=1152, comm=_rs_cols(dw_bin))
    dw_in = dw_in[:, :in_dim]
    cs_in = in_dim // N_DEV
    by_c = jnp.transpose(dw_in.reshape(D, N_DEV // 2, 2, cs_in), (2, 1, 0, 3))
    my_c = lax.axis_index("c")
    keep = lax.dynamic_index_in_dim(by_c, my_c, 0, False)
    give = lax.dynamic_index_in_dim(by_c, 1 - my_c, 0, False)
    got = _run_comm(_pair_comm(give), "rs_a_in_pair")[0]
    pair_sum = _rowmap(lambda a, b: ([a.astype(F32) + b.astype(F32)], []),
                       [(keep.reshape(-1, cs_in), 0, cs_in), (got.reshape(-1, cs_in), 0, cs_in)], [],
                       [(cs_in, BF16)], [], name="rs_a_in_add", tr=512)[0].reshape(N_DEV // 2, D, cs_in)
    dh0, r_in = _matmul(dproj, w_in, name="mm_a_in_dx", tb=True, tk=1152,
                        comm=_quad_comm(pair_sum))
    grad_x, dscale0, dshift0 = mod_bwd(x2d, dh0, dxa, 0, "mod0_bwd")

    dmod = jnp.concatenate([dshift0, dscale0, dgate0, dshift1, dscale1, dgate1], axis=1)
    pieces = [dmod, dlng0, dlng1, dlnb0, dlnb1, ddt_bias, dA_log, dD,
              dconv_w.reshape(1, -1), dconv_b, dnorm_g]
    sizes = [p.shape[1] for p in pieces]
    tot = sum(sizes)
    tot_pad = -(-tot // (8 * LANE)) * (8 * LANE)
    packed = jnp.pad(jnp.concatenate(pieces, axis=1), ((0, 0), (0, tot_pad - tot))).reshape(tot_pad // LANE, LANE)
    packed_all = _all_gather(packed, "ag_small")
    small = _sum_parts(packed_all, name="sum_small").reshape(tot_pad)
    offs = np.cumsum([0] + sizes)
    seg = lambda i: small[int(offs[i]):int(offs[i + 1])]
    g_ada_b = seg(0).reshape(DEPTH, 3 * D)
    g_ln_g = jnp.stack([seg(1), seg(2)])
    g_ln_b = jnp.stack([seg(3), seg(4)])
    g_dt_bias, g_A_log, g_D = seg(5)[None], seg(6)[None], seg(7)[None]
    cs = conv_dim // N_DEV
    g_conv_w = lax.dynamic_slice(seg(8).reshape(SSD_CONV_W, conv_dim), (0, me * cs), (SSD_CONV_W, cs))[None]
    g_conv_b = lax.dynamic_slice(seg(9), (me * cs,), (cs,))[None]
    ns = d_inner // N_DEV
    g_norm_g = lax.dynamic_slice(seg(10), (me * ns,), (ns,))[None]

    ms = 3 * D // N_DEV
    dmod_all = packed_all.reshape(N_DEV, tot_pad)[:, :DEPTH * 3 * D].reshape(N_DEV, DEPTH, 3 * D)
    dmod_cols = lax.dynamic_slice(dmod_all, (0, 0, me * ms), (N_DEV, DEPTH, ms))
    c_t = jnp.transpose(c_all)
    g_ada_w = jnp.stack([_matmul(c_t, dmod_cols[:, l], name=f"mm_ada_dw{l}", exact=True, a_silu=True)
                         for l in range(DEPTH)])[None]

    def upd(parts, w, m, v, name):
        shp = w.shape
        r2 = lambda a: a.reshape(-1, shp[-1])
        return [o.reshape(shp) for o in _adamw(parts, r2(w), r2(m), r2(v), name=name)]

    res = {}
    res["ada_w"] = upd(g_ada_w.reshape(1, -1, ms), ada_w, m_ada_w, v_ada_w, "adam_ada_w")
    res["a_in_w"] = upd(r_in, a_in_w, m_a_in_w, v_a_in_w, "adam_a_in")
    res["a_out_w"] = upd(r_aout, a_out_w, m_a_out_w, v_a_out_w, "adam_a_out")
    res["kv_w"] = upd(r_kv, kv_w, m_kv_w, v_kv_w, "adam_kv")
    res["b_in_w"] = upd(r_bin, b_in_w, m_b_in_w, v_b_in_w, "adam_b_in")
    res["b_out_w"] = upd(r_bout, b_out_w, m_b_out_w, v_b_out_w, "adam_b_out")

    small_names = ["ada_b", "ln_g", "ln_b", "a_conv_w", "a_conv_b", "a_dt_bias", "a_A_log", "a_D", "a_norm_g"]
    small_g = [g_ada_b, g_ln_g, g_ln_b, g_conv_w, g_conv_b, g_dt_bias, g_A_log, g_D, g_norm_g]
    small_w = [ada_b, ln_g, ln_b, a_conv_w, a_conv_b, a_dt_bias, a_A_log, a_D, a_norm_g]
    small_m = [m_ada_b, m_ln_g, m_ln_b, m_a_conv_w, m_a_conv_b, m_a_dt_bias, m_a_A_log, m_a_D, m_a_norm_g]
    small_v = [v_ada_b, v_ln_g, v_ln_b, v_a_conv_w, v_a_conv_b, v_a_dt_bias, v_a_A_log, v_a_D, v_a_norm_g]
    ssz = [int(np.prod(w.shape)) for w in small_w]
    stot = sum(ssz)
    spad = -(-stot // (8 * LANE)) * (8 * LANE)

    def pack(arrs, fill):
        flat = jnp.concatenate([a.reshape(-1) for a in arrs])
        return jnp.concatenate([flat, jnp.full((spad - stot,), fill, F32)]).reshape(spad // LANE, LANE)

    sres = _adamw(pack(small_g, 0.0)[None], pack(small_w, 0.0), pack(small_m, 0.0), pack(small_v, 1.0), name="adam_small")
    soffs = np.cumsum([0] + ssz)
    for i, nme in enumerate(small_names):
        res[nme] = [r.reshape(-1)[int(soffs[i]):int(soffs[i + 1])].reshape(small_w[i].shape) for r in sres]

    order = ["ada_w", "ada_b", "ln_g", "ln_b", "a_in_w", "a_conv_w", "a_conv_b", "a_dt_bias", "a_A_log", "a_D",
             "a_norm_g", "a_out_w", "kv_w", "b_in_w", "b_out_w"]
    outs = [loss, grad_x[None]]
    for j in range(4):
        outs += [res[nme][j] for nme in order]
    return tuple(outs)
```

```python
import functools
import math

import numpy as np
import jax
import jax.numpy as jnp
from jax import lax
from jax.experimental import pallas as pl
from jax.experimental.pallas import tpu as pltpu

F32, BF16 = jnp.float32, jnp.bfloat16
HI = lax.Precision.HIGHEST
MESH = pl.DeviceIdType.MESH
N_DEV = 8

SSD_HEAD_DIM = 64
SSD_N_GROUPS = 8
SSD_D_STATE = 128
SSD_CONV_W = 4
SSD_CHUNK = 256
DIL_PATTERNS = ((128, 1), (512, 4), (2048, 16))
DIL_N_GROUPS = 3
DIL_HEADS = 8
DIL_HEAD_DIM = 128
DIL_BLOCK = 128
DIL_W = DIL_HEADS * DIL_HEAD_DIM
DEPTH = 2
DEEPNORM_ALPHA = (2 * DEPTH) ** 0.25
LN_EPS = 1e-5
RMS_EPS = 1e-5
ADAM_LR, ADAM_B1, ADAM_B2, ADAM_EPS, ADAM_WD, ADAM_STEP = 0.001, 0.9, 0.999, 1e-08, 0.01, 10
LANE = 128
NEG = -1e30
VMEM_LIMIT = 56 * 1024 * 1024


def _cp(sem=None):
    return pltpu.CompilerParams(dimension_semantics=sem, vmem_limit_bytes=VMEM_LIMIT)


def _silu(x):
    return x * jax.nn.sigmoid(x)


def _dsilu(x):
    s = jax.nn.sigmoid(x)
    return s * (1.0 + x * (1.0 - s))


def _softplus(x):
    return jnp.maximum(x, 0.0) + jnp.log(1.0 + jnp.exp(-jnp.abs(x)))


def _nt(a, b):
    return lax.dot_general(a, b, (((1,), (1,)), ((), ())), preferred_element_type=F32)


def _nn(a, b):
    return jnp.dot(a, b, preferred_element_type=F32)


def _hi(a, b):
    return jnp.dot(a, b, preferred_element_type=F32, precision=HI)


def _dot2(a, b01):
    hi = a.astype(BF16)
    lo = (a - hi.astype(F32)).astype(BF16)
    bb = b01.astype(BF16)
    return _nn(hi, bb) + _nn(lo, bb)


def _pick(n, pref, align=LANE):
    if n <= pref:
        return n
    for t in range(pref - pref % align, 0, -align):
        if n % t == 0:
            return t
    return n


class _Comm:
    def __init__(self, ins, outs, sems, start, finish):
        self.ins, self.outs, self.sems, self.start, self.finish = ins, outs, sems, start, finish


def _comm_join(comms):
    ins = [a for c in comms for a in c.ins]
    outs = [a for c in comms for a in c.outs]
    sems = [a for c in comms for a in c.sems]

    def split(refs, attr):
        res, i = [], 0
        for c in comms:
            n = len(getattr(c, attr))
            res.append(refs[i:i + n])
            i += n
        return res

    def start(cin, cout, csem):
        for c, a, b, d in zip(comms, split(cin, "ins"), split(cout, "outs"), split(csem, "sems")):
            c.start(a, b, d)

    def finish(cin, cout, csem):
        for c, a, b, d in zip(comms, split(cin, "ins"), split(cout, "outs"), split(csem, "sems")):
            c.finish(a, b, d)

    return _Comm(ins, outs, sems, start, finish)


def _ag_comm(v, cols=False):
    if cols:
        R, Cs = v.shape
        assert Cs % LANE == 0
        out_sd = jax.ShapeDtypeStruct((R, N_DEV * Cs), v.dtype)
    else:
        out_sd = jax.ShapeDtypeStruct((N_DEV,) + v.shape, v.dtype)

    def parts(x_ref, out_ref, send_sems, recv_sems, local_sem):
        x, y, c = lax.axis_index("x"), lax.axis_index("y"), lax.axis_index("c")
        me, sibling = (x, y, c), (x, y, 1 - c)
        chips = [(1 - x, y), (x, 1 - y), (1 - x, 1 - y)]

        def slab(px, py, pc):
            k = 4 * px + 2 * py + pc
            if cols:
                return out_ref.at[:, pl.ds(pl.multiple_of(k * Cs, LANE), Cs)]
            return out_ref.at[k]

        def copy(k, block, to, src=None):
            return pltpu.make_async_remote_copy(
                src_ref=slab(*block) if src is None else src, dst_ref=slab(*block),
                send_sem=send_sems.at[k], recv_sem=recv_sems.at[k], device_id=to, device_id_type=MESH)

        mine = pltpu.make_async_copy(x_ref, slab(*me), local_sem)
        first = [copy(0, me, sibling, src=x_ref)]
        first += [copy(1 + j, me, (*chip, c), src=x_ref) for j, chip in enumerate(chips)]
        passed = [copy(4 + j, (*chip, c), sibling) for j, chip in enumerate(chips)]
        return me, sibling, chips, c, copy, mine, first, passed

    def start(cin, cout, csem):
        _, _, _, _, _, mine, first, _ = parts(cin[0], cout[0], *csem)
        mine.start()
        for cp in first:
            cp.start()

    def finish(cin, cout, csem):
        me, sibling, chips, c, copy, mine, first, passed = parts(cin[0], cout[0], *csem)
        for j, chip in enumerate(chips):
            copy(1 + j, (*chip, c), me).wait_recv()
            passed[j].start()
        copy(0, sibling, me).wait_recv()
        for j, chip in enumerate(chips):
            copy(4 + j, (*chip, 1 - c), me).wait_recv()
        for cp in first + passed:
            cp.wait_send()
        mine.wait()

    return _Comm([v], [out_sd],
                 [pltpu.SemaphoreType.DMA((7,)), pltpu.SemaphoreType.DMA((7,)), pltpu.SemaphoreType.DMA], start, finish)


def _a2a_comm(v, cols=False):
    if cols:
        R, C = v.shape
        Cs = C // N_DEV
        assert Cs % LANE == 0
        out_sd = jax.ShapeDtypeStruct((N_DEV, R, Cs), v.dtype)
    else:
        out_sd = jax.ShapeDtypeStruct(v.shape, v.dtype)

    def parts(x_ref, out_ref, send_sems, recv_sems, local_sem):
        x, y, c = lax.axis_index("x"), lax.axis_index("y"), lax.axis_index("c")
        me = 4 * x + 2 * y + c

        def src(k):
            if cols:
                return x_ref.at[:, pl.ds(pl.multiple_of(k * Cs, LANE), Cs)]
            return x_ref.at[k]

        mine = pltpu.make_async_copy(src(me), out_ref.at[me], local_sem)
        sends, recvs = [], []
        for k, mask in enumerate(range(1, N_DEV)):
            px = 1 - x if (mask >> 2) & 1 else x
            py = 1 - y if (mask >> 1) & 1 else y
            pc = 1 - c if mask & 1 else c
            peer = 4 * px + 2 * py + pc
            sends.append(pltpu.make_async_remote_copy(
                src_ref=src(peer), dst_ref=out_ref.at[me],
                send_sem=send_sems.at[k], recv_sem=recv_sems.at[k], device_id=(px, py, pc), device_id_type=MESH))
            recvs.append(pltpu.make_async_remote_copy(
                src_ref=src(me), dst_ref=out_ref.at[peer],
                send_sem=send_sems.at[k], recv_sem=recv_sems.at[k], device_id=(px, py, pc), device_id_type=MESH))
        return mine, sends, recvs

    def start(cin, cout, csem):
        mine, sends, _ = parts(cin[0], cout[0], *csem)
        mine.start()
        for cp in sends:
            cp.start()

    def finish(cin, cout, csem):
        mine, sends, recvs = parts(cin[0], cout[0], *csem)
        for cp in recvs:
            cp.wait_recv()
        for cp in sends:
            cp.wait_send()
        mine.wait()

    return _Comm([v], [out_sd],
                 [pltpu.SemaphoreType.DMA((7,)), pltpu.SemaphoreType.DMA((7,)), pltpu.SemaphoreType.DMA], start, finish)


def _pair_comm(v4):
    def copy(x_ref, out_ref, send_sem, recv_sem):
        x, y, c = lax.axis_index("x"), lax.axis_index("y"), lax.axis_index("c")
        return pltpu.make_async_remote_copy(src_ref=x_ref, dst_ref=out_ref, send_sem=send_sem, recv_sem=recv_sem,
                                            device_id=(x, y, 1 - c), device_id_type=MESH)

    def start(cin, cout, csem):
        copy(cin[0], cout[0], *csem).start()

    def finish(cin, cout, csem):
        copy(cin[0], cout[0], *csem).wait()

    return _Comm([v4], [jax.ShapeDtypeStruct(v4.shape, v4.dtype)],
                 [pltpu.SemaphoreType.DMA, pltpu.SemaphoreType.DMA], start, finish)


def _quad_comm(v4):
    def parts(x_ref, out_ref, send_sems, recv_sems, local_sem):
        x, y, c = lax.axis_index("x"), lax.axis_index("y"), lax.axis_index("c")
        me = 2 * x + y
        mine = pltpu.make_async_copy(x_ref.at[me], out_ref.at[me], local_sem)
        sends, recvs = [], []
        for k, mask in enumerate(range(1, 4)):
            px = 1 - x if (mask >> 1) & 1 else x
            py = 1 - y if mask & 1 else y
            peer = 2 * px + py
            sends.append(pltpu.make_async_remote_copy(
                src_ref=x_ref.at[peer], dst_ref=out_ref.at[me],
                send_sem=send_sems.at[k], recv_sem=recv_sems.at[k], device_id=(px, py, c), device_id_type=MESH))
            recvs.append(pltpu.make_async_remote_copy(
                src_ref=x_ref.at[me], dst_ref=out_ref.at[peer],
                send_sem=send_sems.at[k], recv_sem=recv_sems.at[k], device_id=(px, py, c), device_id_type=MESH))
        return mine, sends, recvs

    def start(cin, cout, csem):
        mine, sends, _ = parts(cin[0], cout[0], *csem)
        mine.start()
        for cp in sends:
            cp.start()

    def finish(cin, cout, csem):
        mine, sends, recvs = parts(cin[0], cout[0], *csem)
        for cp in recvs:
            cp.wait_recv()
        for cp in sends:
            cp.wait_send()
        mine.wait()

    return _Comm([v4], [jax.ShapeDtypeStruct(v4.shape, v4.dtype)],
                 [pltpu.SemaphoreType.DMA((3,)), pltpu.SemaphoreType.DMA((3,)), pltpu.SemaphoreType.DMA], start, finish)


def _run_comm(comm, name):
    nci, nco = len(comm.ins), len(comm.outs)

    def body(*refs):
        comm.start(refs[:nci], refs[nci:nci + nco], refs[nci + nco:])
        comm.finish(refs[:nci], refs[nci:nci + nco], refs[nci + nco:])

    anyspec = pl.BlockSpec(memory_space=pl.ANY)
    return pl.pallas_call(body, name=name, out_shape=list(comm.outs), in_specs=[anyspec] * nci,
                          out_specs=[anyspec] * nco, scratch_shapes=list(comm.sems))(*comm.ins)


def _all_gather(v, name):
    return _run_comm(_ag_comm(v), name)[0]


def _pcall(body, args, *, name, grid, in_specs, out_specs, out_shape, scratch=(), sem=None, comm=None, aliases=None):
    out_shape, out_specs = list(out_shape), list(out_specs)
    aliases = dict(aliases or {})
    if comm is None:
        return pl.pallas_call(body, name=name, grid=grid, in_specs=list(in_specs), out_specs=out_specs,
                              out_shape=out_shape, scratch_shapes=list(scratch), input_output_aliases=aliases,
                              compiler_params=_cp(sem))(*args)
    ni, no, ns = len(args), len(out_shape), len(scratch)
    nci, nco = len(comm.ins), len(comm.outs)

    def wrapped(*refs):
        ins, cin = refs[:ni], refs[ni:ni + nci]
        o0 = ni + nci
        outs, cout = refs[o0:o0 + no], refs[o0 + no:o0 + no + nco]
        s0 = o0 + no + nco
        scr, csem = refs[s0:s0 + ns], refs[s0 + ns:]
        first = functools.reduce(jnp.logical_and, [pl.program_id(a) == 0 for a in range(len(grid))])
        last = functools.reduce(jnp.logical_and, [pl.program_id(a) == g - 1 for a, g in enumerate(grid)])

        @pl.when(first)
        def _():
            comm.start(cin, cout, csem)

        body(*ins, *outs, *scr)

        @pl.when(last)
        def _():
            comm.finish(cin, cout, csem)

    anyspec = pl.BlockSpec(memory_space=pl.ANY)
    res = pl.pallas_call(
        wrapped, name=name, grid=grid, in_specs=list(in_specs) + [anyspec] * nci,
        out_specs=out_specs + [anyspec] * nco, out_shape=out_shape + list(comm.outs),
        scratch_shapes=list(scratch) + list(comm.sems), input_output_aliases=aliases,
        compiler_params=_cp(("arbitrary",) * len(grid)))(*args, *comm.ins)
    return list(res[:no]) + list(res[no:])


def _matmul(a, b, *, name, ta=False, tb=False, out_dtype=F32, tm=1024, tn=1024, tk=2048,
            exact=False, a_silu=False, bias=None, comm=None, b_koff=0, out_cols=None, out_buf=None):
    (K, M) = a.shape if ta else a.shape[::-1]
    (N, K2) = b.shape if tb else b.shape[::-1]
    assert K == K2 or (tb and K + b_koff <= K2), (a.shape, b.shape, ta, tb)
    tm, tn, tk = _pick(M, tm), _pick(N, tn), _pick(K, tk)
    nk = K // tk
    assert b_koff % tk == 0
    ko = b_koff // tk
    a_spec = pl.BlockSpec((tk, tm), lambda i, j, k: (k, i)) if ta else pl.BlockSpec((tm, tk), lambda i, j, k: (i, k))
    b_spec = pl.BlockSpec((tn, tk), lambda i, j, k: (j, k + ko)) if tb else pl.BlockSpec((tk, tn), lambda i, j, k: (k, j))
    dims = (((0,) if ta else (1,), (1,) if tb else (0,)), ((), ()))
    in_specs, args = [a_spec, b_spec], [a, b]
    if bias is not None:
        if bias.shape[0] == 1:
            in_specs.append(pl.BlockSpec((1, tn), lambda i, j, k: (0, j)))
        else:
            in_specs.append(pl.BlockSpec((tm, tn), lambda i, j, k: (i, j)))
        args.append(bias)

    aliases = {}
    if out_buf is not None:
        in_specs.append(pl.BlockSpec(memory_space=pl.ANY))
        args.append(out_buf)
        aliases = {len(args) - 1: 0}
    n_in = len(args)
    width, off = out_cols if out_cols is not None else (N, 0)
    assert off % tn == 0

    def body(*refs):
        a_ref, b_ref = refs[0], refs[1]
        bias_ref = refs[2] if bias is not None else None
        o_ref = refs[n_in]
        av, bv = a_ref[...], b_ref[...]
        if a_silu:
            av = _silu(av.astype(F32))
        if exact:
            p = lax.dot_general(av.astype(F32), bv.astype(F32), dims, preferred_element_type=F32, precision=HI)
        else:
            p = lax.dot_general(av.astype(BF16), bv.astype(BF16), dims, preferred_element_type=F32)

        def fin(r):
            if bias_ref is not None:
                r = r + bias_ref[...]
            o_ref[...] = r.astype(o_ref.dtype)

        if nk == 1:
            fin(p)
        else:
            acc = refs[-1]
            k = pl.program_id(2)

            @pl.when(k == 0)
            def _():
                acc[...] = p

            @pl.when(k > 0)
            def _():
                acc[...] += p

            @pl.when(k == nk - 1)
            def _():
                fin(acc[...])

    res = _pcall(
        body, args, name=name,
        out_shape=[jax.ShapeDtypeStruct((M, width), out_dtype)],
        grid=(M // tm, N // tn, nk),
        in_specs=in_specs,
        out_specs=[pl.BlockSpec((tm, tn), lambda i, j, k: (i, j + off // tn))],
        scratch=[pltpu.VMEM((tm, tn), F32)] if nk > 1 else [],
        sem=("parallel", "parallel", "arbitrary"), comm=comm, aliases=aliases)
    return res[0] if comm is None else res


def _rowmap(fn, rows, bcasts, outs, accs, *, name, tr=256, cw=None, comm=None):
    L = rows[0][0].shape[0]
    tr = _pick(L, tr)
    nr, nb, no, na = len(rows), len(bcasts), len(outs), len(accs)
    if cw is None:
        ncol = 1
        widths = [w for (_, _, w) in rows]
    else:
        wtot = rows[0][2]
        ncol = wtot // cw
        widths = [cw] * nr
    in_specs, args = [], []
    for (arr, off, w), bw in zip(rows, widths):
        assert off % bw == 0
        in_specs.append(pl.BlockSpec((tr, bw), functools.partial(lambda j, i, o: (i, o + j), o=off // bw)))
        args.append(arr)
    for arr in bcasts:
        bw = arr.shape[1] if cw is None else cw
        in_specs.append(pl.BlockSpec((arr.shape[0], bw), lambda j, i: (0, j)))
        args.append(arr)
    out_shape, out_specs = [], []
    for spec in outs:
        if len(spec) == 2:
            (w, dt), off = spec, 0
            bw = w if cw is None else cw
        else:
            w, dt, off, bw = spec
        out_shape.append(jax.ShapeDtypeStruct((L, w), dt))
        out_specs.append(pl.BlockSpec((tr, bw), functools.partial(lambda j, i, o: (i, o + j), o=off // bw)))
    for (r, w) in accs:
        bw = w if cw is None else cw
        out_shape.append(jax.ShapeDtypeStruct((r, w), F32))
        out_specs.append(pl.BlockSpec((r, bw), lambda j, i: (0, j)))

    def body(*refs):
        ins = [r[...] for r in refs[:nr + nb]]
        o_refs = refs[nr + nb:nr + nb + no]
        a_refs = refs[nr + nb + no:]
        o, a = fn(*ins)
        for ref, val in zip(o_refs, o):
            ref[...] = val.astype(ref.dtype)
        if na:
            @pl.when(pl.program_id(1) == 0)
            def _():
                for ref in a_refs:
                    ref[...] = jnp.zeros_like(ref)

            for ref, val in zip(a_refs, a):
                ref[...] += val

    return _pcall(body, args, name=name, out_shape=out_shape, grid=(ncol, L // tr), in_specs=in_specs,
                  out_specs=out_specs, sem=("parallel", "arbitrary"), comm=comm)


def _csum(v):
    return jnp.sum(v, axis=0, keepdims=True)


def _shift_rows(v, s, rows):
    if s == 0:
        return v
    n = v.shape[0]
    r = pltpu.roll(v, s % n, 0)
    if s > 0:
        return jnp.where(rows >= s, r, 0.0)
    return jnp.where(rows < n + s, r, 0.0)


def _conv_fwd(proj, off, width, w, b, *, name, tc=256):
    L = proj.shape[0]
    tc = _pick(width, tc)

    def body(x_ref, w_ref, b_ref, o_ref):
        x = x_ref[...]
        rows = lax.broadcasted_iota(jnp.int32, x.shape, 0)
        acc = jnp.zeros_like(x) + b_ref[...]
        for k in range(SSD_CONV_W):
            acc = acc + w_ref[k:k + 1, :] * _shift_rows(x, SSD_CONV_W - 1 - k, rows)
        o_ref[...] = _silu(acc)

    return pl.pallas_call(
        body, name=name, out_shape=jax.ShapeDtypeStruct((L, width), F32), grid=(width // tc,),
        in_specs=[pl.BlockSpec((L, tc), functools.partial(lambda j, o: (0, o + j), o=off // tc)),
                  pl.BlockSpec((SSD_CONV_W, tc), lambda j: (0, j)), pl.BlockSpec((1, tc), lambda j: (0, j))],
        out_specs=pl.BlockSpec((L, tc), lambda j: (0, j)),
        compiler_params=_cp(("parallel",)),
    )(proj, w, b)


def _conv_bwd(proj, off, width, w, b, dys, dproj, *, name, tc=256, comm=None):
    L = proj.shape[0]
    tc = _pick(width, tc)
    ntile = [d.shape[1] // tc for d in dys]
    assert all(d.shape[1] % tc == 0 for d in dys) and sum(ntile) == width // tc
    first = [sum(ntile[:i]) for i in range(len(dys))]

    def body(x_ref, w_ref, b_ref, *rest):
        dy_refs, (dx_ref, dw_ref, db_ref) = rest[:len(dys)], rest[len(dys) + 1:]
        j = pl.program_id(0)
        dy = dy_refs[0][...]
        for i in range(1, len(dys)):
            dy = jnp.where(j >= first[i], dy_refs[i][...], dy)
        x = x_ref[...]
        rows = lax.broadcasted_iota(jnp.int32, x.shape, 0)
        xs = [_shift_rows(x, SSD_CONV_W - 1 - k, rows) for k in range(SSD_CONV_W)]
        pre = jnp.zeros_like(x) + b_ref[...]
        for k in range(SSD_CONV_W):
            pre = pre + w_ref[k:k + 1, :] * xs[k]
        dpre = dy * _dsilu(pre)
        dx = jnp.zeros_like(x)
        for k in range(SSD_CONV_W):
            dx = dx + w_ref[k:k + 1, :] * _shift_rows(dpre, -(SSD_CONV_W - 1 - k), rows)
            dw_ref[k:k + 1, :] = _csum(dpre * xs[k])
        dx_ref[...] = dx.astype(dx_ref.dtype)
        db_ref[...] = _csum(dpre)

    dy_specs = [pl.BlockSpec((L, tc), functools.partial(lambda j, f, n: (0, jnp.clip(j - f, 0, n - 1)), f=f, n=n))
                for f, n in zip(first, ntile)]
    shifted = pl.BlockSpec((L, tc), functools.partial(lambda j, o: (0, o + j), o=off // tc))
    return _pcall(
        body, (proj, w, b, *dys, dproj), name=name,
        out_shape=[jax.ShapeDtypeStruct(dproj.shape, BF16), jax.ShapeDtypeStruct((SSD_CONV_W, width), F32),
                   jax.ShapeDtypeStruct((1, width), F32)],
        grid=(width // tc,),
        in_specs=[shifted, pl.BlockSpec((SSD_CONV_W, tc), lambda j: (0, j)), pl.BlockSpec((1, tc), lambda j: (0, j))]
        + dy_specs + [pl.BlockSpec(memory_space=pl.ANY)],
        out_specs=[shifted, pl.BlockSpec((SSD_CONV_W, tc), lambda j: (0, j)), pl.BlockSpec((1, tc), lambda j: (0, j))],
        sem=("parallel",), comm=comm, aliases={3 + len(dys): 0})


def _tri(Q):
    ri = lax.broadcasted_iota(jnp.int32, (Q, Q), 0)
    ci = lax.broadcasted_iota(jnp.int32, (Q, Q), 1)
    return ri >= ci, ri <= ci


def _ssd_prep(dt_raw, bias, alog, *, name):
    L, W = dt_raw.shape
    Q = SSD_CHUNK

    def body(r_ref, b_ref, al_ref, dt_ref, a_ref):
        lower, _ = _tri(Q)
        dt = _softplus(r_ref[...] + b_ref[...])
        dt_ref[...] = dt
        a_ref[...] = _hi(lower.astype(F32), dt * (-jnp.exp(al_ref[...])))

    blk = pl.BlockSpec((Q, W), lambda c: (c, 0))
    one = pl.BlockSpec((1, W), lambda c: (0, 0))
    sd = jax.ShapeDtypeStruct((L, W), F32)
    return _pcall(body, (dt_raw, bias, alog), name=name, out_shape=[sd, sd], grid=(L // Q,),
                  in_specs=[blk, one, one], out_specs=[blk, blk], sem=("parallel",))


def _ssd_post(da, dar, s1, dt, dt_raw, bias, alog, dproj, col_off, *, name):
    L, W = da.shape
    Q = SSD_CHUNK

    def body(da_ref, dar_ref, s1_ref, dt_ref, r_ref, b_ref, al_ref, buf_ref, o_ref, db_ref, dal_ref):
        _, upper = _tri(Q)
        A = -jnp.exp(al_ref[...])
        ddtA = _hi(upper.astype(F32), da_ref[...] - dar_ref[...])
        ddt_raw = (ddtA * A + s1_ref[...]) * jax.nn.sigmoid(r_ref[...] + b_ref[...])
        o_ref[...] = ddt_raw.astype(o_ref.dtype)

        @pl.when(pl.program_id(0) == 0)
        def _():
            db_ref[...] = jnp.zeros_like(db_ref)
            dal_ref[...] = jnp.zeros_like(dal_ref)

        db_ref[...] += _csum(ddt_raw)
        dal_ref[...] += _csum(ddtA * dt_ref[...]) * A

    blk = pl.BlockSpec((Q, W), lambda c: (c, 0))
    one = pl.BlockSpec((1, W), lambda c: (0, 0))
    return _pcall(body, (da, dar, s1, dt, dt_raw, bias, alog, dproj), name=name,
                  out_shape=[jax.ShapeDtypeStruct(dproj.shape, dproj.dtype), jax.ShapeDtypeStruct((1, W), F32),
                             jax.ShapeDtypeStruct((1, W), F32)],
                  grid=(L // Q,),
                  in_specs=[blk, blk, blk, blk, blk, one, one, pl.BlockSpec(memory_space=pl.ANY)],
                  out_specs=[pl.BlockSpec((Q, W), lambda c: (c, col_off // W)), one, one], aliases={7: 0},
                  sem=("arbitrary",))


def _head_sum(v, K, KP):
    P = KP // K
    t_r = lax.broadcasted_iota(jnp.int32, (KP, K), 0)
    t_c = lax.broadcasted_iota(jnp.int32, (KP, K), 1)
    Et = ((t_r >= t_c * P) & (t_r < (t_c + 1) * P)).astype(BF16)
    hi = v.astype(BF16)
    lo = (v - hi.astype(F32)).astype(BF16)
    return _nn(hi, Et) + _nn(lo, Et)


def _half_masks():
    li = lax.broadcasted_iota(jnp.int32, (1, LANE), 1)
    return [(li < SSD_HEAD_DIM).astype(F32), (li >= SSD_HEAD_DIM).astype(F32)]


def _ssd_specs(K, KP, d_inner, rev, nc):
    Q, N, G = SSD_CHUNK, SSD_D_STATE, SSD_N_GROUPS
    cidx = (lambda c: nc - 1 - c) if rev else (lambda c: c)
    b_off, c_off = d_inner // N, d_inner // N + G
    return [
        pl.BlockSpec((Q, KP), lambda g, c: (cidx(c), g)),
        pl.BlockSpec((Q, N), lambda g, c: (cidx(c), b_off + g)),
        pl.BlockSpec((Q, N), lambda g, c: (cidx(c), c_off + g)),
        pl.BlockSpec((None, Q, K), lambda g, c: (g, cidx(c), 0)),
        pl.BlockSpec((None, Q, K), lambda g, c: (g, cidx(c), 0)),
        pl.BlockSpec((None, K, Q), lambda g, c: (g, 0, cidx(c))),
        pl.BlockSpec((1, KP), lambda g, c: (0, g)),
    ]


def _expand_heads(vc, K):
    Q = vc.shape[0]
    left = lax.broadcasted_iota(jnp.int32, (Q, LANE), 1) < SSD_HEAD_DIM
    parts = []
    for pr in range(K // 2):
        a = jnp.broadcast_to(vc[:, 2 * pr:2 * pr + 1], (Q, LANE))
        b = jnp.broadcast_to(vc[:, 2 * pr + 1:2 * pr + 2], (Q, LANE))
        parts.append(jnp.where(left, a, b))
    return jnp.concatenate(parts, axis=1) if len(parts) > 1 else parts[0]


def _ssd_fwd(xbc, dt_c, a_c, a_r, d_full, *, d_inner, name, comm=None):
    L = xbc.shape[0]
    G, N, Q, P = SSD_N_GROUPS, SSD_D_STATE, SSD_CHUNK, SSD_HEAD_DIM
    KP = d_inner // G
    K = KP // P
    nc = L // Q
    npair = KP // LANE

    def body(xs_ref, b_ref, c_ref, dtc_ref, ac_ref, ar_ref, df_ref, y_ref, st_ref, S):
        @pl.when(pl.program_id(1) == 0)
        def _():
            S[...] = jnp.zeros_like(S)

        lower, _ = _tri(Q)
        st_ref[...] = S[...]
        xs = xs_ref[...]
        Bm, Cm = b_ref[...], c_ref[...]
        Bb, Cb = Bm.astype(BF16), Cm.astype(BF16)
        a_c, a_r = ac_ref[...], ar_ref[...]
        a_f = _expand_heads(a_c, K)
        X = xs * _expand_heads(dtc_ref[...], K)
        ea = jnp.exp(a_f)
        alast = a_f[Q - 1:Q, :]
        tail = jnp.exp(alast - a_f)
        cb = _nt(Cb, Bb)
        Sv = S[...]
        yoff = _nn(Cb, Sv.astype(BF16)) * ea
        skip = xs * df_ref[...]
        masks = _half_masks()
        for pr in range(npair):
            Xp = X[:, pr * LANE:(pr + 1) * LANE]
            acc = yoff[:, pr * LANE:(pr + 1) * LANE] + skip[:, pr * LANE:(pr + 1) * LANE]
            for hh in range(2):
                k = 2 * pr + hh
                seg = a_c[:, k:k + 1] - a_r[k:k + 1, :]
                dec = jnp.where(lower, jnp.exp(jnp.minimum(seg, 0.0)), 0.0)
                acc = acc + _nn((cb * dec).astype(BF16), (Xp * masks[hh]).astype(BF16))
            y_ref[:, pr * LANE:(pr + 1) * LANE] = acc
        Bt = Bm.T
        S[...] = Sv * jnp.exp(alast) + _nn(Bt.astype(BF16), (X * tail).astype(BF16))

    return _pcall(
        body, (xbc, xbc, xbc, dt_c, a_c, a_r, d_full), name=name,
        out_shape=[jax.ShapeDtypeStruct((L, d_inner), F32), jax.ShapeDtypeStruct((G, nc, N, KP), F32)],
        grid=(G, nc),
        in_specs=_ssd_specs(K, KP, d_inner, False, nc),
        out_specs=[pl.BlockSpec((Q, KP), lambda g, c: (c, g)), pl.BlockSpec((None, None, N, KP), lambda g, c: (g, c, 0, 0))],
        scratch=[pltpu.VMEM((N, KP), F32)],
        sem=("parallel", "arbitrary"), comm=comm)


def _ssd_bwd(xbc, dt_c, a_c, a_r, d_full, states, dy, *, d_inner, name, comm=None):
    L = xbc.shape[0]
    G, N, Q, P = SSD_N_GROUPS, SSD_D_STATE, SSD_CHUNK, SSD_HEAD_DIM
    KP = d_inner // G
    K = KP // P
    nc = L // Q
    npair = KP // LANE

    def body(xs_ref, b_ref, c_ref, dtc_ref, ac_ref, ar_ref, df_ref, st_ref, dy_ref,
             dxs_ref, db_ref, dc_ref, da_ref, dar_ref, s1_ref, dd_ref, dS):
        @pl.when(pl.program_id(1) == 0)
        def _():
            dS[...] = jnp.zeros_like(dS)
            dd_ref[...] = jnp.zeros_like(dd_ref)

        lower, _ = _tri(Q)
        a_c, a_r = ac_ref[...], ar_ref[...]
        a_f, dt_f = _expand_heads(a_c, K), _expand_heads(dtc_ref[...], K)
        xs = xs_ref[...]
        Bm, Cm = b_ref[...], c_ref[...]
        Bb, Cb = Bm.astype(BF16), Cm.astype(BF16)
        dY = dy_ref[...]
        X = xs * dt_f
        ea = jnp.exp(a_f)
        alast = a_f[Q - 1:Q, :]
        tail = jnp.exp(alast - a_f)
        el = jnp.exp(alast)
        Sv, dSn = st_ref[...], dS[...]
        Sb, dSb = Sv.astype(BF16), dSn.astype(BF16)
        cb = _nt(Cb, Bb)
        yoff_raw = _nn(Cb, Sb)
        dYe = dY * ea
        dC = _nt(dYe.astype(BF16), Sb)
        dS[...] = dSn * el + _nn(Cm.T.astype(BF16), dYe.astype(BF16))
        Gx = _nn(Bb, dSb)
        dB = _nt((X * tail).astype(BF16), dSb)
        dtl = Gx * X * tail
        da_f = dYe * yoff_raw - dtl
        dalast_f = _csum(dtl) + _csum(dSn * Sv) * el
        da_c = _head_sum(da_f, K, KP)
        onek = lax.broadcasted_iota(jnp.int32, (1, K), 1)
        onek_col = lax.broadcasted_iota(jnp.int32, (K, 1), 0)
        masks = _half_masks()
        dcb = jnp.zeros((Q, Q), F32)
        da_r = jnp.zeros((K, Q), F32)
        dX_parts = []
        for pr in range(npair):
            Xp = X[:, pr * LANE:(pr + 1) * LANE]
            dYp = dY[:, pr * LANE:(pr + 1) * LANE]
            dXp = Gx[:, pr * LANE:(pr + 1) * LANE] * tail[:, pr * LANE:(pr + 1) * LANE]
            for hh in range(2):
                k = 2 * pr + hh
                Xk = (Xp * masks[hh]).astype(BF16)
                dYk = (dYp * masks[hh]).astype(BF16)
                seg = a_c[:, k:k + 1] - a_r[k:k + 1, :]
                dec = jnp.where(lower, jnp.exp(jnp.minimum(seg, 0.0)), 0.0)
                Mk = cb * dec
                dM = _nt(dYk, Xk)
                dcb = dcb + dM * dec
                Gk = dM * Mk
                da_c = da_c + jnp.sum(Gk, axis=1, keepdims=True) * (onek == k).astype(F32)
                da_r = da_r + (onek_col == k).astype(F32) * jnp.sum(Gk, axis=0, keepdims=True)
                dXp = dXp + _tn(Mk.astype(BF16), dYk)
            dX_parts.append(dXp)
        dX = jnp.concatenate(dX_parts, axis=1) if npair > 1 else dX_parts[0]
        dcbb = dcb.astype(BF16)
        dC = dC + _nn(dcbb, Bb)
        dB = dB + _tn(dcbb, Cb)
        lastrow = (lax.broadcasted_iota(jnp.int32, (Q, 1), 0) == Q - 1).astype(F32)
        da_ref[...] = da_c + lastrow * _head_sum(dalast_f, K, KP)
        dar_ref[...] = da_r
        s1_ref[...] = _head_sum(dX * xs, K, KP)
        dd_ref[...] += _csum(dY * xs)
        dxs_ref[...] = dX * dt_f + dY * df_ref[...]
        db_ref[...] = dB
        dc_ref[...] = dC

    rc = lambda c: nc - 1 - c
    tok = jax.ShapeDtypeStruct((G, L, K), F32)
    tok_spec = pl.BlockSpec((None, Q, K), lambda g, c: (g, rc(c), 0))
    return _pcall(
        body, (xbc, xbc, xbc, dt_c, a_c, a_r, d_full, states, dy), name=name,
        out_shape=[jax.ShapeDtypeStruct((L, d_inner), F32), jax.ShapeDtypeStruct((L, G * N), F32),
                   jax.ShapeDtypeStruct((L, G * N), F32), tok, jax.ShapeDtypeStruct((G, K, L), F32), tok,
                   jax.ShapeDtypeStruct((1, d_inner), F32)],
        grid=(G, nc),
        in_specs=_ssd_specs(K, KP, d_inner, True, nc) + [
            pl.BlockSpec((None, None, N, KP), lambda g, c: (g, rc(c), 0, 0)),
            pl.BlockSpec((Q, KP), lambda g, c: (rc(c), g))],
        out_specs=[pl.BlockSpec((Q, KP), lambda g, c: (rc(c), g)), pl.BlockSpec((Q, N), lambda g, c: (rc(c), g)),
                   pl.BlockSpec((Q, N), lambda g, c: (rc(c), g)), tok_spec,
                   pl.BlockSpec((None, K, Q), lambda g, c: (g, 0, rc(c))), tok_spec,
                   pl.BlockSpec((1, KP), lambda g, c: (0, g))],
        scratch=[pltpu.VMEM((N, KP), F32)],
        sem=("parallel", "arbitrary"), comm=comm)


def _slopes():
    n = DIL_N_GROUPS * DIL_HEADS
    s = 2.0 ** (-8.0 * np.arange(1, n + 1) / n)
    return s.reshape(DIL_N_GROUPS, DIL_HEADS).astype(np.float32)


ATT_TB = 2048


def _tn(a, b):
    return lax.dot_general(a, b, (((0,), (0,)), ((), ())), preferred_element_type=F32)


def _slope_rows(g):
    return jnp.asarray(np.repeat(_slopes()[g][:, None], LANE, axis=1))


def _attn_bias(slope_row, d):
    B = DIL_BLOCK
    qi = lax.broadcasted_iota(jnp.int32, (B, B), 0)
    kj = lax.broadcasted_iota(jnp.int32, (B, B), 1)
    sd = slope_row * float(d)
    cur = jnp.where(kj <= qi, -(qi - kj).astype(F32) * sd, NEG)
    prv = jnp.where(kj >= qi, -(qi + B - kj).astype(F32) * sd, NEG)
    return cur, prv


def _sub_rows(j, r, d):
    base = j * DIL_BLOCK * d + r
    return pl.ds(base, DIL_BLOCK, stride=d) if d > 1 else pl.ds(base, DIL_BLOCK)


def _attn_geometry(L, g):
    window, d = DIL_PATTERNS[g]
    tb = min(ATT_TB, L)
    assert window // d == DIL_BLOCK and tb % (d * DIL_BLOCK) == 0 and L % tb == 0
    return d, tb, L // tb, tb // (d * DIL_BLOCK)


def _kv_specs(g, tb, nb):
    E, nh = DIL_HEAD_DIM, DIL_N_GROUPS * DIL_HEADS
    prev = lambda b: jnp.maximum(b - 1, 0)
    return [pl.BlockSpec((tb, E), lambda b, h: (b, g * DIL_HEADS + h)),
            pl.BlockSpec((tb, E), lambda b, h: (b, nh + g * DIL_HEADS + h)),
            pl.BlockSpec((tb, E), lambda b, h: (prev(b), g * DIL_HEADS + h)),
            pl.BlockSpec((tb, E), lambda b, h: (prev(b), nh + g * DIL_HEADS + h))]


def _attn_fwd(qz, kv, g, *, name):
    L = qz.shape[0]
    d, tb, nb, nj = _attn_geometry(L, g)
    B, E = DIL_BLOCK, DIL_HEAD_DIM
    scale = E ** -0.5

    def body(sl_ref, q_ref, kc_ref, vc_ref, kp_ref, vp_ref, o_ref, lse_ref):
        b, h = pl.program_id(0), pl.program_id(1)

        @pl.when(h == 0)
        def _():
            lse_ref[...] = jnp.zeros_like(lse_ref)

        bias_c, bias_p = _attn_bias(sl_ref[pl.ds(h, 1), :], d)
        bias_p0 = jnp.where(b > 0, bias_p, NEG)
        oneh = (lax.broadcasted_iota(jnp.int32, (1, LANE), 1) == h).astype(F32)
        for r in range(d):
            for j in range(nj):
                rows = _sub_rows(j, r, d)
                qs = q_ref[rows, :].astype(BF16)
                kc, vc = kc_ref[rows, :].astype(BF16), vc_ref[rows, :].astype(BF16)
                if j > 0:
                    prows, bp = _sub_rows(j - 1, r, d), bias_p
                    kq, vq = kc_ref[prows, :].astype(BF16), vc_ref[prows, :].astype(BF16)
                else:
                    prows, bp = _sub_rows(nj - 1, r, d), bias_p0
                    kq, vq = kp_ref[prows, :].astype(BF16), vp_ref[prows, :].astype(BF16)
                sc = _nt(qs, kc) * scale + bias_c
                sp = _nt(qs, kq) * scale + bp
                m = jnp.maximum(jnp.max(sc, axis=1, keepdims=True), jnp.max(sp, axis=1, keepdims=True))
                pc, pp = jnp.exp(sc - m), jnp.exp(sp - m)
                den = jnp.sum(pc, axis=1, keepdims=True) + jnp.sum(pp, axis=1, keepdims=True)
                o = _nn(pc.astype(BF16), vc) + _nn(pp.astype(BF16), vq)
                o_ref[rows, :] = o / den
                lse_ref[rows, :] = lse_ref[rows, :] + (m + jnp.log(den)) * oneh

    return _pcall(
        body, (_slope_rows(g), qz, kv, kv, kv, kv), name=name,
        out_shape=[jax.ShapeDtypeStruct((L, DIL_W), F32), jax.ShapeDtypeStruct((L, LANE), F32)],
        grid=(nb, DIL_HEADS),
        in_specs=[pl.BlockSpec((DIL_HEADS, LANE), lambda b, h: (0, 0)),
                  pl.BlockSpec((tb, E), lambda b, h: (b, g * DIL_HEADS + h))] + _kv_specs(g, tb, nb),
        out_specs=[pl.BlockSpec((tb, E), lambda b, h: (b, h)), pl.BlockSpec((tb, LANE), lambda b, h: (b, 0))],
        sem=("parallel", "arbitrary"))


def _attn_bwd(qz, kv, do, lse, dl, dqz, dk, dv, g, *, name):
    L = qz.shape[0]
    d, tb, nb, nj = _attn_geometry(L, g)
    B, E = DIL_BLOCK, DIL_HEAD_DIM
    scale = E ** -0.5
    nxt = lambda b: jnp.minimum(b + 1, nb - 1)
    fresh = dk is None

    def body(sl_ref, qc_ref, qn_ref, kc_ref, vc_ref, kp_ref, vp_ref, doc_ref, don_ref, lsec_ref, lsen_ref,
             dlc_ref, dln_ref, *rest):
        dq_ref, dk_ref, dv_ref = rest[-3:]
        b, h = pl.program_id(0), pl.program_id(1)
        bias_c, bias_p = _attn_bias(sl_ref[pl.ds(h, 1), :], d)
        bias_first = jnp.where(b > 0, bias_p, NEG)
        bias_last = jnp.where(b < nb - 1, bias_p, NEG)
        oneh = (lax.broadcasted_iota(jnp.int32, (1, LANE), 1) == h).astype(F32)

        def col(ref, rows):
            return jnp.sum(ref[rows, :] * oneh, axis=1, keepdims=True)

        def pair(q, do_, lse_, dl_, k_, v_, bias):
            p = jnp.exp(_nt(q, k_) * scale + bias - lse_)
            ds = p * (_nt(do_, v_) - dl_)
            return p.astype(BF16), ds.astype(BF16)

        for r in range(d):
            rows0 = _sub_rows(0, r, d)
            qj, doj = qc_ref[rows0, :].astype(BF16), doc_ref[rows0, :].astype(BF16)
            lsej, dlj = col(lsec_ref, rows0), col(dlc_ref, rows0)
            prows = _sub_rows(nj - 1, r, d)
            kq, vq = kp_ref[prows, :].astype(BF16), vp_ref[prows, :].astype(BF16)
            _, ds = pair(qj, doj, lsej, dlj, kq, vq, bias_first)
            dq_carry = _nn(ds, kq)
            for j in range(nj):
                rows = _sub_rows(j, r, d)
                kj, vj = kc_ref[rows, :].astype(BF16), vc_ref[rows, :].astype(BF16)
                p, ds = pair(qj, doj, lsej, dlj, kj, vj, bias_c)
                dq_ref[rows, :] = (dq_carry + _nn(ds, kj)) * scale
                dkj, dvj = _tn(ds, qj), _tn(p, doj)
                if j < nj - 1:
                    nrows = _sub_rows(j + 1, r, d)
                    qn, don = qc_ref[nrows, :].astype(BF16), doc_ref[nrows, :].astype(BF16)
                    lsen, dln, bias = col(lsec_ref, nrows), col(dlc_ref, nrows), bias_p
                else:
                    qn, don = qn_ref[rows0, :].astype(BF16), don_ref[rows0, :].astype(BF16)
                    lsen, dln, bias = col(lsen_ref, rows0), col(dln_ref, rows0), bias_last
                p2, ds2 = pair(qn, don, lsen, dln, kj, vj, bias)
                dk_ref[rows, :] = (dkj + _tn(ds2, qn)) * scale
                dv_ref[rows, :] = dvj + _tn(p2, don)
                dq_carry = _nn(ds2, kj)
                qj, doj, lsej, dlj = qn, don, lsen, dln

    hb = lambda b, h: (b, g * DIL_HEADS + h)
    anyspec = pl.BlockSpec(memory_space=pl.ANY)
    args = [_slope_rows(g), qz, qz, kv, kv, kv, kv, do, do, lse, lse, dl, dl, dqz] + ([] if fresh else [dk, dv])
    in_specs = ([pl.BlockSpec((DIL_HEADS, LANE), lambda b, h: (0, 0)),
                 pl.BlockSpec((tb, E), hb), pl.BlockSpec((tb, E), lambda b, h: (nxt(b), g * DIL_HEADS + h))]
                + _kv_specs(g, tb, nb)
                + [pl.BlockSpec((tb, E), lambda b, h: (b, h)), pl.BlockSpec((tb, E), lambda b, h: (nxt(b), h)),
                   pl.BlockSpec((tb, LANE), lambda b, h: (b, 0)), pl.BlockSpec((tb, LANE), lambda b, h: (nxt(b), 0)),
                   pl.BlockSpec((tb, LANE), lambda b, h: (b, 0)), pl.BlockSpec((tb, LANE), lambda b, h: (nxt(b), 0)),
                   anyspec] + ([] if fresh else [anyspec, anyspec]))
    aliases = {13: 0} if fresh else {13: 0, 14: 1, 15: 2}
    dkv_sd = jax.ShapeDtypeStruct((L, DIL_N_GROUPS * DIL_W), F32)
    return pl.pallas_call(
        body, name=name,
        out_shape=[jax.ShapeDtypeStruct(dqz.shape, F32), dkv_sd, dkv_sd],
        grid=(nb, DIL_HEADS), in_specs=in_specs,
        out_specs=[pl.BlockSpec((tb, E), hb), pl.BlockSpec((tb, E), hb), pl.BlockSpec((tb, E), hb)],
        input_output_aliases=aliases,
        compiler_params=_cp(("parallel", "parallel")))(*args)


def _head_expand():
    r = lax.broadcasted_iota(jnp.int32, (LANE, DIL_W), 0)
    c = lax.broadcasted_iota(jnp.int32, (LANE, DIL_W), 1)
    E = ((c >= r * DIL_HEAD_DIM) & (c < (r + 1) * DIL_HEAD_DIM)).astype(F32)
    r2 = lax.broadcasted_iota(jnp.int32, (DIL_W, LANE), 0)
    c2 = lax.broadcasted_iota(jnp.int32, (DIL_W, LANE), 1)
    Et = ((r2 >= c2 * DIL_HEAD_DIM) & (r2 < (c2 + 1) * DIL_HEAD_DIM)).astype(F32)
    return E, Et


def _merge_weights(l0, l1, l2):
    m = jnp.maximum(jnp.maximum(l0, l1), l2)
    e = [jnp.exp(l - m) for l in (l0, l1, l2)]
    tot = e[0] + e[1] + e[2]
    return [v / tot for v in e]


def _merge_fwd(os_, lses, qz, z_off, *, name):
    def fn(o0, o1, o2, l0, l1, l2, z):
        E, _ = _head_expand()
        w = _merge_weights(l0, l1, l2)
        om = sum(_dot2(wg, E) * og for wg, og in zip(w, (o0, o1, o2)))
        return [om * _silu(z)], []

    rows = [(o, 0, DIL_W) for o in os_] + [(l, 0, LANE) for l in lses] + [(qz, z_off, DIL_W)]
    return _rowmap(fn, rows, [], [(DIL_W, BF16)], [], name=name)[0]


def _merge_bwd(os_, lses, qz, z_off, dog, *, name, comm=None):
    def fn(o0, o1, o2, l0, l1, l2, z, dg):
        E, Et = _head_expand()
        dg = dg.astype(F32)
        w = _merge_weights(l0, l1, l2)
        wf = [_dot2(wg, E) for wg in w]
        os3 = (o0, o1, o2)
        om = sum(a * b for a, b in zip(wf, os3))
        dom = dg * _silu(z)
        dz = dg * om * _dsilu(z)
        dw = [_dot2(dom * og, Et) for og in os3]
        tot = sum(a * b for a, b in zip(w, dw))
        return [wf[0] * dom, wf[1] * dom, wf[2] * dom, w[0] * tot, w[1] * tot, w[2] * tot, dz], []

    rows = ([(o, 0, DIL_W) for o in os_] + [(l, 0, LANE) for l in lses] + [(qz, z_off, DIL_W), (dog, 0, DIL_W)])
    outs = [(DIL_W, F32)] * 3 + [(LANE, F32)] * 3 + [(qz.shape[1], F32, z_off, DIL_W)]
    return _rowmap(fn, rows, [], outs, [], name=name, comm=comm)


def _adamw(gparts, w, m, v, *, name, tr=128):
    n, R, C = gparts.shape
    tr = _pick(R, tr)
    c1 = 1.0 - ADAM_B1 ** ADAM_STEP
    c2 = 1.0 - ADAM_B2 ** ADAM_STEP

    def body(g_ref, w_ref, m_ref, v_ref, go_ref, d_ref, mo_ref, vo_ref):
        g = g_ref[0].astype(F32)
        for i in range(1, n):
            g = g + g_ref[i].astype(F32)
        mn = ADAM_B1 * m_ref[...] + (1.0 - ADAM_B1) * g
        vn = ADAM_B2 * v_ref[...] + (1.0 - ADAM_B2) * jnp.square(g)
        d_ref[...] = -ADAM_LR * ((mn / c1) / (jnp.sqrt(vn / c2) + ADAM_EPS) + ADAM_WD * w_ref[...])
        go_ref[...] = g
        mo_ref[...] = mn
        vo_ref[...] = vn

    blk = pl.BlockSpec((tr, C), lambda i: (i, 0))
    sd = jax.ShapeDtypeStruct((R, C), F32)
    return pl.pallas_call(
        body, name=name, out_shape=[sd, sd, sd, sd], grid=(R // tr,),
        in_specs=[pl.BlockSpec((n, tr, C), lambda i: (0, i, 0)), blk, blk, blk],
        out_specs=[blk, blk, blk, blk],
        compiler_params=_cp(("parallel",)),
    )(gparts, w, m, v)


def _sum_parts(parts, *, name):
    n, R, C = parts.shape

    def body(p_ref, o_ref):
        s = p_ref[0]
        for i in range(1, n):
            s = s + p_ref[i]
        o_ref[...] = s

    return pl.pallas_call(
        body, name=name, out_shape=jax.ShapeDtypeStruct((R, C), F32),
        in_specs=[pl.BlockSpec(memory_space=pltpu.VMEM)], out_specs=pl.BlockSpec(memory_space=pltpu.VMEM),
    )(parts)


def _cols_from(g):
    _, R, Cs = g.shape
    return jnp.transpose(g, (1, 0, 2)).reshape(R, N_DEV * Cs)


def _col_parts(dw):
    R, C = dw.shape
    return jnp.transpose(dw.reshape(R, N_DEV, C // N_DEV), (1, 0, 2))


def _ag_cols(w_loc):
    if w_loc.shape[1] % LANE == 0:
        return _ag_comm(w_loc, cols=True), (lambda g: g)
    return _ag_comm(w_loc), _cols_from


def _rs_cols(dw):
    if (dw.shape[1] // N_DEV) % LANE == 0:
        return _a2a_comm(dw, cols=True)
    return _a2a_comm(_col_parts(dw))


def kernel(x, c, ada_w, ada_b, ln_g, ln_b, a_in_w, a_conv_w, a_conv_b, a_dt_bias, a_A_log, a_D, a_norm_g, a_out_w, kv_w, b_in_w, b_out_w, loss_target, m_ada_w, m_ada_b, m_ln_g, m_ln_b, m_a_in_w, m_a_conv_w, m_a_conv_b, m_a_dt_bias, m_a_A_log, m_a_D, m_a_norm_g, m_a_out_w, m_kv_w, m_b_in_w, m_b_out_w, v_ada_w, v_ada_b, v_ln_g, v_ln_b, v_a_in_w, v_a_conv_w, v_a_conv_b, v_a_dt_bias, v_a_A_log, v_a_D, v_a_norm_g, v_a_out_w, v_kv_w, v_b_in_w, v_b_out_w):
    L, D = x.shape[1], x.shape[2]
    H = a_dt_bias.shape[1]
    d_inner = H * SSD_HEAD_DIM
    G, N, P = SSD_N_GROUPS, SSD_D_STATE, SSD_HEAD_DIM
    K = H // G
    KP = K * P
    conv_dim = d_inner + 2 * G * N
    in_dim = d_inner + conv_dim + H
    in_pad = d_inner + conv_dim + LANE
    assert H <= LANE and KP % LANE == 0 and L % SSD_CHUNK == 0
    me = 4 * lax.axis_index("x") + 2 * lax.axis_index("y") + lax.axis_index("c")
    x2d, tgt = x[0], loss_target[0]

    c_all = _all_gather(c, "ag_c").reshape(N_DEV, D)
    mods = []
    for l in range(DEPTH):
        ab = lax.dynamic_slice(ada_b[l], (me * (3 * D // N_DEV),), (3 * D // N_DEV,))[None]
        mods.append(_matmul(c_all, ada_w[l], name=f"mod{l}", exact=True, a_silu=True, bias=ab))
    mod_all = _all_gather(jnp.stack(mods), "ag_mod")
    mod_me = lax.dynamic_index_in_dim(jnp.transpose(mod_all, (2, 1, 0, 3)).reshape(N_DEV, DEPTH, 3 * D), me, 0, False)
    shift = [mod_me[l, None, 0:D] for l in range(DEPTH)]
    scale = [mod_me[l, None, D:2 * D] for l in range(DEPTH)]
    gate = [mod_me[l, None, 2 * D:3 * D] for l in range(DEPTH)]

    w_in = _cols_from(_all_gather(a_in_w[0].astype(BF16), "ag_a_in"))
    w_in = jnp.pad(w_in, ((0, 0), (0, in_pad - in_dim)))
    conv_w = _all_gather(a_conv_w[0], "ag_conv_w")
    conv_w = jnp.transpose(conv_w, (1, 0, 2)).reshape(SSD_CONV_W, conv_dim)
    conv_b = _all_gather(a_conv_b, "ag_conv_b").reshape(1, conv_dim)
    norm_g = _all_gather(a_norm_g, "ag_norm_g").reshape(1, d_inner)

    def modulate(xin, l, name):
        fn = lambda xv, sc, sh: ([xv * (1.0 + sc) + sh], [])
        return _rowmap(fn, [(xin, 0, D)], [scale[l], shift[l]], [(D, BF16)], [], name=name)[0]

    def ln_out(xv, yv, gt, g, b):
        u = DEEPNORM_ALPHA * xv + (1.0 + gt) * yv
        mu = jnp.mean(u, axis=1, keepdims=True)
        uc = u - mu
        var = jnp.mean(uc * uc, axis=1, keepdims=True)
        return uc * lax.rsqrt(var + LN_EPS) * g + b

    def ln0_fwd(xin, y, name):
        def fn(xv, yv, gt, g, b, sc, sh):
            o = ln_out(xv, yv, gt, g, b)
            return [o, o, o * (1.0 + sc) + sh], []
        return _rowmap(fn, [(xin, 0, D), (y, 0, D)], [gate[0], ln_g[0:1], ln_b[0:1], scale[1], shift[1]],
                       [(D, F32), (D, BF16), (D, BF16)], [], name=name)

    def ln1_loss(xin, y, target, name):
        def fn(xv, yv, tv, gt, g, b):
            e = ln_out(xv, yv, gt, g, b) - tv
            return [e * (1.0 / D)], [_csum(e * e) * (0.5 / D)]
        return _rowmap(fn, [(xin, 0, D), (y, 0, D), (target, 0, D)], [gate[1], ln_g[1:2], ln_b[1:2]],
                       [(D, F32)], [(1, D)], name=name)

    def ln_bwd(xin, y, dout, l, name):
        def fn(xv, yv, do, gt, g, b):
            u = DEEPNORM_ALPHA * xv + (1.0 + gt) * yv
            mu = jnp.mean(u, axis=1, keepdims=True)
            uc = u - mu
            var = jnp.mean(uc * uc, axis=1, keepdims=True)
            rs = lax.rsqrt(var + LN_EPS)
            xh = uc * rs
            dxh = do * g
            du = rs * (dxh - jnp.mean(dxh, axis=1, keepdims=True) - xh * jnp.mean(dxh * xh, axis=1, keepdims=True))
            return [DEEPNORM_ALPHA * du, (1.0 + gt) * du], [_csum(du * yv), _csum(do * xh), _csum(do)]
        return _rowmap(fn, [(xin, 0, D), (y, 0, D), (dout, 0, D)], [gate[l], ln_g[l:l + 1], ln_b[l:l + 1]],
                       [(D, F32), (D, BF16)], [(1, D)] * 3, name=name)

    def mod_bwd(xin, dh, dx_acc, l, name):
        def fn(xv, dhv, dxa, sc):
            return [dxa + dhv * (1.0 + sc)], [_csum(dhv * xv), _csum(dhv)]
        return _rowmap(fn, [(xin, 0, D), (dh, 0, D), (dx_acc, 0, D)], [scale[l]], [(D, F32)], [(1, D)] * 2, name=name)

    h0 = modulate(x2d, 0, "mod_h0")
    proj, g_aout = _matmul(h0, w_in, name="mm_a_in", tm=2048, tn=1152,
                           comm=_ag_comm(a_out_w[0].astype(BF16)))
    w_aout = g_aout.reshape(d_inner, D)
    xbc = _conv_fwd(proj, d_inner, conv_dim, conv_w, conv_b, name="conv_fwd")
    dt_raw = proj[:, d_inner + conv_dim:]
    padh = lambda a: jnp.pad(a, ((0, 0), (0, LANE - H)))
    bias_p, alog_p = padh(a_dt_bias), padh(a_A_log)
    dt_p, a_p = _ssd_prep(dt_raw, bias_p, alog_p, name="ssd_prep")
    dt_c = jnp.transpose(dt_p[:, :H].reshape(L, G, K), (1, 0, 2))
    a_c = jnp.transpose(a_p[:, :H].reshape(L, G, K), (1, 0, 2))
    a_r = jnp.transpose(a_c, (0, 2, 1))
    d_full = jnp.repeat(a_D.reshape(H), P)[None]
    ssd_in = (xbc, dt_c, a_c, a_r, d_full)
    cm_kv, fix_kv = _ag_cols(kv_w.astype(BF16))
    y_ssd, states, w_kv = _ssd_fwd(*ssd_in, d_inner=d_inner, name="ssd_fwd", comm=cm_kv)
    w_kv = fix_kv(w_kv)

    gw = d_inner // G

    def gnorm_fn(yv, zv, g):
        yg = yv * _silu(zv)
        r = lax.rsqrt(jnp.mean(yg * yg, axis=1, keepdims=True) + RMS_EPS)
        return [yg * r * g], []
    yn = _rowmap(gnorm_fn, [(y_ssd, 0, d_inner), (proj, 0, d_inner)], [norm_g], [(d_inner, BF16)], [],
                 name="gnorm_fwd", cw=gw, tr=1024)[0]
    cm_bin, fix_bin = _ag_cols(b_in_w[0].astype(BF16))
    ya, w_bin = _matmul(yn, w_aout, name="mm_a_out", comm=cm_bin)
    w_bin = fix_bin(w_bin)
    x1, x1b, h1 = ln0_fwd(x2d, ya, "ln0_fwd")

    cm_bout, fix_bout = _ag_cols(b_out_w[0].astype(BF16))
    kv, w_bout = _matmul(x1b, w_kv, name="mm_kv", comm=cm_bout)
    w_bout = fix_bout(w_bout)
    qz = _matmul(h1, w_bin, name="mm_b_in")
    z_off = DIL_N_GROUPS * DIL_W
    os_, lses = [], []
    for g in range(DIL_N_GROUPS):
        o, lse = _attn_fwd(qz, kv, g, name=f"attn_fwd{g}")
        os_.append(o)
        lses.append(lse)
    og = _merge_fwd(os_, lses, qz, z_off, name="merge_fwd")
    yb = _matmul(og, w_bout, name="mm_b_out")
    dx2, loss_cols = ln1_loss(x1, yb, tgt, "ln1_loss")
    loss = lax.psum(jnp.sum(loss_cols), ("x", "y", "c"))

    dx1a, dyb, dgate1, dlng1, dlnb1 = ln_bwd(x1, yb, dx2, 1, "ln1_bwd")
    dw_bout = _matmul(og, dyb, name="mm_b_out_dw", ta=True, out_dtype=BF16)
    dog = _matmul(dyb, w_bout, name="mm_b_out_dx", tb=True, out_dtype=BF16)
    do0, do1, do2, dl0, dl1, dl2, dqz, r_bout = _merge_bwd(os_, lses, qz, z_off, dog, name="merge_bwd",
                                                           comm=_rs_cols(dw_bout))
    dk = dv = None
    for g, (do_g, dl_g) in enumerate(zip((do0, do1, do2), (dl0, dl1, dl2))):
        dqz, dk, dv = _attn_bwd(qz, kv, do_g, lses[g], dl_g, dqz, dk, dv, g, name=f"attn_bwd{g}")
    dw_bin = _matmul(h1, dqz, name="mm_b_in_dw", ta=True, out_dtype=BF16)
    dh1 = _matmul(dqz, w_bin, name="mm_b_in_dx", tb=True)
    dx1b, dscale1, dshift1 = mod_bwd(x1, dh1, dx1a, 1, "mod1_bwd")
    kw = DIL_N_GROUPS * DIL_W
    dw_k = _matmul(x1b, dk, name="mm_k_dw", ta=True, out_dtype=BF16, out_cols=(2 * kw, 0))
    dw_kv = _matmul(x1b, dv, name="mm_v_dw", ta=True, out_dtype=BF16, out_cols=(2 * kw, kw), out_buf=dw_k)
    dx1k = _matmul(dk, w_kv, name="mm_k_dx", tb=True, bias=dx1b)
    dx1 = _matmul(dv, w_kv, name="mm_v_dx", tb=True, bias=dx1k, b_koff=kw)

    dxa, dya, dgate0, dlng0, dlnb0 = ln_bwd(x2d, ya, dx1, 0, "ln0_bwd")
    dw_aout = _matmul(yn, dya, name="mm_a_out_dw", ta=True, out_dtype=BF16)
    dyn = _matmul(dya, w_aout, name="mm_a_out_dx", tb=True)

    def gnorm_bwd_fn(yv, zv, dn, g):
        sz = _silu(zv)
        yg = yv * sz
        r = lax.rsqrt(jnp.mean(yg * yg, axis=1, keepdims=True) + RMS_EPS)
        nrm = yg * r
        dnn = dn * g
        dyg = r * (dnn - nrm * jnp.mean(dnn * nrm, axis=1, keepdims=True))
        return [dyg * sz, dyg * yv * _dsilu(zv)], [_csum(dn * nrm)]
    dy_ssd, dproj, dnorm_g = _rowmap(gnorm_bwd_fn, [(y_ssd, 0, d_inner), (proj, 0, d_inner), (dyn, 0, d_inner)],
                                     [norm_g], [(d_inner, F32), (in_pad, BF16, 0, gw)], [(1, d_inner)],
                                     name="gnorm_bwd", cw=gw, tr=1024)
    dxs, dB, dC, da_t, dar_t, s1_t, dD_f, r_kv = _ssd_bwd(
        *ssd_in, states, dy_ssd, d_inner=d_inner, name="ssd_bwd", comm=_rs_cols(dw_kv))
    dproj, dconv_w, dconv_b, r_aout = _conv_bwd(
        proj, d_inner, conv_dim, conv_w, conv_b, (dxs, dB, dC), dproj, name="conv_bwd", tc=128,
        comm=_a2a_comm(dw_aout.reshape(N_DEV, d_inner // N_DEV, D)))
    tokp = lambda t: padh(jnp.transpose(t, (1, 0, 2)).reshape(L, H))
    dar_tok = padh(jnp.transpose(dar_t, (2, 0, 1)).reshape(L, H))
    dproj, ddt_bias_p, dA_log_p = _ssd_post(tokp(da_t), dar_tok, tokp(s1_t), dt_p, dt_raw, bias_p, alog_p,
                                            dproj, d_inner + conv_dim, name="ssd_post")
    ddt_bias, dA_log = ddt_bias_p[:, :H], dA_log_p[:, :H]
    dD = jnp.sum(dD_f.reshape(H, P), axis=1)[None]
    dw_in, r_bin = _matmul(h0, dproj, name="mm_a_in_dw", ta=True, out_dtype=BF16, tn=1152, comm=_rs_cols(dw_bin))
    dw_in = dw_in[:, :in_dim]
    cs_in = in_dim // N_DEV
    by_c = jnp.transpose(dw_in.reshape(D, N_DEV // 2, 2, cs_in), (2, 1, 0, 3))
    my_c = lax.axis_index("c")
    keep = lax.dynamic_index_in_dim(by_c, my_c, 0, False)
    give = lax.dynamic_index_in_dim(by_c, 1 - my_c, 0, False)
    got = _run_comm(_pair_comm(give), "rs_a_in_pair")[0]
    pair_sum = _rowmap(lambda a, b: ([a.astype(F32) + b.astype(F32)], []),
                       [(keep.reshape(-1, cs_in), 0, cs_in), (got.reshape(-1, cs_in), 0, cs_in)], [],
                       [(cs_in, BF16)], [], name="rs_a_in_add", tr=512)[0].reshape(N_DEV // 2, D, cs_in)
    dh0, r_in = _matmul(dproj, w_in, name="mm_a_in_dx", tb=True, tm=2048, tk=1152,
                        comm=_quad_comm(pair_sum))
    grad_x, dscale0, dshift0 = mod_bwd(x2d, dh0, dxa, 0, "mod0_bwd")

    dmod = jnp.concatenate([dshift0, dscale0, dgate0, dshift1, dscale1, dgate1], axis=1)
    pieces = [dmod, dlng0, dlng1, dlnb0, dlnb1, ddt_bias, dA_log, dD,
              dconv_w.reshape(1, -1), dconv_b, dnorm_g]
    sizes = [p.shape[1] for p in pieces]
    tot = sum(sizes)
    tot_pad = -(-tot // (8 * LANE)) * (8 * LANE)
    packed = jnp.pad(jnp.concatenate(pieces, axis=1), ((0, 0), (0, tot_pad - tot))).reshape(tot_pad // LANE, LANE)
    packed_all = _all_gather(packed, "ag_small")
    small = _sum_parts(packed_all, name="sum_small").reshape(tot_pad)
    offs = np.cumsum([0] + sizes)
    seg = lambda i: small[int(offs[i]):int(offs[i + 1])]
    g_ada_b = seg(0).reshape(DEPTH, 3 * D)
    g_ln_g = jnp.stack([seg(1), seg(2)])
    g_ln_b = jnp.stack([seg(3), seg(4)])
    g_dt_bias, g_A_log, g_D = seg(5)[None], seg(6)[None], seg(7)[None]
    cs = conv_dim // N_DEV
    g_conv_w = lax.dynamic_slice(seg(8).reshape(SSD_CONV_W, conv_dim), (0, me * cs), (SSD_CONV_W, cs))[None]
    g_conv_b = lax.dynamic_slice(seg(9), (me * cs,), (cs,))[None]
    ns = d_inner // N_DEV
    g_norm_g = lax.dynamic_slice(seg(10), (me * ns,), (ns,))[None]

    ms = 3 * D // N_DEV
    dmod_all = packed_all.reshape(N_DEV, tot_pad)[:, :DEPTH * 3 * D].reshape(N_DEV, DEPTH, 3 * D)
    dmod_cols = lax.dynamic_slice(dmod_all, (0, 0, me * ms), (N_DEV, DEPTH, ms))
    c_t = jnp.transpose(c_all)
    g_ada_w = jnp.stack([_matmul(c_t, dmod_cols[:, l], name=f"mm_ada_dw{l}", exact=True, a_silu=True)
                         for l in range(DEPTH)])[None]

    def upd(parts, w, m, v, name):
        shp = w.shape
        r2 = lambda a: a.reshape(-1, shp[-1])
        return [o.reshape(shp) for o in _adamw(parts, r2(w), r2(m), r2(v), name=name)]

    res = {}
    res["ada_w"] = upd(g_ada_w.reshape(1, -1, ms), ada_w, m_ada_w, v_ada_w, "adam_ada_w")
    res["a_in_w"] = upd(r_in, a_in_w, m_a_in_w, v_a_in_w, "adam_a_in")
    res["a_out_w"] = upd(r_aout, a_out_w, m_a_out_w, v_a_out_w, "adam_a_out")
    res["kv_w"] = upd(r_kv, kv_w, m_kv_w, v_kv_w, "adam_kv")
    res["b_in_w"] = upd(r_bin, b_in_w, m_b_in_w, v_b_in_w, "adam_b_in")
    res["b_out_w"] = upd(r_bout, b_out_w, m_b_out_w, v_b_out_w, "adam_b_out")

    small_names = ["ada_b", "ln_g", "ln_b", "a_conv_w", "a_conv_b", "a_dt_bias", "a_A_log", "a_D", "a_norm_g"]
    small_g = [g_ada_b, g_ln_g, g_ln_b, g_conv_w, g_conv_b, g_dt_bias, g_A_log, g_D, g_norm_g]
    small_w = [ada_b, ln_g, ln_b, a_conv_w, a_conv_b, a_dt_bias, a_A_log, a_D, a_norm_g]
    small_m = [m_ada_b, m_ln_g, m_ln_b, m_a_conv_w, m_a_conv_b, m_a_dt_bias, m_a_A_log, m_a_D, m_a_norm_g]
    small_v = [v_ada_b, v_ln_g, v_ln_b, v_a_conv_w, v_a_conv_b, v_a_dt_bias, v_a_A_log, v_a_D, v_a_norm_g]
    ssz = [int(np.prod(w.shape)) for w in small_w]
    stot = sum(ssz)
    spad = -(-stot // (8 * LANE)) * (8 * LANE)

    def pack(arrs, fill):
        flat = jnp.concatenate([a.reshape(-1) for a in arrs])
        return jnp.concatenate([flat, jnp.full((spad - stot,), fill, F32)]).reshape(spad // LANE, LANE)

    sres = _adamw(pack(small_g, 0.0)[None], pack(small_w, 0.0), pack(small_m, 0.0), pack(small_v, 1.0), name="adam_small")
    soffs = np.cumsum([0] + ssz)
    for i, nme in enumerate(small_names):
        res[nme] = [r.reshape(-1)[int(soffs[i]):int(soffs[i + 1])].reshape(small_w[i].shape) for r in sres]

    order = ["ada_w", "ada_b", "ln_g", "ln_b", "a_in_w", "a_conv_w", "a_conv_b", "a_dt_bias", "a_A_log", "a_D",
             "a_norm_g", "a_out_w", "kv_w", "b_in_w", "b_out_w"]
    outs = [loss, grad_x[None]]
    for j in range(4):
        outs += [res[nme][j] for nme in order]
    return tuple(outs)
```

```python
import functools
import math

import numpy as np
import jax
import jax.numpy as jnp
from jax import lax
from jax.experimental import pallas as pl
from jax.experimental.pallas import tpu as pltpu

F32, BF16 = jnp.float32, jnp.bfloat16
HI = lax.Precision.HIGHEST
MESH = pl.DeviceIdType.MESH
N_DEV = 8

SSD_HEAD_DIM = 64
SSD_N_GROUPS = 8
SSD_D_STATE = 128
SSD_CONV_W = 4
SSD_CHUNK = 256
DIL_PATTERNS = ((128, 1), (512, 4), (2048, 16))
DIL_N_GROUPS = 3
DIL_HEADS = 8
DIL_HEAD_DIM = 128
DIL_BLOCK = 128
DIL_W = DIL_HEADS * DIL_HEAD_DIM
DEPTH = 2
DEEPNORM_ALPHA = (2 * DEPTH) ** 0.25
LN_EPS = 1e-5
RMS_EPS = 1e-5
ADAM_LR, ADAM_B1, ADAM_B2, ADAM_EPS, ADAM_WD, ADAM_STEP = 0.001, 0.9, 0.999, 1e-08, 0.01, 10
LANE = 128
NEG = -1e30
VMEM_LIMIT = 56 * 1024 * 1024


def _cp(sem=None):
    return pltpu.CompilerParams(dimension_semantics=sem, vmem_limit_bytes=VMEM_LIMIT)


def _silu(x):
    return x * jax.nn.sigmoid(x)


def _dsilu(x):
    s = jax.nn.sigmoid(x)
    return s * (1.0 + x * (1.0 - s))


def _softplus(x):
    return jnp.maximum(x, 0.0) + jnp.log(1.0 + jnp.exp(-jnp.abs(x)))


def _nt(a, b):
    return lax.dot_general(a, b, (((1,), (1,)), ((), ())), preferred_element_type=F32)


def _nn(a, b):
    return jnp.dot(a, b, preferred_element_type=F32)


def _hi(a, b):
    return jnp.dot(a, b, preferred_element_type=F32, precision=HI)


def _dot2(a, b01):
    hi = a.astype(BF16)
    lo = (a - hi.astype(F32)).astype(BF16)
    bb = b01.astype(BF16)
    return _nn(hi, bb) + _nn(lo, bb)


def _pick(n, pref, align=LANE):
    if n <= pref:
        return n
    for t in range(pref - pref % align, 0, -align):
        if n % t == 0:
            return t
    return n


class _Comm:
    def __init__(self, ins, outs, sems, start, finish):
        self.ins, self.outs, self.sems, self.start, self.finish = ins, outs, sems, start, finish


def _comm_join(comms):
    ins = [a for c in comms for a in c.ins]
    outs = [a for c in comms for a in c.outs]
    sems = [a for c in comms for a in c.sems]

    def split(refs, attr):
        res, i = [], 0
        for c in comms:
            n = len(getattr(c, attr))
            res.append(refs[i:i + n])
            i += n
        return res

    def start(cin, cout, csem):
        for c, a, b, d in zip(comms, split(cin, "ins"), split(cout, "outs"), split(csem, "sems")):
            c.start(a, b, d)

    def finish(cin, cout, csem):
        for c, a, b, d in zip(comms, split(cin, "ins"), split(cout, "outs"), split(csem, "sems")):
            c.finish(a, b, d)

    return _Comm(ins, outs, sems, start, finish)


def _ag_comm(v, cols=False):
    if cols:
        R, Cs = v.shape
        assert Cs % LANE == 0
        out_sd = jax.ShapeDtypeStruct((R, N_DEV * Cs), v.dtype)
    else:
        out_sd = jax.ShapeDtypeStruct((N_DEV,) + v.shape, v.dtype)

    def parts(x_ref, out_ref, send_sems, recv_sems, local_sem):
        x, y, c = lax.axis_index("x"), lax.axis_index("y"), lax.axis_index("c")
        me, sibling = (x, y, c), (x, y, 1 - c)
        chips = [(1 - x, y), (x, 1 - y), (1 - x, 1 - y)]

        def slab(px, py, pc):
            k = 4 * px + 2 * py + pc
            if cols:
                return out_ref.at[:, pl.ds(pl.multiple_of(k * Cs, LANE), Cs)]
            return out_ref.at[k]

        def copy(k, block, to, src=None):
            return pltpu.make_async_remote_copy(
                src_ref=slab(*block) if src is None else src, dst_ref=slab(*block),
                send_sem=send_sems.at[k], recv_sem=recv_sems.at[k], device_id=to, device_id_type=MESH)

        mine = pltpu.make_async_copy(x_ref, slab(*me), local_sem)
        first = [copy(0, me, sibling, src=x_ref)]
        first += [copy(1 + j, me, (*chip, c), src=x_ref) for j, chip in enumerate(chips)]
        passed = [copy(4 + j, (*chip, c), sibling) for j, chip in enumerate(chips)]
        return me, sibling, chips, c, copy, mine, first, passed

    def start(cin, cout, csem):
        _, _, _, _, _, mine, first, _ = parts(cin[0], cout[0], *csem)
        mine.start()
        for cp in first:
            cp.start()

    def finish(cin, cout, csem):
        me, sibling, chips, c, copy, mine, first, passed = parts(cin[0], cout[0], *csem)
        for j, chip in enumerate(chips):
            copy(1 + j, (*chip, c), me).wait_recv()
            passed[j].start()
        copy(0, sibling, me).wait_recv()
        for j, chip in enumerate(chips):
            copy(4 + j, (*chip, 1 - c), me).wait_recv()
        for cp in first + passed:
            cp.wait_send()
        mine.wait()

    return _Comm([v], [out_sd],
                 [pltpu.SemaphoreType.DMA((7,)), pltpu.SemaphoreType.DMA((7,)), pltpu.SemaphoreType.DMA], start, finish)


def _a2a_comm(v, cols=False):
    if cols:
        R, C = v.shape
        Cs = C // N_DEV
        assert Cs % LANE == 0
        out_sd = jax.ShapeDtypeStruct((N_DEV, R, Cs), v.dtype)
    else:
        out_sd = jax.ShapeDtypeStruct(v.shape, v.dtype)

    def parts(x_ref, out_ref, send_sems, recv_sems, local_sem):
        x, y, c = lax.axis_index("x"), lax.axis_index("y"), lax.axis_index("c")
        me = 4 * x + 2 * y + c

        def src(k):
            if cols:
                return x_ref.at[:, pl.ds(pl.multiple_of(k * Cs, LANE), Cs)]
            return x_ref.at[k]

        mine = pltpu.make_async_copy(src(me), out_ref.at[me], local_sem)
        sends, recvs = [], []
        for k, mask in enumerate(range(1, N_DEV)):
            px = 1 - x if (mask >> 2) & 1 else x
            py = 1 - y if (mask >> 1) & 1 else y
            pc = 1 - c if mask & 1 else c
            peer = 4 * px + 2 * py + pc
            sends.append(pltpu.make_async_remote_copy(
                src_ref=src(peer), dst_ref=out_ref.at[me],
                send_sem=send_sems.at[k], recv_sem=recv_sems.at[k], device_id=(px, py, pc), device_id_type=MESH))
            recvs.append(pltpu.make_async_remote_copy(
                src_ref=src(me), dst_ref=out_ref.at[peer],
                send_sem=send_sems.at[k], recv_sem=recv_sems.at[k], device_id=(px, py, pc), device_id_type=MESH))
        return mine, sends, recvs

    def start(cin, cout, csem):
        mine, sends, _ = parts(cin[0], cout[0], *csem)
        mine.start()
        for cp in sends:
            cp.start()

    def finish(cin, cout, csem):
        mine, sends, recvs = parts(cin[0], cout[0], *csem)
        for cp in recvs:
            cp.wait_recv()
        for cp in sends:
            cp.wait_send()
        mine.wait()

    return _Comm([v], [out_sd],
                 [pltpu.SemaphoreType.DMA((7,)), pltpu.SemaphoreType.DMA((7,)), pltpu.SemaphoreType.DMA], start, finish)


def _pair_comm(v4):
    def copy(x_ref, out_ref, send_sem, recv_sem):
        x, y, c = lax.axis_index("x"), lax.axis_index("y"), lax.axis_index("c")
        return pltpu.make_async_remote_copy(src_ref=x_ref, dst_ref=out_ref, send_sem=send_sem, recv_sem=recv_sem,
                                            device_id=(x, y, 1 - c), device_id_type=MESH)

    def start(cin, cout, csem):
        copy(cin[0], cout[0], *csem).start()

    def finish(cin, cout, csem):
        copy(cin[0], cout[0], *csem).wait()

    return _Comm([v4], [jax.ShapeDtypeStruct(v4.shape, v4.dtype)],
                 [pltpu.SemaphoreType.DMA, pltpu.SemaphoreType.DMA], start, finish)


def _quad_comm(v4):
    def parts(x_ref, out_ref, send_sems, recv_sems, local_sem):
        x, y, c = lax.axis_index("x"), lax.axis_index("y"), lax.axis_index("c")
        me = 2 * x + y
        mine = pltpu.make_async_copy(x_ref.at[me], out_ref.at[me], local_sem)
        sends, recvs = [], []
        for k, mask in enumerate(range(1, 4)):
            px = 1 - x if (mask >> 1) & 1 else x
            py = 1 - y if mask & 1 else y
            peer = 2 * px + py
            sends.append(pltpu.make_async_remote_copy(
                src_ref=x_ref.at[peer], dst_ref=out_ref.at[me],
                send_sem=send_sems.at[k], recv_sem=recv_sems.at[k], device_id=(px, py, c), device_id_type=MESH))
            recvs.append(pltpu.make_async_remote_copy(
                src_ref=x_ref.at[me], dst_ref=out_ref.at[peer],
                send_sem=send_sems.at[k], recv_sem=recv_sems.at[k], device_id=(px, py, c), device_id_type=MESH))
        return mine, sends, recvs

    def start(cin, cout, csem):
        mine, sends, _ = parts(cin[0], cout[0], *csem)
        mine.start()
        for cp in sends:
            cp.start()

    def finish(cin, cout, csem):
        mine, sends, recvs = parts(cin[0], cout[0], *csem)
        for cp in recvs:
            cp.wait_recv()
        for cp in sends:
            cp.wait_send()
        mine.wait()

    return _Comm([v4], [jax.ShapeDtypeStruct(v4.shape, v4.dtype)],
                 [pltpu.SemaphoreType.DMA((3,)), pltpu.SemaphoreType.DMA((3,)), pltpu.SemaphoreType.DMA], start, finish)


def _run_comm(comm, name):
    nci, nco = len(comm.ins), len(comm.outs)

    def body(*refs):
        comm.start(refs[:nci], refs[nci:nci + nco], refs[nci + nco:])
        comm.finish(refs[:nci], refs[nci:nci + nco], refs[nci + nco:])

    anyspec = pl.BlockSpec(memory_space=pl.ANY)
    return pl.pallas_call(body, name=name, out_shape=list(comm.outs), in_specs=[anyspec] * nci,
                          out_specs=[anyspec] * nco, scratch_shapes=list(comm.sems))(*comm.ins)


def _all_gather(v, name):
    return _run_comm(_ag_comm(v), name)[0]


def _pcall(body, args, *, name, grid, in_specs, out_specs, out_shape, scratch=(), sem=None, comm=None, aliases=None):
    out_shape, out_specs = list(out_shape), list(out_specs)
    aliases = dict(aliases or {})
    if comm is None:
        return pl.pallas_call(body, name=name, grid=grid, in_specs=list(in_specs), out_specs=out_specs,
                              out_shape=out_shape, scratch_shapes=list(scratch), input_output_aliases=aliases,
                              compiler_params=_cp(sem))(*args)
    ni, no, ns = len(args), len(out_shape), len(scratch)
    nci, nco = len(comm.ins), len(comm.outs)

    def wrapped(*refs):
        ins, cin = refs[:ni], refs[ni:ni + nci]
        o0 = ni + nci
        outs, cout = refs[o0:o0 + no], refs[o0 + no:o0 + no + nco]
        s0 = o0 + no + nco
        scr, csem = refs[s0:s0 + ns], refs[s0 + ns:]
        first = functools.reduce(jnp.logical_and, [pl.program_id(a) == 0 for a in range(len(grid))])
        last = functools.reduce(jnp.logical_and, [pl.program_id(a) == g - 1 for a, g in enumerate(grid)])

        @pl.when(first)
        def _():
            comm.start(cin, cout, csem)

        body(*ins, *outs, *scr)

        @pl.when(last)
        def _():
            comm.finish(cin, cout, csem)

    anyspec = pl.BlockSpec(memory_space=pl.ANY)
    res = pl.pallas_call(
        wrapped, name=name, grid=grid, in_specs=list(in_specs) + [anyspec] * nci,
        out_specs=out_specs + [anyspec] * nco, out_shape=out_shape + list(comm.outs),
        scratch_shapes=list(scratch) + list(comm.sems), input_output_aliases=aliases,
        compiler_params=_cp(("arbitrary",) * len(grid)))(*args, *comm.ins)
    return list(res[:no]) + list(res[no:])


def _matmul(a, b, *, name, ta=False, tb=False, out_dtype=F32, tm=1024, tn=1024, tk=2048,
            exact=False, a_silu=False, bias=None, comm=None, b_koff=0, out_cols=None, out_buf=None):
    (K, M) = a.shape if ta else a.shape[::-1]
    (N, K2) = b.shape if tb else b.shape[::-1]
    assert K == K2 or (tb and K + b_koff <= K2), (a.shape, b.shape, ta, tb)
    tm, tn, tk = _pick(M, tm), _pick(N, tn), _pick(K, tk)
    nk = K // tk
    assert b_koff % tk == 0
    ko = b_koff // tk
    a_spec = pl.BlockSpec((tk, tm), lambda i, j, k: (k, i)) if ta else pl.BlockSpec((tm, tk), lambda i, j, k: (i, k))
    b_spec = pl.BlockSpec((tn, tk), lambda i, j, k: (j, k + ko)) if tb else pl.BlockSpec((tk, tn), lambda i, j, k: (k, j))
    dims = (((0,) if ta else (1,), (1,) if tb else (0,)), ((), ()))
    in_specs, args = [a_spec, b_spec], [a, b]
    if bias is not None:
        if bias.shape[0] == 1:
            in_specs.append(pl.BlockSpec((1, tn), lambda i, j, k: (0, j)))
        else:
            in_specs.append(pl.BlockSpec((tm, tn), lambda i, j, k: (i, j)))
        args.append(bias)

    aliases = {}
    if out_buf is not None:
        in_specs.append(pl.BlockSpec(memory_space=pl.ANY))
        args.append(out_buf)
        aliases = {len(args) - 1: 0}
    n_in = len(args)
    width, off = out_cols if out_cols is not None else (N, 0)
    assert off % tn == 0

    def body(*refs):
        a_ref, b_ref = refs[0], refs[1]
        bias_ref = refs[2] if bias is not None else None
        o_ref = refs[n_in]
        av, bv = a_ref[...], b_ref[...]
        if a_silu:
            av = _silu(av.astype(F32))
        if exact:
            p = lax.dot_general(av.astype(F32), bv.astype(F32), dims, preferred_element_type=F32, precision=HI)
        else:
            p = lax.dot_general(av.astype(BF16), bv.astype(BF16), dims, preferred_element_type=F32)

        def fin(r):
            if bias_ref is not None:
                r = r + bias_ref[...]
            o_ref[...] = r.astype(o_ref.dtype)

        if nk == 1:
            fin(p)
        else:
            acc = refs[-1]
            k = pl.program_id(2)

            @pl.when(k == 0)
            def _():
                acc[...] = p

            @pl.when(k > 0)
            def _():
                acc[...] += p

            @pl.when(k == nk - 1)
            def _():
                fin(acc[...])

    res = _pcall(
        body, args, name=name,
        out_shape=[jax.ShapeDtypeStruct((M, width), out_dtype)],
        grid=(M // tm, N // tn, nk),
        in_specs=in_specs,
        out_specs=[pl.BlockSpec((tm, tn), lambda i, j, k: (i, j + off // tn))],
        scratch=[pltpu.VMEM((tm, tn), F32)] if nk > 1 else [],
        sem=("parallel", "parallel", "arbitrary"), comm=comm, aliases=aliases)
    return res[0] if comm is None else res


def _rowmap(fn, rows, bcasts, outs, accs, *, name, tr=256, cw=None, comm=None):
    L = rows[0][0].shape[0]
    tr = _pick(L, tr)
    nr, nb, no, na = len(rows), len(bcasts), len(outs), len(accs)
    if cw is None:
        ncol = 1
        widths = [w for (_, _, w) in rows]
    else:
        wtot = rows[0][2]
        ncol = wtot // cw
        widths = [cw] * nr
    in_specs, args = [], []
    for (arr, off, w), bw in zip(rows, widths):
        assert off % bw == 0
        in_specs.append(pl.BlockSpec((tr, bw), functools.partial(lambda j, i, o: (i, o + j), o=off // bw)))
        args.append(arr)
    for arr in bcasts:
        bw = arr.shape[1] if cw is None else cw
        in_specs.append(pl.BlockSpec((arr.shape[0], bw), lambda j, i: (0, j)))
        args.append(arr)
    out_shape, out_specs = [], []
    for spec in outs:
        if len(spec) == 2:
            (w, dt), off = spec, 0
            bw = w if cw is None else cw
        else:
            w, dt, off, bw = spec
        out_shape.append(jax.ShapeDtypeStruct((L, w), dt))
        out_specs.append(pl.BlockSpec((tr, bw), functools.partial(lambda j, i, o: (i, o + j), o=off // bw)))
    for (r, w) in accs:
        bw = w if cw is None else cw
        out_shape.append(jax.ShapeDtypeStruct((r, w), F32))
        out_specs.append(pl.BlockSpec((r, bw), lambda j, i: (0, j)))

    def body(*refs):
        ins = [r[...] for r in refs[:nr + nb]]
        o_refs = refs[nr + nb:nr + nb + no]
        a_refs = refs[nr + nb + no:]
        o, a = fn(*ins)
        for ref, val in zip(o_refs, o):
            ref[...] = val.astype(ref.dtype)
        if na:
            @pl.when(pl.program_id(1) == 0)
            def _():
                for ref in a_refs:
                    ref[...] = jnp.zeros_like(ref)

            for ref, val in zip(a_refs, a):
                ref[...] += val

    return _pcall(body, args, name=name, out_shape=out_shape, grid=(ncol, L // tr), in_specs=in_specs,
                  out_specs=out_specs, sem=("parallel", "arbitrary"), comm=comm)


def _csum(v):
    return jnp.sum(v, axis=0, keepdims=True)


def _shift_rows(v, s, rows):
    if s == 0:
        return v
    n = v.shape[0]
    r = pltpu.roll(v, s % n, 0)
    if s > 0:
        return jnp.where(rows >= s, r, 0.0)
    return jnp.where(rows < n + s, r, 0.0)


def _conv_fwd(proj, off, width, w, b, *, name, tc=256):
    L = proj.shape[0]
    tc = _pick(width, tc)

    R, HALO = 64, 8
    assert L % R == 0 and SSD_CONV_W - 1 <= HALO

    def body(x_ref, w_ref, b_ref, o_ref):
        wv, bv = w_ref[...], b_ref[...]

        def step(c, carry):
            t0 = pl.multiple_of(c * R, R)
            cur = x_ref[pl.ds(t0, R), :]
            p0 = pl.multiple_of(jnp.maximum(t0 - HALO, 0), HALO)
            prev = jnp.where(c > 0, x_ref[pl.ds(p0, HALO), :], 0.0)
            ext = jnp.concatenate([prev, cur], axis=0)
            acc = bv + wv[SSD_CONV_W - 1:SSD_CONV_W, :] * cur
            for k in range(SSD_CONV_W - 1):
                acc = acc + wv[k:k + 1, :] * pltpu.roll(ext, SSD_CONV_W - 1 - k, 0)[HALO:]
            o_ref[pl.ds(t0, R), :] = _silu(acc)
            return carry

        lax.fori_loop(0, L // R, step, 0)

    return pl.pallas_call(
        body, name=name, out_shape=jax.ShapeDtypeStruct((L, width), F32), grid=(width // tc,),
        in_specs=[pl.BlockSpec((L, tc), functools.partial(lambda j, o: (0, o + j), o=off // tc)),
                  pl.BlockSpec((SSD_CONV_W, tc), lambda j: (0, j)), pl.BlockSpec((1, tc), lambda j: (0, j))],
        out_specs=pl.BlockSpec((L, tc), lambda j: (0, j)),
        compiler_params=_cp(("parallel",)),
    )(proj, w, b)


def _conv_bwd(proj, off, width, w, b, dys, dproj, *, name, tc=256, comm=None):
    L = proj.shape[0]
    tc = _pick(width, tc)
    ntile = [d.shape[1] // tc for d in dys]
    assert all(d.shape[1] % tc == 0 for d in dys) and sum(ntile) == width // tc
    first = [sum(ntile[:i]) for i in range(len(dys))]

    def body(x_ref, w_ref, b_ref, *rest):
        dy_refs, (dx_ref, dw_ref, db_ref) = rest[:len(dys)], rest[len(dys) + 1:]
        j = pl.program_id(0)
        dy = dy_refs[0][...]
        for i in range(1, len(dys)):
            dy = jnp.where(j >= first[i], dy_refs[i][...], dy)
        x = x_ref[...]
        rows = lax.broadcasted_iota(jnp.int32, x.shape, 0)
        xs = [_shift_rows(x, SSD_CONV_W - 1 - k, rows) for k in range(SSD_CONV_W)]
        pre = jnp.zeros_like(x) + b_ref[...]
        for k in range(SSD_CONV_W):
            pre = pre + w_ref[k:k + 1, :] * xs[k]
        dpre = dy * _dsilu(pre)
        dx = jnp.zeros_like(x)
        for k in range(SSD_CONV_W):
            dx = dx + w_ref[k:k + 1, :] * _shift_rows(dpre, -(SSD_CONV_W - 1 - k), rows)
            dw_ref[k:k + 1, :] = _csum(dpre * xs[k])
        dx_ref[...] = dx.astype(dx_ref.dtype)
        db_ref[...] = _csum(dpre)

    dy_specs = [pl.BlockSpec((L, tc), functools.partial(lambda j, f, n: (0, jnp.clip(j - f, 0, n - 1)), f=f, n=n))
                for f, n in zip(first, ntile)]
    shifted = pl.BlockSpec((L, tc), functools.partial(lambda j, o: (0, o + j), o=off // tc))
    return _pcall(
        body, (proj, w, b, *dys, dproj), name=name,
        out_shape=[jax.ShapeDtypeStruct(dproj.shape, BF16), jax.ShapeDtypeStruct((SSD_CONV_W, width), F32),
                   jax.ShapeDtypeStruct((1, width), F32)],
        grid=(width // tc,),
        in_specs=[shifted, pl.BlockSpec((SSD_CONV_W, tc), lambda j: (0, j)), pl.BlockSpec((1, tc), lambda j: (0, j))]
        + dy_specs + [pl.BlockSpec(memory_space=pl.ANY)],
        out_specs=[shifted, pl.BlockSpec((SSD_CONV_W, tc), lambda j: (0, j)), pl.BlockSpec((1, tc), lambda j: (0, j))],
        sem=("parallel",), comm=comm, aliases={3 + len(dys): 0})


def _tri(Q):
    ri = lax.broadcasted_iota(jnp.int32, (Q, Q), 0)
    ci = lax.broadcasted_iota(jnp.int32, (Q, Q), 1)
    return ri >= ci, ri <= ci


def _ssd_prep(dt_raw, bias, alog, *, name):
    L, W = dt_raw.shape
    Q = SSD_CHUNK

    def body(r_ref, b_ref, al_ref, dt_ref, a_ref):
        lower, _ = _tri(Q)
        dt = _softplus(r_ref[...] + b_ref[...])
        dt_ref[...] = dt
        a_ref[...] = _hi(lower.astype(F32), dt * (-jnp.exp(al_ref[...])))

    blk = pl.BlockSpec((Q, W), lambda c: (c, 0))
    one = pl.BlockSpec((1, W), lambda c: (0, 0))
    sd = jax.ShapeDtypeStruct((L, W), F32)
    return _pcall(body, (dt_raw, bias, alog), name=name, out_shape=[sd, sd], grid=(L // Q,),
                  in_specs=[blk, one, one], out_specs=[blk, blk], sem=("parallel",))


def _ssd_post(da, dar, s1, dt, dt_raw, bias, alog, dproj, col_off, *, name):
    L, W = da.shape
    Q = SSD_CHUNK

    def body(da_ref, dar_ref, s1_ref, dt_ref, r_ref, b_ref, al_ref, buf_ref, o_ref, db_ref, dal_ref):
        _, upper = _tri(Q)
        A = -jnp.exp(al_ref[...])
        ddtA = _hi(upper.astype(F32), da_ref[...] - dar_ref[...])
        ddt_raw = (ddtA * A + s1_ref[...]) * jax.nn.sigmoid(r_ref[...] + b_ref[...])
        o_ref[...] = ddt_raw.astype(o_ref.dtype)

        @pl.when(pl.program_id(0) == 0)
        def _():
            db_ref[...] = jnp.zeros_like(db_ref)
            dal_ref[...] = jnp.zeros_like(dal_ref)

        db_ref[...] += _csum(ddt_raw)
        dal_ref[...] += _csum(ddtA * dt_ref[...]) * A

    blk = pl.BlockSpec((Q, W), lambda c: (c, 0))
    one = pl.BlockSpec((1, W), lambda c: (0, 0))
    return _pcall(body, (da, dar, s1, dt, dt_raw, bias, alog, dproj), name=name,
                  out_shape=[jax.ShapeDtypeStruct(dproj.shape, dproj.dtype), jax.ShapeDtypeStruct((1, W), F32),
                             jax.ShapeDtypeStruct((1, W), F32)],
                  grid=(L // Q,),
                  in_specs=[blk, blk, blk, blk, blk, one, one, pl.BlockSpec(memory_space=pl.ANY)],
                  out_specs=[pl.BlockSpec((Q, W), lambda c: (c, col_off // W)), one, one], aliases={7: 0},
                  sem=("arbitrary",))


def _head_sum(v, K, KP):
    P = KP // K
    t_r = lax.broadcasted_iota(jnp.int32, (KP, K), 0)
    t_c = lax.broadcasted_iota(jnp.int32, (KP, K), 1)
    Et = ((t_r >= t_c * P) & (t_r < (t_c + 1) * P)).astype(BF16)
    hi = v.astype(BF16)
    lo = (v - hi.astype(F32)).astype(BF16)
    return _nn(hi, Et) + _nn(lo, Et)


def _half_masks():
    li = lax.broadcasted_iota(jnp.int32, (1, LANE), 1)
    return [(li < SSD_HEAD_DIM).astype(F32), (li >= SSD_HEAD_DIM).astype(F32)]


def _ssd_specs(K, KP, d_inner, rev, nc):
    Q, N, G = SSD_CHUNK, SSD_D_STATE, SSD_N_GROUPS
    cidx = (lambda c: nc - 1 - c) if rev else (lambda c: c)
    b_off, c_off = d_inner // N, d_inner // N + G
    return [
        pl.BlockSpec((Q, KP), lambda g, c: (cidx(c), g)),
        pl.BlockSpec((Q, N), lambda g, c: (cidx(c), b_off + g)),
        pl.BlockSpec((Q, N), lambda g, c: (cidx(c), c_off + g)),
        pl.BlockSpec((None, Q, K), lambda g, c: (g, cidx(c), 0)),
        pl.BlockSpec((None, Q, K), lambda g, c: (g, cidx(c), 0)),
        pl.BlockSpec((None, K, Q), lambda g, c: (g, 0, cidx(c))),
        pl.BlockSpec((1, KP), lambda g, c: (0, g)),
    ]


def _expand_heads(vc, K):
    Q = vc.shape[0]
    left = lax.broadcasted_iota(jnp.int32, (Q, LANE), 1) < SSD_HEAD_DIM
    parts = []
    for pr in range(K // 2):
        a = jnp.broadcast_to(vc[:, 2 * pr:2 * pr + 1], (Q, LANE))
        b = jnp.broadcast_to(vc[:, 2 * pr + 1:2 * pr + 2], (Q, LANE))
        parts.append(jnp.where(left, a, b))
    return jnp.concatenate(parts, axis=1) if len(parts) > 1 else parts[0]


def _ssd_fwd(xbc, dt_c, a_c, a_r, d_full, *, d_inner, name, comm=None):
    L = xbc.shape[0]
    G, N, Q, P = SSD_N_GROUPS, SSD_D_STATE, SSD_CHUNK, SSD_HEAD_DIM
    KP = d_inner // G
    K = KP // P
    nc = L // Q
    npair = KP // LANE

    def body(xs_ref, b_ref, c_ref, dtc_ref, ac_ref, ar_ref, df_ref, y_ref, st_ref, S):
        @pl.when(pl.program_id(1) == 0)
        def _():
            S[...] = jnp.zeros_like(S)

        lower, _ = _tri(Q)
        st_ref[...] = S[...]
        xs = xs_ref[...]
        Bm, Cm = b_ref[...], c_ref[...]
        Bb, Cb = Bm.astype(BF16), Cm.astype(BF16)
        a_c, a_r = ac_ref[...], ar_ref[...]
        a_f = _expand_heads(a_c, K)
        X = xs * _expand_heads(dtc_ref[...], K)
        ea = jnp.exp(a_f)
        alast = a_f[Q - 1:Q, :]
        tail = jnp.exp(alast - a_f)
        cb = _nt(Cb, Bb)
        Sv = S[...]
        yoff = _nn(Cb, Sv.astype(BF16)) * ea
        skip = xs * df_ref[...]
        masks = _half_masks()
        for pr in range(npair):
            Xp = X[:, pr * LANE:(pr + 1) * LANE]
            acc = yoff[:, pr * LANE:(pr + 1) * LANE] + skip[:, pr * LANE:(pr + 1) * LANE]
            for hh in range(2):
                k = 2 * pr + hh
                seg = a_c[:, k:k + 1] - a_r[k:k + 1, :]
                dec = jnp.where(lower, jnp.exp(jnp.minimum(seg, 0.0)), 0.0)
                acc = acc + _nn((cb * dec).astype(BF16), (Xp * masks[hh]).astype(BF16))
            y_ref[:, pr * LANE:(pr + 1) * LANE] = acc
        Bt = Bm.T
        S[...] = Sv * jnp.exp(alast) + _nn(Bt.astype(BF16), (X * tail).astype(BF16))

    return _pcall(
        body, (xbc, xbc, xbc, dt_c, a_c, a_r, d_full), name=name,
        out_shape=[jax.ShapeDtypeStruct((L, d_inner), F32), jax.ShapeDtypeStruct((G, nc, N, KP), F32)],
        grid=(G, nc),
        in_specs=_ssd_specs(K, KP, d_inner, False, nc),
        out_specs=[pl.BlockSpec((Q, KP), lambda g, c: (c, g)), pl.BlockSpec((None, None, N, KP), lambda g, c: (g, c, 0, 0))],
        scratch=[pltpu.VMEM((N, KP), F32)],
        sem=("parallel", "arbitrary"), comm=comm)


def _ssd_bwd(xbc, dt_c, a_c, a_r, d_full, states, dy, *, d_inner, name, comm=None):
    L = xbc.shape[0]
    G, N, Q, P = SSD_N_GROUPS, SSD_D_STATE, SSD_CHUNK, SSD_HEAD_DIM
    KP = d_inner // G
    K = KP // P
    nc = L // Q
    npair = KP // LANE

    def body(xs_ref, b_ref, c_ref, dtc_ref, ac_ref, ar_ref, df_ref, st_ref, dy_ref,
             dxs_ref, db_ref, dc_ref, da_ref, dar_ref, s1_ref, dd_ref, dS):
        @pl.when(pl.program_id(1) == 0)
        def _():
            dS[...] = jnp.zeros_like(dS)
            dd_ref[...] = jnp.zeros_like(dd_ref)

        lower, _ = _tri(Q)
        a_c, a_r = ac_ref[...], ar_ref[...]
        a_f, dt_f = _expand_heads(a_c, K), _expand_heads(dtc_ref[...], K)
        xs = xs_ref[...]
        Bm, Cm = b_ref[...], c_ref[...]
        Bb, Cb = Bm.astype(BF16), Cm.astype(BF16)
        dY = dy_ref[...]
        X = xs * dt_f
        ea = jnp.exp(a_f)
        alast = a_f[Q - 1:Q, :]
        tail = jnp.exp(alast - a_f)
        el = jnp.exp(alast)
        Sv, dSn = st_ref[...], dS[...]
        Sb, dSb = Sv.astype(BF16), dSn.astype(BF16)
        cb = _nt(Cb, Bb)
        yoff_raw = _nn(Cb, Sb)
        dYe = dY * ea
        dC = _nt(dYe.astype(BF16), Sb)
        dS[...] = dSn * el + _nn(Cm.T.astype(BF16), dYe.astype(BF16))
        Gx = _nn(Bb, dSb)
        dB = _nt((X * tail).astype(BF16), dSb)
        dtl = Gx * X * tail
        da_f = dYe * yoff_raw - dtl
        dalast_f = _csum(dtl) + _csum(dSn * Sv) * el
        da_c = _head_sum(da_f, K, KP)
        onek = lax.broadcasted_iota(jnp.int32, (1, K), 1)
        onek_col = lax.broadcasted_iota(jnp.int32, (K, 1), 0)
        masks = _half_masks()
        dcb = jnp.zeros((Q, Q), F32)
        da_r = jnp.zeros((K, Q), F32)
        dX_parts = []
        for pr in range(npair):
            Xp = X[:, pr * LANE:(pr + 1) * LANE]
            dYp = dY[:, pr * LANE:(pr + 1) * LANE]
            dXp = Gx[:, pr * LANE:(pr + 1) * LANE] * tail[:, pr * LANE:(pr + 1) * LANE]
            for hh in range(2):
                k = 2 * pr + hh
                Xk = (Xp * masks[hh]).astype(BF16)
                dYk = (dYp * masks[hh]).astype(BF16)
                seg = a_c[:, k:k + 1] - a_r[k:k + 1, :]
                dec = jnp.where(lower, jnp.exp(jnp.minimum(seg, 0.0)), 0.0)
                Mk = cb * dec
                dM = _nt(dYk, Xk)
                dcb = dcb + dM * dec
                Gk = dM * Mk
                da_c = da_c + jnp.sum(Gk, axis=1, keepdims=True) * (onek == k).astype(F32)
                da_r = da_r + (onek_col == k).astype(F32) * jnp.sum(Gk, axis=0, keepdims=True)
                dXp = dXp + _tn(Mk.astype(BF16), dYk)
            dX_parts.append(dXp)
        dX = jnp.concatenate(dX_parts, axis=1) if npair > 1 else dX_parts[0]
        dcbb = dcb.astype(BF16)
        dC = dC + _nn(dcbb, Bb)
        dB = dB + _tn(dcbb, Cb)
        lastrow = (lax.broadcasted_iota(jnp.int32, (Q, 1), 0) == Q - 1).astype(F32)
        da_ref[...] = da_c + lastrow * _head_sum(dalast_f, K, KP)
        dar_ref[...] = da_r
        s1_ref[...] = _head_sum(dX * xs, K, KP)
        dd_ref[...] += _csum(dY * xs)
        dxs_ref[...] = dX * dt_f + dY * df_ref[...]
        db_ref[...] = dB
        dc_ref[...] = dC

    rc = lambda c: nc - 1 - c
    tok = jax.ShapeDtypeStruct((G, L, K), F32)
    tok_spec = pl.BlockSpec((None, Q, K), lambda g, c: (g, rc(c), 0))
    return _pcall(
        body, (xbc, xbc, xbc, dt_c, a_c, a_r, d_full, states, dy), name=name,
        out_shape=[jax.ShapeDtypeStruct((L, d_inner), F32), jax.ShapeDtypeStruct((L, G * N), F32),
                   jax.ShapeDtypeStruct((L, G * N), F32), tok, jax.ShapeDtypeStruct((G, K, L), F32), tok,
                   jax.ShapeDtypeStruct((1, d_inner), F32)],
        grid=(G, nc),
        in_specs=_ssd_specs(K, KP, d_inner, True, nc) + [
            pl.BlockSpec((None, None, N, KP), lambda g, c: (g, rc(c), 0, 0)),
            pl.BlockSpec((Q, KP), lambda g, c: (rc(c), g))],
        out_specs=[pl.BlockSpec((Q, KP), lambda g, c: (rc(c), g)), pl.BlockSpec((Q, N), lambda g, c: (rc(c), g)),
                   pl.BlockSpec((Q, N), lambda g, c: (rc(c), g)), tok_spec,
                   pl.BlockSpec((None, K, Q), lambda g, c: (g, 0, rc(c))), tok_spec,
                   pl.BlockSpec((1, KP), lambda g, c: (0, g))],
        scratch=[pltpu.VMEM((N, KP), F32)],
        sem=("parallel", "arbitrary"), comm=comm)


def _slopes():
    n = DIL_N_GROUPS * DIL_HEADS
    s = 2.0 ** (-8.0 * np.arange(1, n + 1) / n)
    return s.reshape(DIL_N_GROUPS, DIL_HEADS).astype(np.float32)


ATT_TB = 2048


def _tn(a, b):
    return lax.dot_general(a, b, (((0,), (0,)), ((), ())), preferred_element_type=F32)


def _slope_rows(g):
    return jnp.asarray(np.repeat(_slopes()[g][:, None], LANE, axis=1))


def _attn_bias(slope_row, d):
    B = DIL_BLOCK
    qi = lax.broadcasted_iota(jnp.int32, (B, B), 0)
    kj = lax.broadcasted_iota(jnp.int32, (B, B), 1)
    sd = slope_row * float(d)
    cur = jnp.where(kj <= qi, -(qi - kj).astype(F32) * sd, NEG)
    prv = jnp.where(kj >= qi, -(qi + B - kj).astype(F32) * sd, NEG)
    return cur, prv


def _sub_rows(j, r, d):
    base = j * DIL_BLOCK * d + r
    return pl.ds(base, DIL_BLOCK, stride=d) if d > 1 else pl.ds(base, DIL_BLOCK)


def _attn_geometry(L, g):
    window, d = DIL_PATTERNS[g]
    tb = min(ATT_TB, L)
    assert window // d == DIL_BLOCK and tb % (d * DIL_BLOCK) == 0 and L % tb == 0
    return d, tb, L // tb, tb // (d * DIL_BLOCK)


def _kv_specs(g, tb, nb):
    E, nh = DIL_HEAD_DIM, DIL_N_GROUPS * DIL_HEADS
    prev = lambda b: jnp.maximum(b - 1, 0)
    return [pl.BlockSpec((tb, E), lambda b, h: (b, g * DIL_HEADS + h)),
            pl.BlockSpec((tb, E), lambda b, h: (b, nh + g * DIL_HEADS + h)),
            pl.BlockSpec((tb, E), lambda b, h: (prev(b), g * DIL_HEADS + h)),
            pl.BlockSpec((tb, E), lambda b, h: (prev(b), nh + g * DIL_HEADS + h))]


def _attn_fwd(qz, kv, g, *, name):
    L = qz.shape[0]
    d, tb, nb, nj = _attn_geometry(L, g)
    B, E = DIL_BLOCK, DIL_HEAD_DIM
    scale = E ** -0.5

    def body(sl_ref, q_ref, kc_ref, vc_ref, kp_ref, vp_ref, o_ref, lse_ref):
        b, h = pl.program_id(0), pl.program_id(1)

        @pl.when(h == 0)
        def _():
            lse_ref[...] = jnp.zeros_like(lse_ref)

        bias_c, bias_p = _attn_bias(sl_ref[pl.ds(h, 1), :], d)
        bias_p0 = jnp.where(b > 0, bias_p, NEG)
        oneh = (lax.broadcasted_iota(jnp.int32, (1, LANE), 1) == h).astype(F32)
        for r in range(d):
            for j in range(nj):
                rows = _sub_rows(j, r, d)
                qs = q_ref[rows, :].astype(BF16)
                kc, vc = kc_ref[rows, :].astype(BF16), vc_ref[rows, :].astype(BF16)
                if j > 0:
                    prows, bp = _sub_rows(j - 1, r, d), bias_p
                    kq, vq = kc_ref[prows, :].astype(BF16), vc_ref[prows, :].astype(BF16)
                else:
                    prows, bp = _sub_rows(nj - 1, r, d), bias_p0
                    kq, vq = kp_ref[prows, :].astype(BF16), vp_ref[prows, :].astype(BF16)
                sc = _nt(qs, kc) * scale + bias_c
                sp = _nt(qs, kq) * scale + bp
                m = jnp.maximum(jnp.max(sc, axis=1, keepdims=True), jnp.max(sp, axis=1, keepdims=True))
                pc, pp = jnp.exp(sc - m), jnp.exp(sp - m)
                den = jnp.sum(pc, axis=1, keepdims=True) + jnp.sum(pp, axis=1, keepdims=True)
                o = _nn(pc.astype(BF16), vc) + _nn(pp.astype(BF16), vq)
                o_ref[rows, :] = o / den
                lse_ref[rows, :] = lse_ref[rows, :] + (m + jnp.log(den)) * oneh

    return _pcall(
        body, (_slope_rows(g), qz, kv, kv, kv, kv), name=name,
        out_shape=[jax.ShapeDtypeStruct((L, DIL_W), F32), jax.ShapeDtypeStruct((L, LANE), F32)],
        grid=(nb, DIL_HEADS),
        in_specs=[pl.BlockSpec((DIL_HEADS, LANE), lambda b, h: (0, 0)),
                  pl.BlockSpec((tb, E), lambda b, h: (b, g * DIL_HEADS + h))] + _kv_specs(g, tb, nb),
        out_specs=[pl.BlockSpec((tb, E), lambda b, h: (b, h)), pl.BlockSpec((tb, LANE), lambda b, h: (b, 0))],
        sem=("parallel", "arbitrary"))


def _attn_bwd(qz, kv, do, lse, dl, dqz, dk, dv, g, *, name):
    L = qz.shape[0]
    d, tb, nb, nj = _attn_geometry(L, g)
    B, E = DIL_BLOCK, DIL_HEAD_DIM
    scale = E ** -0.5
    nxt = lambda b: jnp.minimum(b + 1, nb - 1)
    fresh = dk is None

    def body(sl_ref, qc_ref, qn_ref, kc_ref, vc_ref, kp_ref, vp_ref, doc_ref, don_ref, lsec_ref, lsen_ref,
             dlc_ref, dln_ref, *rest):
        dq_ref, dk_ref, dv_ref = rest[-3:]
        b, h = pl.program_id(0), pl.program_id(1)
        bias_c, bias_p = _attn_bias(sl_ref[pl.ds(h, 1), :], d)
        bias_first = jnp.where(b > 0, bias_p, NEG)
        bias_last = jnp.where(b < nb - 1, bias_p, NEG)
        oneh = (lax.broadcasted_iota(jnp.int32, (1, LANE), 1) == h).astype(F32)

        def col(ref, rows):
            return jnp.sum(ref[rows, :] * oneh, axis=1, keepdims=True)

        def pair(q, do_, lse_, dl_, k_, v_, bias):
            p = jnp.exp(_nt(q, k_) * scale + bias - lse_)
            ds = p * (_nt(do_, v_) - dl_)
            return p.astype(BF16), ds.astype(BF16)

        for r in range(d):
            rows0 = _sub_rows(0, r, d)
            qj, doj = qc_ref[rows0, :].astype(BF16), doc_ref[rows0, :].astype(BF16)
            lsej, dlj = col(lsec_ref, rows0), col(dlc_ref, rows0)
            prows = _sub_rows(nj - 1, r, d)
            kq, vq = kp_ref[prows, :].astype(BF16), vp_ref[prows, :].astype(BF16)
            _, ds = pair(qj, doj, lsej, dlj, kq, vq, bias_first)
            dq_carry = _nn(ds, kq)
            for j in range(nj):
                rows = _sub_rows(j, r, d)
                kj, vj = kc_ref[rows, :].astype(BF16), vc_ref[rows, :].astype(BF16)
                p, ds = pair(qj, doj, lsej, dlj, kj, vj, bias_c)
                dq_ref[rows, :] = (dq_carry + _nn(ds, kj)) * scale
                dkj, dvj = _tn(ds, qj), _tn(p, doj)
                if j < nj - 1:
                    nrows = _sub_rows(j + 1, r, d)
                    qn, don = qc_ref[nrows, :].astype(BF16), doc_ref[nrows, :].astype(BF16)
                    lsen, dln, bias = col(lsec_ref, nrows), col(dlc_ref, nrows), bias_p
                else:
                    qn, don = qn_ref[rows0, :].astype(BF16), don_ref[rows0, :].astype(BF16)
                    lsen, dln, bias = col(lsen_ref, rows0), col(dln_ref, rows0), bias_last
                p2, ds2 = pair(qn, don, lsen, dln, kj, vj, bias)
                dk_ref[rows, :] = (dkj + _tn(ds2, qn)) * scale
                dv_ref[rows, :] = dvj + _tn(p2, don)
                dq_carry = _nn(ds2, kj)
                qj, doj, lsej, dlj = qn, don, lsen, dln

    hb = lambda b, h: (b, g * DIL_HEADS + h)
    anyspec = pl.BlockSpec(memory_space=pl.ANY)
    args = [_slope_rows(g), qz, qz, kv, kv, kv, kv, do, do, lse, lse, dl, dl, dqz] + ([] if fresh else [dk, dv])
    in_specs = ([pl.BlockSpec((DIL_HEADS, LANE), lambda b, h: (0, 0)),
                 pl.BlockSpec((tb, E), hb), pl.BlockSpec((tb, E), lambda b, h: (nxt(b), g * DIL_HEADS + h))]
                + _kv_specs(g, tb, nb)
                + [pl.BlockSpec((tb, E), lambda b, h: (b, h)), pl.BlockSpec((tb, E), lambda b, h: (nxt(b), h)),
                   pl.BlockSpec((tb, LANE), lambda b, h: (b, 0)), pl.BlockSpec((tb, LANE), lambda b, h: (nxt(b), 0)),
                   pl.BlockSpec((tb, LANE), lambda b, h: (b, 0)), pl.BlockSpec((tb, LANE), lambda b, h: (nxt(b), 0)),
                   anyspec] + ([] if fresh else [anyspec, anyspec]))
    aliases = {13: 0} if fresh else {13: 0, 14: 1, 15: 2}
    dkv_sd = jax.ShapeDtypeStruct((L, DIL_N_GROUPS * DIL_W), F32)
    return pl.pallas_call(
        body, name=name,
        out_shape=[jax.ShapeDtypeStruct(dqz.shape, F32), dkv_sd, dkv_sd],
        grid=(nb, DIL_HEADS), in_specs=in_specs,
        out_specs=[pl.BlockSpec((tb, E), hb), pl.BlockSpec((tb, E), hb), pl.BlockSpec((tb, E), hb)],
        input_output_aliases=aliases,
        compiler_params=_cp(("parallel", "parallel")))(*args)


def _head_expand():
    r = lax.broadcasted_iota(jnp.int32, (LANE, DIL_W), 0)
    c = lax.broadcasted_iota(jnp.int32, (LANE, DIL_W), 1)
    E = ((c >= r * DIL_HEAD_DIM) & (c < (r + 1) * DIL_HEAD_DIM)).astype(F32)
    r2 = lax.broadcasted_iota(jnp.int32, (DIL_W, LANE), 0)
    c2 = lax.broadcasted_iota(jnp.int32, (DIL_W, LANE), 1)
    Et = ((r2 >= c2 * DIL_HEAD_DIM) & (r2 < (c2 + 1) * DIL_HEAD_DIM)).astype(F32)
    return E, Et


def _merge_weights(l0, l1, l2):
    m = jnp.maximum(jnp.maximum(l0, l1), l2)
    e = [jnp.exp(l - m) for l in (l0, l1, l2)]
    tot = e[0] + e[1] + e[2]
    return [v / tot for v in e]


def _merge_fwd(os_, lses, qz, z_off, *, name):
    def fn(o0, o1, o2, l0, l1, l2, z):
        E, _ = _head_expand()
        w = _merge_weights(l0, l1, l2)
        om = sum(_dot2(wg, E) * og for wg, og in zip(w, (o0, o1, o2)))
        return [om * _silu(z)], []

    rows = [(o, 0, DIL_W) for o in os_] + [(l, 0, LANE) for l in lses] + [(qz, z_off, DIL_W)]
    return _rowmap(fn, rows, [], [(DIL_W, BF16)], [], name=name)[0]


def _merge_bwd(os_, lses, qz, z_off, dog, *, name, comm=None):
    def fn(o0, o1, o2, l0, l1, l2, z, dg):
        E, Et = _head_expand()
        dg = dg.astype(F32)
        w = _merge_weights(l0, l1, l2)
        wf = [_dot2(wg, E) for wg in w]
        os3 = (o0, o1, o2)
        om = sum(a * b for a, b in zip(wf, os3))
        dom = dg * _silu(z)
        dz = dg * om * _dsilu(z)
        dw = [_dot2(dom * og, Et) for og in os3]
        tot = sum(a * b for a, b in zip(w, dw))
        return [wf[0] * dom, wf[1] * dom, wf[2] * dom, w[0] * tot, w[1] * tot, w[2] * tot, dz], []

    rows = ([(o, 0, DIL_W) for o in os_] + [(l, 0, LANE) for l in lses] + [(qz, z_off, DIL_W), (dog, 0, DIL_W)])
    outs = [(DIL_W, F32)] * 3 + [(LANE, F32)] * 3 + [(qz.shape[1], F32, z_off, DIL_W)]
    return _rowmap(fn, rows, [], outs, [], name=name, comm=comm)


def _adamw(gparts, w, m, v, *, name, tr=128):
    n, R, C = gparts.shape
    tr = _pick(R, tr)
    c1 = 1.0 - ADAM_B1 ** ADAM_STEP
    c2 = 1.0 - ADAM_B2 ** ADAM_STEP

    def body(g_ref, w_ref, m_ref, v_ref, go_ref, d_ref, mo_ref, vo_ref):
        g = g_ref[0].astype(F32)
        for i in range(1, n):
            g = g + g_ref[i].astype(F32)
        mn = ADAM_B1 * m_ref[...] + (1.0 - ADAM_B1) * g
        vn = ADAM_B2 * v_ref[...] + (1.0 - ADAM_B2) * jnp.square(g)
        d_ref[...] = -ADAM_LR * ((mn / c1) / (jnp.sqrt(vn / c2) + ADAM_EPS) + ADAM_WD * w_ref[...])
        go_ref[...] = g
        mo_ref[...] = mn
        vo_ref[...] = vn

    blk = pl.BlockSpec((tr, C), lambda i: (i, 0))
    sd = jax.ShapeDtypeStruct((R, C), F32)
    return pl.pallas_call(
        body, name=name, out_shape=[sd, sd, sd, sd], grid=(R // tr,),
        in_specs=[pl.BlockSpec((n, tr, C), lambda i: (0, i, 0)), blk, blk, blk],
        out_specs=[blk, blk, blk, blk],
        compiler_params=_cp(("parallel",)),
    )(gparts, w, m, v)


def _sum_parts(parts, *, name):
    n, R, C = parts.shape

    def body(p_ref, o_ref):
        s = p_ref[0]
        for i in range(1, n):
            s = s + p_ref[i]
        o_ref[...] = s

    return pl.pallas_call(
        body, name=name, out_shape=jax.ShapeDtypeStruct((R, C), F32),
        in_specs=[pl.BlockSpec(memory_space=pltpu.VMEM)], out_specs=pl.BlockSpec(memory_space=pltpu.VMEM),
    )(parts)


def _cols_from(g):
    _, R, Cs = g.shape
    return jnp.transpose(g, (1, 0, 2)).reshape(R, N_DEV * Cs)


def _col_parts(dw):
    R, C = dw.shape
    return jnp.transpose(dw.reshape(R, N_DEV, C // N_DEV), (1, 0, 2))


def _ag_cols(w_loc):
    if w_loc.shape[1] % LANE == 0:
        return _ag_comm(w_loc, cols=True), (lambda g: g)
    return _ag_comm(w_loc), _cols_from


def _rs_cols(dw):
    if (dw.shape[1] // N_DEV) % LANE == 0:
        return _a2a_comm(dw, cols=True)
    return _a2a_comm(_col_parts(dw))


def kernel(x, c, ada_w, ada_b, ln_g, ln_b, a_in_w, a_conv_w, a_conv_b, a_dt_bias, a_A_log, a_D, a_norm_g, a_out_w, kv_w, b_in_w, b_out_w, loss_target, m_ada_w, m_ada_b, m_ln_g, m_ln_b, m_a_in_w, m_a_conv_w, m_a_conv_b, m_a_dt_bias, m_a_A_log, m_a_D, m_a_norm_g, m_a_out_w, m_kv_w, m_b_in_w, m_b_out_w, v_ada_w, v_ada_b, v_ln_g, v_ln_b, v_a_in_w, v_a_conv_w, v_a_conv_b, v_a_dt_bias, v_a_A_log, v_a_D, v_a_norm_g, v_a_out_w, v_kv_w, v_b_in_w, v_b_out_w):
    L, D = x.shape[1], x.shape[2]
    H = a_dt_bias.shape[1]
    d_inner = H * SSD_HEAD_DIM
    G, N, P = SSD_N_GROUPS, SSD_D_STATE, SSD_HEAD_DIM
    K = H // G
    KP = K * P
    conv_dim = d_inner + 2 * G * N
    in_dim = d_inner + conv_dim + H
    in_pad = d_inner + conv_dim + LANE
    assert H <= LANE and KP % LANE == 0 and L % SSD_CHUNK == 0
    me = 4 * lax.axis_index("x") + 2 * lax.axis_index("y") + lax.axis_index("c")
    x2d, tgt = x[0], loss_target[0]

    c_all = _all_gather(c, "ag_c").reshape(N_DEV, D)
    mods = []
    for l in range(DEPTH):
        ab = lax.dynamic_slice(ada_b[l], (me * (3 * D // N_DEV),), (3 * D // N_DEV,))[None]
        mods.append(_matmul(c_all, ada_w[l], name=f"mod{l}", exact=True, a_silu=True, bias=ab))
    mod_all = _all_gather(jnp.stack(mods), "ag_mod")
    mod_me = lax.dynamic_index_in_dim(jnp.transpose(mod_all, (2, 1, 0, 3)).reshape(N_DEV, DEPTH, 3 * D), me, 0, False)
    shift = [mod_me[l, None, 0:D] for l in range(DEPTH)]
    scale = [mod_me[l, None, D:2 * D] for l in range(DEPTH)]
    gate = [mod_me[l, None, 2 * D:3 * D] for l in range(DEPTH)]

    w_in = _cols_from(_all_gather(a_in_w[0].astype(BF16), "ag_a_in"))
    w_in = jnp.pad(w_in, ((0, 0), (0, in_pad - in_dim)))
    conv_w = _all_gather(a_conv_w[0], "ag_conv_w")
    conv_w = jnp.transpose(conv_w, (1, 0, 2)).reshape(SSD_CONV_W, conv_dim)
    conv_b = _all_gather(a_conv_b, "ag_conv_b").reshape(1, conv_dim)
    norm_g = _all_gather(a_norm_g, "ag_norm_g").reshape(1, d_inner)

    def modulate(xin, l, name):
        fn = lambda xv, sc, sh: ([xv * (1.0 + sc) + sh], [])
        return _rowmap(fn, [(xin, 0, D)], [scale[l], shift[l]], [(D, BF16)], [], name=name)[0]

    def ln_out(xv, yv, gt, g, b):
        u = DEEPNORM_ALPHA * xv + (1.0 + gt) * yv
        mu = jnp.mean(u, axis=1, keepdims=True)
        uc = u - mu
        var = jnp.mean(uc * uc, axis=1, keepdims=True)
        return uc * lax.rsqrt(var + LN_EPS) * g + b

    def ln0_fwd(xin, y, name):
        def fn(xv, yv, gt, g, b, sc, sh):
            o = ln_out(xv, yv, gt, g, b)
            return [o, o, o * (1.0 + sc) + sh], []
        return _rowmap(fn, [(xin, 0, D), (y, 0, D)], [gate[0], ln_g[0:1], ln_b[0:1], scale[1], shift[1]],
                       [(D, F32), (D, BF16), (D, BF16)], [], name=name)

    def ln1_loss(xin, y, target, name):
        def fn(xv, yv, tv, gt, g, b):
            e = ln_out(xv, yv, gt, g, b) - tv
            return [e * (1.0 / D)], [_csum(e * e) * (0.5 / D)]
        return _rowmap(fn, [(xin, 0, D), (y, 0, D), (target, 0, D)], [gate[1], ln_g[1:2], ln_b[1:2]],
                       [(D, F32)], [(1, D)], name=name)

    def ln_bwd(xin, y, dout, l, name):
        def fn(xv, yv, do, gt, g, b):
            u = DEEPNORM_ALPHA * xv + (1.0 + gt) * yv
            mu = jnp.mean(u, axis=1, keepdims=True)
            uc = u - mu
            var = jnp.mean(uc * uc, axis=1, keepdims=True)
            rs = lax.rsqrt(var + LN_EPS)
            xh = uc * rs
            dxh = do * g
            du = rs * (dxh - jnp.mean(dxh, axis=1, keepdims=True) - xh * jnp.mean(dxh * xh, axis=1, keepdims=True))
            return [DEEPNORM_ALPHA * du, (1.0 + gt) * du], [_csum(du * yv), _csum(do * xh), _csum(do)]
        return _rowmap(fn, [(xin, 0, D), (y, 0, D), (dout, 0, D)], [gate[l], ln_g[l:l + 1], ln_b[l:l + 1]],
                       [(D, F32), (D, BF16)], [(1, D)] * 3, name=name)

    def mod_bwd(xin, dh, dx_acc, l, name):
        def fn(xv, dhv, dxa, sc):
            return [dxa + dhv * (1.0 + sc)], [_csum(dhv * xv), _csum(dhv)]
        return _rowmap(fn, [(xin, 0, D), (dh, 0, D), (dx_acc, 0, D)], [scale[l]], [(D, F32)], [(1, D)] * 2, name=name)

    h0 = modulate(x2d, 0, "mod_h0")
    proj, g_aout = _matmul(h0, w_in, name="mm_a_in", tm=2048, tn=1152,
                           comm=_ag_comm(a_out_w[0].astype(BF16)))
    w_aout = g_aout.reshape(d_inner, D)
    xbc = _conv_fwd(proj, d_inner, conv_dim, conv_w, conv_b, name="conv_fwd")
    dt_raw = proj[:, d_inner + conv_dim:]
    padh = lambda a: jnp.pad(a, ((0, 0), (0, LANE - H)))
    bias_p, alog_p = padh(a_dt_bias), padh(a_A_log)
    dt_p, a_p = _ssd_prep(dt_raw, bias_p, alog_p, name="ssd_prep")
    dt_c = jnp.transpose(dt_p[:, :H].reshape(L, G, K), (1, 0, 2))
    a_c = jnp.transpose(a_p[:, :H].reshape(L, G, K), (1, 0, 2))
    a_r = jnp.transpose(a_c, (0, 2, 1))
    d_full = jnp.repeat(a_D.reshape(H), P)[None]
    ssd_in = (xbc, dt_c, a_c, a_r, d_full)
    cm_kv, fix_kv = _ag_cols(kv_w.astype(BF16))
    y_ssd, states, w_kv = _ssd_fwd(*ssd_in, d_inner=d_inner, name="ssd_fwd", comm=cm_kv)
    w_kv = fix_kv(w_kv)

    gw = d_inner // G

    def gnorm_fn(yv, zv, g):
        yg = yv * _silu(zv)
        r = lax.rsqrt(jnp.mean(yg * yg, axis=1, keepdims=True) + RMS_EPS)
        return [yg * r * g], []
    yn = _rowmap(gnorm_fn, [(y_ssd, 0, d_inner), (proj, 0, d_inner)], [norm_g], [(d_inner, BF16)], [],
                 name="gnorm_fwd", cw=gw, tr=1024)[0]
    cm_bin, fix_bin = _ag_cols(b_in_w[0].astype(BF16))
    ya, w_bin = _matmul(yn, w_aout, name="mm_a_out", comm=cm_bin)
    w_bin = fix_bin(w_bin)
    x1, x1b, h1 = ln0_fwd(x2d, ya, "ln0_fwd")

    cm_bout, fix_bout = _ag_cols(b_out_w[0].astype(BF16))
    kv, w_bout = _matmul(x1b, w_kv, name="mm_kv", comm=cm_bout)
    w_bout = fix_bout(w_bout)
    qz = _matmul(h1, w_bin, name="mm_b_in")
    z_off = DIL_N_GROUPS * DIL_W
    os_, lses = [], []
    for g in range(DIL_N_GROUPS):
        o, lse = _attn_fwd(qz, kv, g, name=f"attn_fwd{g}")
        os_.append(o)
        lses.append(lse)
    og = _merge_fwd(os_, lses, qz, z_off, name="merge_fwd")
    yb = _matmul(og, w_bout, name="mm_b_out")
    dx2, loss_cols = ln1_loss(x1, yb, tgt, "ln1_loss")
    loss = lax.psum(jnp.sum(loss_cols), ("x", "y", "c"))

    dx1a, dyb, dgate1, dlng1, dlnb1 = ln_bwd(x1, yb, dx2, 1, "ln1_bwd")
    dw_bout = _matmul(og, dyb, name="mm_b_out_dw", ta=True, out_dtype=BF16)
    dog = _matmul(dyb, w_bout, name="mm_b_out_dx", tb=True, out_dtype=BF16)
    do0, do1, do2, dl0, dl1, dl2, dqz, r_bout = _merge_bwd(os_, lses, qz, z_off, dog, name="merge_bwd",
                                                           comm=_rs_cols(dw_bout))
    dk = dv = None
    for g, (do_g, dl_g) in enumerate(zip((do0, do1, do2), (dl0, dl1, dl2))):
        dqz, dk, dv = _attn_bwd(qz, kv, do_g, lses[g], dl_g, dqz, dk, dv, g, name=f"attn_bwd{g}")
    dw_bin = _matmul(h1, dqz, name="mm_b_in_dw", ta=True, out_dtype=BF16)
    dh1 = _matmul(dqz, w_bin, name="mm_b_in_dx", tb=True)
    dx1b, dscale1, dshift1 = mod_bwd(x1, dh1, dx1a, 1, "mod1_bwd")
    kw = DIL_N_GROUPS * DIL_W
    dw_k = _matmul(x1b, dk, name="mm_k_dw", ta=True, out_dtype=BF16, out_cols=(2 * kw, 0))
    dw_kv = _matmul(x1b, dv, name="mm_v_dw", ta=True, out_dtype=BF16, out_cols=(2 * kw, kw), out_buf=dw_k)
    dx1k = _matmul(dk, w_kv, name="mm_k_dx", tb=True, bias=dx1b)
    dx1 = _matmul(dv, w_kv, name="mm_v_dx", tb=True, bias=dx1k, b_koff=kw)

    dxa, dya, dgate0, dlng0, dlnb0 = ln_bwd(x2d, ya, dx1, 0, "ln0_bwd")
    dw_aout = _matmul(yn, dya, name="mm_a_out_dw", ta=True, out_dtype=BF16)
    dyn = _matmul(dya, w_aout, name="mm_a_out_dx", tb=True)

    def gnorm_bwd_fn(yv, zv, dn, g):
        sz = _silu(zv)
        yg = yv * sz
        r = lax.rsqrt(jnp.mean(yg * yg, axis=1, keepdims=True) + RMS_EPS)
        nrm = yg * r
        dnn = dn * g
        dyg = r * (dnn - nrm * jnp.mean(dnn * nrm, axis=1, keepdims=True))
        return [dyg * sz, dyg * yv * _dsilu(zv)], [_csum(dn * nrm)]
    dy_ssd, dproj, dnorm_g = _rowmap(gnorm_bwd_fn, [(y_ssd, 0, d_inner), (proj, 0, d_inner), (dyn, 0, d_inner)],
                                     [norm_g], [(d_inner, F32), (in_pad, BF16, 0, gw)], [(1, d_inner)],
                                     name="gnorm_bwd", cw=gw, tr=1024)
    dxs, dB, dC, da_t, dar_t, s1_t, dD_f, r_kv = _ssd_bwd(
        *ssd_in, states, dy_ssd, d_inner=d_inner, name="ssd_bwd", comm=_rs_cols(dw_kv))
    dproj, dconv_w, dconv_b, r_aout = _conv_bwd(
        proj, d_inner, conv_dim, conv_w, conv_b, (dxs, dB, dC), dproj, name="conv_bwd", tc=128,
        comm=_a2a_comm(dw_aout.reshape(N_DEV, d_inner // N_DEV, D)))
    tokp = lambda t: padh(jnp.transpose(t, (1, 0, 2)).reshape(L, H))
    dar_tok = padh(jnp.transpose(dar_t, (2, 0, 1)).reshape(L, H))
    dproj, ddt_bias_p, dA_log_p = _ssd_post(tokp(da_t), dar_tok, tokp(s1_t), dt_p, dt_raw, bias_p, alog_p,
                                            dproj, d_inner + conv_dim, name="ssd_post")
    ddt_bias, dA_log = ddt_bias_p[:, :H], dA_log_p[:, :H]
    dD = jnp.sum(dD_f.reshape(H, P), axis=1)[None]
    dw_in, r_bin = _matmul(h0, dproj, name="mm_a_in_dw", ta=True, out_dtype=BF16, tn=1152, comm=_rs_cols(dw_bin))
    dw_in = dw_in[:, :in_dim]
    cs_in = in_dim // N_DEV
    by_c = jnp.transpose(dw_in.reshape(D, N_DEV // 2, 2, cs_in), (2, 1, 0, 3))
    my_c = lax.axis_index("c")
    keep = lax.dynamic_index_in_dim(by_c, my_c, 0, False)
    give = lax.dynamic_index_in_dim(by_c, 1 - my_c, 0, False)
    got = _run_comm(_pair_comm(give), "rs_a_in_pair")[0]
    pair_sum = _rowmap(lambda a, b: ([a.astype(F32) + b.astype(F32)], []),
                       [(keep.reshape(-1, cs_in), 0, cs_in), (got.reshape(-1, cs_in), 0, cs_in)], [],
                       [(cs_in, BF16)], [], name="rs_a_in_add", tr=512)[0].reshape(N_DEV // 2, D, cs_in)
    dh0, r_in = _matmul(dproj, w_in, name="mm_a_in_dx", tb=True, tm=2048, tk=1152,
                        comm=_quad_comm(pair_sum))
    grad_x, dscale0, dshift0 = mod_bwd(x2d, dh0, dxa, 0, "mod0_bwd")

    dmod = jnp.concatenate([dshift0, dscale0, dgate0, dshift1, dscale1, dgate1], axis=1)
    pieces = [dmod, dlng0, dlng1, dlnb0, dlnb1, ddt_bias, dA_log, dD,
              dconv_w.reshape(1, -1), dconv_b, dnorm_g]
    sizes = [p.shape[1] for p in pieces]
    tot = sum(sizes)
    tot_pad = -(-tot // (8 * LANE)) * (8 * LANE)
    packed = jnp.pad(jnp.concatenate(pieces, axis=1), ((0, 0), (0, tot_pad - tot))).reshape(tot_pad // LANE, LANE)
    packed_all = _all_gather(packed, "ag_small")
    small = _sum_parts(packed_all, name="sum_small").reshape(tot_pad)
    offs = np.cumsum([0] + sizes)
    seg = lambda i: small[int(offs[i]):int(offs[i + 1])]
    g_ada_b = seg(0).reshape(DEPTH, 3 * D)
    g_ln_g = jnp.stack([seg(1), seg(2)])
    g_ln_b = jnp.stack([seg(3), seg(4)])
    g_dt_bias, g_A_log, g_D = seg(5)[None], seg(6)[None], seg(7)[None]
    cs = conv_dim // N_DEV
    g_conv_w = lax.dynamic_slice(seg(8).reshape(SSD_CONV_W, conv_dim), (0, me * cs), (SSD_CONV_W, cs))[None]
    g_conv_b = lax.dynamic_slice(seg(9), (me * cs,), (cs,))[None]
    ns = d_inner // N_DEV
    g_norm_g = lax.dynamic_slice(seg(10), (me * ns,), (ns,))[None]

    ms = 3 * D // N_DEV
    dmod_all = packed_all.reshape(N_DEV, tot_pad)[:, :DEPTH * 3 * D].reshape(N_DEV, DEPTH, 3 * D)
    dmod_cols = lax.dynamic_slice(dmod_all, (0, 0, me * ms), (N_DEV, DEPTH, ms))
    c_t = jnp.transpose(c_all)
    g_ada_w = jnp.stack([_matmul(c_t, dmod_cols[:, l], name=f"mm_ada_dw{l}", exact=True, a_silu=True)
                         for l in range(DEPTH)])[None]

    def upd(parts, w, m, v, name):
        shp = w.shape
        r2 = lambda a: a.reshape(-1, shp[-1])
        return [o.reshape(shp) for o in _adamw(parts, r2(w), r2(m), r2(v), name=name)]

    res = {}
    res["ada_w"] = upd(g_ada_w.reshape(1, -1, ms), ada_w, m_ada_w, v_ada_w, "adam_ada_w")
    res["a_in_w"] = upd(r_in, a_in_w, m_a_in_w, v_a_in_w, "adam_a_in")
    res["a_out_w"] = upd(r_aout, a_out_w, m_a_out_w, v_a_out_w, "adam_a_out")
    res["kv_w"] = upd(r_kv, kv_w, m_kv_w, v_kv_w, "adam_kv")
    res["b_in_w"] = upd(r_bin, b_in_w, m_b_in_w, v_b_in_w, "adam_b_in")
    res["b_out_w"] = upd(r_bout, b_out_w, m_b_out_w, v_b_out_w, "adam_b_out")

    small_names = ["ada_b", "ln_g", "ln_b", "a_conv_w", "a_conv_b", "a_dt_bias", "a_A_log", "a_D", "a_norm_g"]
    small_g = [g_ada_b, g_ln_g, g_ln_b, g_conv_w, g_conv_b, g_dt_bias, g_A_log, g_D, g_norm_g]
    small_w = [ada_b, ln_g, ln_b, a_conv_w, a_conv_b, a_dt_bias, a_A_log, a_D, a_norm_g]
    small_m = [m_ada_b, m_ln_g, m_ln_b, m_a_conv_w, m_a_conv_b, m_a_dt_bias, m_a_A_log, m_a_D, m_a_norm_g]
    small_v = [v_ada_b, v_ln_g, v_ln_b, v_a_conv_w, v_a_conv_b, v_a_dt_bias, v_a_A_log, v_a_D, v_a_norm_g]
    ssz = [int(np.prod(w.shape)) for w in small_w]
    stot = sum(ssz)
    spad = -(-stot // (8 * LANE)) * (8 * LANE)

    def pack(arrs, fill):
        flat = jnp.concatenate([a.reshape(-1) for a in arrs])
        return jnp.concatenate([flat, jnp.full((spad - stot,), fill, F32)]).reshape(spad // LANE, LANE)

    sres = _adamw(pack(small_g, 0.0)[None], pack(small_w, 0.0), pack(small_m, 0.0), pack(small_v, 1.0), name="adam_small")
    soffs = np.cumsum([0] + ssz)
    for i, nme in enumerate(small_names):
        res[nme] = [r.reshape(-1)[int(soffs[i]):int(soffs[i + 1])].reshape(small_w[i].shape) for r in sres]

    order = ["ada_w", "ada_b", "ln_g", "ln_b", "a_in_w", "a_conv_w", "a_conv_b", "a_dt_bias", "a_A_log", "a_D",
             "a_norm_g", "a_out_w", "kv_w", "b_in_w", "b_out_w"]
    outs = [loss, grad_x[None]]
    for j in range(4):
        outs += [res[nme][j] for nme in order]
    return tuple(outs)
```
